```python
import jax, jax.numpy as jnp
from jax import lax
import numpy as np

D_MODEL = 2048
BATCH = 2
SEQ = 16384
DEPTH = 1

POOL_WIDTH = D_MODEL // 2
POOL_WINDOWS = (2, 4, 8, 16)
POOL_GROUPS = len(POOL_WINDOWS)
POOL_GROUP_DIM = POOL_WIDTH // POOL_GROUPS
RET_HEADS = 8
RET_QK_DIM = D_MODEL // 16
RET_V_DIM = D_MODEL // 16
RET_QK_WIDTH = RET_HEADS * RET_QK_DIM
RET_V_WIDTH = RET_HEADS * RET_V_DIM
RET_CHUNK = 128
ROPE_BASE = 10000.0
N_GROUPS = 4
EXPERTS_PER_GROUP = 8
N_EXPERTS = N_GROUPS * EXPERTS_PER_GROUP
TOP_K = 2
D_EXPERT = D_MODEL // 2
MOE_BLOCK = 128
N_ADA = 6
EPS = 1e-6
IN_SPLITS = (POOL_WIDTH, RET_QK_WIDTH, RET_QK_WIDTH, RET_V_WIDTH, RET_V_WIDTH, D_MODEL, D_MODEL)
IN_WIDTH = sum(IN_SPLITS)
IN_OFFSETS = tuple(int(v) for v in np.cumsum(IN_SPLITS)[:-1])

kernel_name = "hybrid_pool_retention_hmoe_adaln"


def rms_norm(x, gain):
    xf = x.astype(jnp.float32)
    y = xf * lax.rsqrt(jnp.mean(xf * xf, axis=-1, keepdims=True) + EPS)
    return (y * gain.astype(jnp.float32)).astype(x.dtype)


def modulate(h, shift, scale):
    return h * (1.0 + scale[:, None, :]) + shift[:, None, :]


def rotary(t, positions):
    half = t.shape[-1] // 2
    inv_freq = ROPE_BASE ** (-jnp.arange(half, dtype=jnp.float32) / half)
    ang = positions.astype(jnp.float32)[..., None] * inv_freq
    cos = jnp.cos(ang)[:, :, None, :]
    sin = jnp.sin(ang)[:, :, None, :]
    t = t.astype(jnp.float32)
    t1, t2 = t[..., :half], t[..., half:]
    return jnp.concatenate([t1 * cos - t2 * sin, t2 * cos + t1 * sin], axis=-1)


def pool_mixer(a, w_pool, pool_scale):
    B, S, _ = a.shape
    af = a.astype(jnp.float32).reshape(B, S, POOL_GROUPS, POOL_GROUP_DIM)
    cs = jnp.cumsum(af, axis=1)
    t = jnp.arange(S)
    outs = []
    for g, w in enumerate(POOL_WINDOWS):
        csg = cs[:, :, g]
        lagged = jnp.pad(csg, ((0, 0), (w, 0), (0, 0)))[:, :S]
        count = jnp.minimum(t + 1, w).astype(jnp.float32)[None, :, None]
        pooled = (csg - lagged) / count - af[:, :, g]
        outs.append(jnp.einsum('bsc,cd->bsd', pooled.astype(a.dtype), w_pool[g]))
    return jnp.concatenate(outs, axis=-1) * pool_scale


def retention(q, k, v, g):
    B, S, H, dk = q.shape
    dv = v.shape[-1]
    C = RET_CHUNK
    N = S // C
    log_gamma = jnp.log1p(-jnp.exp2(-5.0 - jnp.arange(H, dtype=jnp.float32)))
    idx = jnp.arange(C, dtype=jnp.float32)
    diff = idx[:, None] - idx[None, :]
    decay_mask = jnp.where(diff >= 0, jnp.exp(log_gamma[:, None, None] * jnp.maximum(diff, 0.0)), 0.0)
    q_decay = jnp.exp(log_gamma[:, None] * (idx + 1.0))
    k_decay = jnp.exp(log_gamma[:, None] * (C - 1.0 - idx))
    chunk_decay = jnp.exp(log_gamma * C)

    def to_chunks(t):
        return t.astype(jnp.float32).reshape(B, N, C, H, -1).transpose(1, 0, 3, 2, 4)

    def step(state, inp):
        qc, kc, vc = inp
        scores = jnp.einsum('bhid,bhjd->bhij', qc, kc) * decay_mask
        o = jnp.einsum('bhij,bhje->bhie', scores, vc)
        o = o + jnp.einsum('bhid,bhde->bhie', qc * q_decay[:, :, None], state)
        state = state * chunk_decay[:, None, None] + jnp.einsum('bhjd,bhje->bhde', kc * k_decay[:, :, None], vc)
        return state, o

    state0 = jnp.zeros((B, H, dk, dv), jnp.float32)
    _, o = lax.scan(step, state0, (to_chunks(q), to_chunks(k), to_chunks(v)))
    o = o.transpose(1, 0, 3, 2, 4).reshape(B, S, H, dv)
    o = o * lax.rsqrt(jnp.mean(o * o, axis=-1, keepdims=True) + EPS)
    o = o.reshape(B, S, H * dv) * jax.nn.silu(g.astype(jnp.float32))
    return o.astype(g.dtype)


def hierarchical_moe(h, w_group, b_group, w_router, b_router, w1, w3, w2):
    B, S, D = h.shape
    xt = h.reshape(-1, D)
    T = xt.shape[0]
    group_logits = jnp.einsum('td,dg->tg', xt, w_group).astype(jnp.float32) + b_group
    group_probs = jax.nn.softmax(group_logits, axis=-1)
    grp = jnp.argmax(group_logits, axis=-1)
    p_grp = jnp.take_along_axis(group_probs, grp[:, None], axis=-1)[:, 0]
    exp_logits = (jnp.einsum('td,de->te', xt, w_router).astype(jnp.float32) + b_router)
    exp_logits = exp_logits.reshape(T, N_GROUPS, EXPERTS_PER_GROUP)
    exp_logits = jnp.take_along_axis(exp_logits, grp[:, None, None], axis=1)[:, 0]
    top_v, top_i = lax.top_k(exp_logits, TOP_K)
    gate_w = jax.nn.softmax(top_v, axis=-1) * p_grp[:, None]
    expert = grp[:, None] * EXPERTS_PER_GROUP + top_i

    flat_e = expert.reshape(-1)
    flat_w = gate_w.reshape(-1)
    flat_tok = jnp.repeat(jnp.arange(T), TOP_K)
    order = jnp.argsort(flat_e)
    e_s, tok_s, w_s = flat_e[order], flat_tok[order], flat_w[order]
    counts = jnp.bincount(flat_e, length=N_EXPERTS)
    padded = (counts + MOE_BLOCK - 1) // MOE_BLOCK * MOE_BLOCK
    start = jnp.cumsum(counts) - counts
    pend = jnp.cumsum(padded)
    pstart = pend - padded
    A = T * TOP_K
    dest = pstart[e_s] + jnp.arange(A) - start[e_s]
    P = (A + MOE_BLOCK - 1) // MOE_BLOCK * MOE_BLOCK + N_EXPERTS * MOE_BLOCK
    nb = P // MOE_BLOCK
    buf = jnp.zeros((P, D), h.dtype).at[dest].set(xt[tok_s])
    block_e = jnp.clip(jnp.searchsorted(pend, jnp.arange(nb) * MOE_BLOCK, side='right'), 0, N_EXPERTS - 1)

    def expert_block(args):
        xb, e = args
        hid = jax.nn.silu(xb @ w1[e]) * (xb @ w3[e])
        return hid @ w2[e]

    yb = lax.map(expert_block, (buf.reshape(nb, MOE_BLOCK, D), block_e)).reshape(P, D)
    y = jnp.zeros((T, D), h.dtype).at[tok_s].add(yb[dest] * w_s[:, None].astype(h.dtype))
    return y.reshape(B, S, D)


def setup_inputs(seed: int = 0) -> dict:
    key = jax.random.key(seed)
    ks = jax.random.split(key, 24)
    f32 = jnp.float32
    D, L = D_MODEL, DEPTH
    nrm = lambda k, shape, fan_in: jax.random.normal(k, shape, f32) * (fan_in ** -0.5)
    return {
        "x": jax.random.normal(ks[0], (BATCH, SEQ, D), f32),
        "c": jax.random.normal(ks[1], (BATCH, D), f32),
        "positions": jnp.broadcast_to(jnp.arange(SEQ, dtype=jnp.int32)[None, :], (BATCH, SEQ)),
        "w_ada": nrm(ks[2], (L, D, N_ADA * D), D) * 0.5,
        "b_ada": jax.random.normal(ks[3], (L, N_ADA * D), f32) * 0.01,
        "norm1_gain": 1.0 + 0.05 * jax.random.normal(ks[4], (L, D), f32),
        "w_in": nrm(ks[5], (L, D, IN_WIDTH), D),
        "w_pool": nrm(ks[6], (L, POOL_GROUPS, POOL_GROUP_DIM, POOL_GROUP_DIM), POOL_GROUP_DIM),
        "pool_scale": 1.0 + 0.05 * jax.random.normal(ks[7], (L, POOL_WIDTH), f32),
        "w_branch_pool": nrm(ks[8], (L, POOL_WIDTH, D), POOL_WIDTH),
        "w_branch_ret": nrm(ks[9], (L, RET_V_WIDTH, D), RET_V_WIDTH),
        "w_out": nrm(ks[10], (L, D, D), D),
        "norm2_gain": 1.0 + 0.05 * jax.random.normal(ks[11], (L, D), f32),
        "w_group": nrm(ks[12], (L, D, N_GROUPS), D),
        "b_group": jax.random.normal(ks[13], (L, N_GROUPS), f32) * 0.01,
        "w_router": nrm(ks[14], (L, D, N_EXPERTS), D),
        "b_router": jax.random.normal(ks[15], (L, N_EXPERTS), f32) * 0.01,
        "w1": nrm(ks[16], (L, N_EXPERTS, D, D_EXPERT), D),
        "w3": nrm(ks[17], (L, N_EXPERTS, D, D_EXPERT), D),
        "w2": nrm(ks[18], (L, N_EXPERTS, D_EXPERT, D), D_EXPERT),
        "final_gain": 1.0 + 0.05 * jax.random.normal(ks[19], (D,), f32),
    }


def reference(x, c, positions, w_ada, b_ada, norm1_gain, w_in, w_pool, pool_scale,
              w_branch_pool, w_branch_ret, w_out, norm2_gain, w_group, b_group,
              w_router, b_router, w1, w3, w2, final_gain):
    B, S, D = x.shape
    c_act = jax.nn.silu(c)
    for l in range(DEPTH):
        mod = (c_act @ w_ada[l] + b_ada[l]).reshape(B, N_ADA, D)
        shift1, scale1, gate1, shift2, scale2, gate2 = [mod[:, i] for i in range(N_ADA)]

        h = modulate(rms_norm(x, norm1_gain[l]), shift1, scale1)
        proj = jnp.einsum('bsd,de->bse', h, w_in[l])
        a, q, k, v, rg, ga, gb = jnp.split(proj, IN_OFFSETS, axis=-1)
        q = rotary(q.reshape(B, S, RET_HEADS, RET_QK_DIM), positions)
        k = rotary(k.reshape(B, S, RET_HEADS, RET_QK_DIM), positions) * (RET_QK_DIM ** -0.5)
        v = v.reshape(B, S, RET_HEADS, RET_V_DIM)
        y_pool = jnp.einsum('bsc,cd->bsd', pool_mixer(a, w_pool[l], pool_scale[l]), w_branch_pool[l])
        y_ret = jnp.einsum('bsc,cd->bsd', retention(q, k, v, rg), w_branch_ret[l])
        merged = jax.nn.sigmoid(ga) * y_pool + jax.nn.sigmoid(gb) * y_ret
        x = x + gate1[:, None, :] * jnp.einsum('bsd,de->bse', merged, w_out[l])

        h2 = modulate(rms_norm(x, norm2_gain[l]), shift2, scale2)
        y_moe = hierarchical_moe(h2, w_group[l], b_group[l], w_router[l], b_router[l], w1[l], w3[l], w2[l])
        x = x + gate2[:, None, :] * y_moe
    return rms_norm(x, final_gain)
```

```python
import functools

import jax
import jax.numpy as jnp
import numpy as np
from jax import lax
from jax.experimental import pallas as pl
from jax.experimental.pallas import tpu as pltpu

F32 = jnp.float32
BF16 = jnp.bfloat16

EPS = 1e-6
ROPE_BASE = 10000.0
POOL_WINDOWS = (2, 4, 8, 16)
POOL_HALO = 16
RET_HEADS = 8
RET_CHUNK = 128
N_GROUPS = 4
EXPERTS_PER_GROUP = 8
N_EXPERTS = N_GROUPS * EXPERTS_PER_GROUP
N_ADA = 6
LANES = 128
ROUTE_LANE0 = N_GROUPS
EXPERT_ROWS = 256
VMEM_LIMIT = 56 * 1024 * 1024


def _cparams(sem):
    return pltpu.CompilerParams(dimension_semantics=sem, vmem_limit_bytes=VMEM_LIMIT)


def _const_spec(shape):
    n = len(shape)
    return pl.BlockSpec(shape, lambda *_: (0,) * n, pipeline_mode=pl.Buffered(1))


def _ada_kernel(cb_ref, w_ref, b_ref, o_ref, cact_ref):
    cb = cb_ref[...]
    cact_ref[...] = cb * jax.nn.sigmoid(cb)
    tn = w_ref.shape[1]
    for b in range(cb_ref.shape[0]):
        for j in range(tn // LANES):
            sl = slice(j * LANES, (j + 1) * LANES)
            prod = w_ref[:, sl] * cact_ref[b]
            o_ref[b:b + 1, sl] = jnp.sum(prod, axis=0, keepdims=True) + b_ref[:, sl]


def _ada(c, w_ada, b_ada):
    B, D = c.shape
    N = w_ada.shape[1]
    tn = 1024
    cb = jnp.broadcast_to(c[:, :, None], (B, D, LANES))
    return pl.pallas_call(
        _ada_kernel,
        out_shape=jax.ShapeDtypeStruct((B, N), F32),
        grid=(N // tn,),
        in_specs=[
            pl.BlockSpec((B, D, LANES), lambda j: (0, 0, 0)),
            pl.BlockSpec((D, tn), lambda j: (0, j)),
            pl.BlockSpec((1, tn), lambda j: (0, j)),
        ],
        out_specs=pl.BlockSpec((B, tn), lambda j: (0, j)),
        scratch_shapes=[pltpu.VMEM((B, D, LANES), F32)],
        compiler_params=_cparams(("arbitrary",)),
        name="ada",
    )(cb, w_ada, b_ada)


def _norm_mod_rows(x, gain, shift, scale):
    ms = jnp.mean(x * x, axis=-1, keepdims=True)
    y = (x * lax.rsqrt(ms + EPS)) * gain
    return y * (1.0 + scale) + shift


def _inproj_kernel(x_ref, mod_ref, g_ref, w_ref, o_ref, h_ref):
    tm = x_ref.shape[0]
    rc = 64

    @pl.when(pl.program_id(1) == 0)
    def _():
        shift = mod_ref[0, 0:1, :]
        scale = mod_ref[0, 1:2, :]
        gain = g_ref[...]

        def body(c, carry):
            rows = pl.ds(pl.multiple_of(c * rc, rc), rc)
            h_ref[rows, :] = _norm_mod_rows(x_ref[rows, :], gain, shift, scale).astype(BF16)
            return carry

        lax.fori_loop(0, tm // rc, body, 0)

    o_ref[...] = jnp.dot(h_ref[...], w_ref[...], preferred_element_type=F32).astype(o_ref.dtype)


def _inproj(x2d, mod3, gain, w_in_bf, seq, tm, tn):
    T, D = x2d.shape
    N = w_in_bf.shape[1]
    tps = seq // tm
    return pl.pallas_call(
        _inproj_kernel,
        out_shape=jax.ShapeDtypeStruct((T, N), BF16),
        grid=(T // tm, N // tn),
        in_specs=[
            pl.BlockSpec((tm, D), lambda i, j: (i, 0)),
            pl.BlockSpec((1, N_ADA, D), lambda i, j: (i // tps, 0, 0)),
            pl.BlockSpec((1, D), lambda i, j: (0, 0)),
            pl.BlockSpec((D, tn), lambda i, j: (0, j)),
        ],
        out_specs=pl.BlockSpec((tm, tn), lambda i, j: (i, j)),
        scratch_shapes=[pltpu.VMEM((tm, D), BF16)],
        compiler_params=_cparams(("arbitrary", "arbitrary")),
        name="inproj",
    )(x2d, mod3, gain, w_in_bf)


def _ret_kernel(pos_ref, q_ref, k_ref, v_ref, rg_ref, invf_ref, sign_ref, mask_ref,
                qd_ref, kd_ref, cd_ref, o_ref, state_ref, *, n_chunks, k_scale):
    C = RET_CHUNK

    @pl.when(pl.program_id(1) == 0)
    def _():
        state_ref[...] = jnp.zeros_like(state_ref)

    for c in range(n_chunks):
        rows = slice(c * C, (c + 1) * C)
        ang = pos_ref[rows, :] * invf_ref[...]
        cosv = jnp.cos(ang)
        sinv = jnp.sin(ang) * sign_ref[...]
        for h in range(RET_HEADS):
            cols = slice(h * C, (h + 1) * C)
            q = q_ref[rows, cols].astype(F32)
            k = k_ref[rows, cols].astype(F32)
            qr = q * cosv + pltpu.roll(q, C // 2, 1) * sinv
            kr = (k * cosv + pltpu.roll(k, C // 2, 1) * sinv) * k_scale
            v = v_ref[rows, cols]
            s = lax.dot_general(qr.astype(BF16), kr.astype(BF16), (((1,), (1,)), ((), ())),
                                preferred_element_type=F32) * mask_ref[h]
            st = state_ref[h]
            o = jnp.dot(s.astype(BF16), v, preferred_element_type=F32)
            o = o + jnp.dot((qr * qd_ref[h]).astype(BF16), st.astype(BF16),
                            preferred_element_type=F32)
            kv = lax.dot_general((kr * kd_ref[h]).astype(BF16), v, (((0,), (0,)), ((), ())),
                                 preferred_element_type=F32)
            state_ref[h] = st * cd_ref[h] + kv
            o = o * lax.rsqrt(jnp.mean(o * o, axis=-1, keepdims=True) + EPS)
            g = rg_ref[rows, cols].astype(F32)
            o_ref[rows, cols] = (o * (g * jax.nn.sigmoid(g))).astype(o_ref.dtype)


def _retention(posf, proj, batch, seq, n_chunks):
    T = proj.shape[0]
    H, C = RET_HEADS, RET_CHUNK
    W = H * C
    ct = n_chunks * C
    spb = seq // ct
    half = C // 2
    inv_freq = ROPE_BASE ** (-jnp.arange(half, dtype=F32) / half)
    invf2 = jnp.concatenate([inv_freq, inv_freq])[None, :]
    sign = jnp.concatenate([-jnp.ones((half,), F32), jnp.ones((half,), F32)])[None, :]
    log_gamma = jnp.log1p(-jnp.exp2(-5.0 - jnp.arange(H, dtype=F32)))
    idx = jnp.arange(C, dtype=F32)
    diff = idx[:, None] - idx[None, :]
    mask = jnp.where(diff >= 0, jnp.exp(log_gamma[:, None, None] * jnp.maximum(diff, 0.0)), 0.0)
    q_decay = jnp.exp(log_gamma[:, None] * (idx + 1.0))
    k_decay = jnp.exp(log_gamma[:, None] * (C - 1.0 - idx))
    chunk_decay = jnp.exp(log_gamma * C)
    qd = jnp.broadcast_to(q_decay[:, :, None], (H, C, C))
    kd = jnp.broadcast_to(k_decay[:, :, None], (H, C, C))
    cd = jnp.broadcast_to(chunk_decay[:, None, None], (H, 1, C))

    def tok(col):
        return pl.BlockSpec((ct, W), lambda b, n, col=col: (b * spb + n, col))

    return pl.pallas_call(
        functools.partial(_ret_kernel, n_chunks=n_chunks, k_scale=float(C) ** -0.5),
        out_shape=jax.ShapeDtypeStruct((T, W), BF16),
        grid=(batch, spb),
        in_specs=[
            pl.BlockSpec((ct, 1), lambda b, n: (b * spb + n, 0)),
            tok(1), tok(2), tok(3), tok(4),
            pl.BlockSpec((1, C), lambda b, n: (0, 0)),
            pl.BlockSpec((1, C), lambda b, n: (0, 0)),
            pl.BlockSpec((H, C, C), lambda b, n: (0, 0, 0)),
            pl.BlockSpec((H, C, C), lambda b, n: (0, 0, 0)),
            pl.BlockSpec((H, C, C), lambda b, n: (0, 0, 0)),
            pl.BlockSpec((H, 1, C), lambda b, n: (0, 0, 0)),
        ],
        out_specs=pl.BlockSpec((ct, W), lambda b, n: (b * spb + n, 0)),
        scratch_shapes=[pltpu.VMEM((H, C, C), F32)],
        compiler_params=_cparams(("arbitrary", "arbitrary")),
        name="ret",
    )(posf, proj, proj, proj, proj, invf2, sign, mask, qd, kd, cd)


def _mix_kernel(a_ref, halo_ref, r_ref, ga0_ref, ga1_ref, gb0_ref, gb1_ref, x_ref, mod_ref,
                wpool_ref, pscale_ref, wbp_ref, wbr_ref, wout_ref, g2_ref,
                wrh_ref, wrl_ref, brt_ref,
                x1_ref, h2_ref, lg_ref, ext_ref, pm_ref, mg_ref, *, tiles_per_seq):
    tm = a_ref.shape[0]
    pw = a_ref.shape[1]
    gd = pw // len(POOL_WINDOWS)
    i = pl.program_id(0)
    it = i % tiles_per_seq
    halo = halo_ref[...].astype(F32)
    ext_ref[0:POOL_HALO, :] = jnp.where(it == 0, 0.0, halo)
    ext_ref[POOL_HALO:, :] = a_ref[...].astype(F32)
    tpos = it * tm + lax.broadcasted_iota(jnp.int32, (tm, 1), 0)
    for g, w in enumerate(POOL_WINDOWS):
        cols = slice(g * gd, (g + 1) * gd)
        cur = ext_ref[POOL_HALO:POOL_HALO + tm, cols]
        s = cur
        for j in range(1, w):
            s = s + ext_ref[POOL_HALO - j:POOL_HALO - j + tm, cols]
        cnt = jnp.minimum(tpos + 1, w).astype(F32)
        pooled = s / cnt - cur
        pm = jnp.dot(pooled.astype(BF16), wpool_ref[g], preferred_element_type=F32)
        pm_ref[:, cols] = (pm * pscale_ref[:, cols]).astype(BF16)

    y_pool = jnp.dot(pm_ref[...], wbp_ref[...], preferred_element_type=F32)
    y_ret = jnp.dot(r_ref[...], wbr_ref[...], preferred_element_type=F32)
    half = y_pool.shape[1] // 2
    for hh, (ga_ref, gb_ref) in enumerate(((ga0_ref, gb0_ref), (ga1_ref, gb1_ref))):
        cols = slice(hh * half, (hh + 1) * half)
        ga = jax.nn.sigmoid(ga_ref[...].astype(F32))
        gb = jax.nn.sigmoid(gb_ref[...].astype(F32))
        mg_ref[:, cols] = (ga * y_pool[:, cols] + gb * y_ret[:, cols]).astype(BF16)
    z = jnp.dot(mg_ref[...], wout_ref[...], preferred_element_type=F32)
    gate1 = mod_ref[0, 2:3, :]
    x1 = x_ref[...] + gate1 * z
    x1_ref[...] = x1
    h2 = _norm_mod_rows(x1, g2_ref[...], mod_ref[0, 3:4, :], mod_ref[0, 4:5, :])
    h2_ref[...] = h2
    h_hi = h2.astype(BF16)
    h_lo = (h2 - h_hi.astype(F32)).astype(BF16)
    lg = jnp.dot(h_hi, wrh_ref[...], preferred_element_type=F32)
    lg = lg + jnp.dot(h_lo, wrh_ref[...], preferred_element_type=F32)
    lg = lg + jnp.dot(h_hi, wrl_ref[...], preferred_element_type=F32)
    lg_ref[...] = lg + brt_ref[...]


def _mix(proj, r, x2d, mod3, wpool_bf, pscale, wbp_bf, wbr_bf, wout_bf, g2, wr_hi, wr_lo, brt,
         seq, tm):
    T, D = x2d.shape
    PW = wbp_bf.shape[0]
    tps = seq // tm
    hb = tm // POOL_HALO

    def tok(col):
        return pl.BlockSpec((tm, PW), lambda i, col=col: (i, col))

    kern = functools.partial(_mix_kernel, tiles_per_seq=tps)
    return pl.pallas_call(
        kern,
        out_shape=(jax.ShapeDtypeStruct((T, D), F32),
                   jax.ShapeDtypeStruct((T, D), F32),
                   jax.ShapeDtypeStruct((T, LANES), F32)),
        grid=(T // tm,),
        in_specs=[
            tok(0),
            pl.BlockSpec((POOL_HALO, PW), lambda i: (jnp.maximum(i * hb - 1, 0), 0)),
            pl.BlockSpec((tm, PW), lambda i: (i, 0)),
            tok(5), tok(6), tok(7), tok(8),
            pl.BlockSpec((tm, D), lambda i: (i, 0)),
            pl.BlockSpec((1, N_ADA, D), lambda i: (i // tps, 0, 0)),
            _const_spec(wpool_bf.shape),
            _const_spec(pscale.shape),
            _const_spec(wbp_bf.shape),
            _const_spec(wbr_bf.shape),
            _const_spec(wout_bf.shape),
            _const_spec(g2.shape),
            _const_spec(wr_hi.shape),
            _const_spec(wr_lo.shape),
            _const_spec(brt.shape),
        ],
        out_specs=(pl.BlockSpec((tm, D), lambda i: (i, 0)),
                   pl.BlockSpec((tm, D), lambda i: (i, 0)),
                   pl.BlockSpec((tm, LANES), lambda i: (i, 0))),
        scratch_shapes=[pltpu.VMEM((POOL_HALO + tm, PW), F32),
                        pltpu.VMEM((tm, PW), BF16),
                        pltpu.VMEM((tm, D), BF16)],
        compiler_params=_cparams(("arbitrary",)),
        name="mix",
    )(proj, proj, r, proj, proj, proj, proj, x2d, mod3, wpool_bf, pscale, wbp_bf, wbr_bf,
      wout_bf, g2, wr_hi, wr_lo, brt)


def _route_kernel(lg_ref, tri_ref, col_ref, row_ref, cnt_ref, carry_ref):
    tm = lg_ref.shape[0]

    @pl.when(pl.program_id(0) == 0)
    def _():
        carry_ref[...] = jnp.zeros_like(carry_ref)

    L = lg_ref[...]
    lane = lax.broadcasted_iota(jnp.int32, (tm, LANES), 1)
    neg = -jnp.inf
    is_g = lane < N_GROUPS
    gl = jnp.where(is_g, L, neg)
    gmax = jnp.max(gl, axis=1, keepdims=True)
    grp = jnp.min(jnp.where(gl == gmax, lane, LANES), axis=1, keepdims=True)
    gsum = jnp.sum(jnp.where(is_g, jnp.exp(gl - gmax), 0.0), axis=1, keepdims=True)
    p_grp = 1.0 / gsum
    lo = ROUTE_LANE0 + grp * EXPERTS_PER_GROUP
    el = jnp.where((lane >= lo) & (lane < lo + EXPERTS_PER_GROUP), L, neg)
    v1 = jnp.max(el, axis=1, keepdims=True)
    i1 = jnp.min(jnp.where(el == v1, lane, LANES), axis=1, keepdims=True)
    el2 = jnp.where(lane == i1, neg, el)
    v2 = jnp.max(el2, axis=1, keepdims=True)
    i2 = jnp.min(jnp.where(el2 == v2, lane, LANES), axis=1, keepdims=True)
    e = jnp.exp(v2 - v1)
    w1 = p_grp / (1.0 + e)
    w2 = p_grp * e / (1.0 + e)
    sel1 = lane == i1
    sel2 = lane == i2
    onehot = jnp.where(sel1 | sel2, 1.0, 0.0)
    cum = jnp.dot(tri_ref[...], onehot.astype(BF16), preferred_element_type=F32)
    cum = cum + carry_ref[0:1, :]
    r1 = jnp.sum(jnp.where(sel1, cum, 0.0), axis=1, keepdims=True)
    r2 = jnp.sum(jnp.where(sel2, cum, 0.0), axis=1, keepdims=True)
    carry_ref[...] = carry_ref[...] + jnp.sum(onehot, axis=0, keepdims=True)
    cnt_ref[...] = carry_ref[...]
    slab = jnp.where(lane == 0, (i1 - ROUTE_LANE0).astype(F32), 0.0)
    slab = jnp.where(lane == 1, (i2 - ROUTE_LANE0).astype(F32), slab)
    slab = jnp.where(lane == 2, r1, slab)
    slab = jnp.where(lane == 3, r2, slab)
    slab = jnp.where(lane == 4, w1, slab)
    slab = jnp.where(lane == 5, w2, slab)
    col_ref[...] = slab
    row_ref[...] = slab.T[0:8, :]


def _route(logits, tm):
    T = logits.shape[0]
    tri = jnp.tril(jnp.ones((tm, tm), F32), -1).astype(BF16)
    return pl.pallas_call(
        _route_kernel,
        out_shape=(jax.ShapeDtypeStruct((T, LANES), F32),
                   jax.ShapeDtypeStruct((8, T), F32),
                   jax.ShapeDtypeStruct((8, LANES), F32)),
        grid=(T // tm,),
        in_specs=[pl.BlockSpec((tm, LANES), lambda i: (i, 0)),
                  pl.BlockSpec((tm, tm), lambda i: (0, 0))],
        out_specs=(pl.BlockSpec((tm, LANES), lambda i: (i, 0)),
                   pl.BlockSpec((8, tm), lambda i: (0, i)),
                   pl.BlockSpec((8, LANES), lambda i: (0, 0))),
        scratch_shapes=[pltpu.VMEM((8, LANES), F32)],
        compiler_params=_cparams(("arbitrary",)),
        name="route",
    )(logits, tri)


def _row_copy(src_ref, src_row, dst_ref, dst_row, sem):
    return pltpu.make_async_copy(src_ref.at[pl.ds(src_row, 1)], dst_ref.at[pl.ds(dst_row, 1)], sem)


def _dispatch_kernel(pstart_ref, pend_ref, nu_ref, idx_ref, h2_ref, buf_ref, zero_ref, sem):
    tm = h2_ref.shape[0]
    blk = zero_ref.shape[0]
    nb = buf_ref.shape[0] // blk

    @pl.when(pl.program_id(0) == 0)
    def _():
        zero_ref[...] = jnp.zeros_like(zero_ref)

        def zero_block(start):
            cp = pltpu.make_async_copy(zero_ref, buf_ref.at[pl.ds(start, blk)], sem)
            cp.start()
            cp.wait()

        def zbody(e, carry):
            @pl.when(pend_ref[e] > pstart_ref[e])
            def _():
                zero_block(pl.multiple_of(pend_ref[e] - blk, blk))
            return carry

        lax.fori_loop(0, N_EXPERTS, zbody, 0)

        def tail(j, carry):
            zero_block(pl.multiple_of(j * blk, blk))
            return carry

        lax.fori_loop(nu_ref[0], nb, tail, 0)

    def issue(t, carry):
        for k in range(2):
            dest = pstart_ref[idx_ref[k, t]] + idx_ref[2 + k, t]
            _row_copy(h2_ref, t, buf_ref, dest, sem).start()
        return carry

    lax.fori_loop(0, tm, issue, 0)

    def drain(t, carry):
        for k in range(2):
            _row_copy(h2_ref, 0, buf_ref, 0, sem).wait()
        return carry

    lax.fori_loop(0, tm, drain, 0)


def _dispatch(pstart, pend, n_used, idx, h2, n_rows, tm):
    T, D = h2.shape
    grid_spec = pltpu.PrefetchScalarGridSpec(
        num_scalar_prefetch=3,
        grid=(T // tm,),
        in_specs=[pl.BlockSpec((8, tm), lambda i, *_: (0, i), memory_space=pltpu.SMEM),
                  pl.BlockSpec((tm, D), lambda i, *_: (i, 0))],
        out_specs=pl.BlockSpec(memory_space=pl.ANY),
        scratch_shapes=[pltpu.VMEM((EXPERT_ROWS, D), h2.dtype),
                        pltpu.SemaphoreType.DMA(())],
    )
    return pl.pallas_call(
        _dispatch_kernel,
        out_shape=jax.ShapeDtypeStruct((n_rows, D), h2.dtype),
        grid_spec=grid_spec,
        compiler_params=_cparams(("arbitrary",)),
        name="dispatch",
    )(pstart, pend, n_used, idx, h2)


def _expert_kernel(be_ref, nu_ref, x_ref, w1_ref, w3_ref, w2_ref, o_ref):
    @pl.when(pl.program_id(0) < nu_ref[0])
    def _():
        x = x_ref[...].astype(BF16)
        a = jnp.dot(x, w1_ref[...], preferred_element_type=F32)
        b = jnp.dot(x, w3_ref[...], preferred_element_type=F32)
        hid = (a * jax.nn.sigmoid(a)) * b
        o_ref[...] = jnp.dot(hid.astype(BF16), w2_ref[...], preferred_element_type=F32)

    @pl.when(pl.program_id(0) >= nu_ref[0])
    def _():
        o_ref[...] = jnp.zeros_like(o_ref)


def _experts(block_e, n_used, buf, w1_bf, w3_bf, w2_bf):
    P, D = buf.shape
    F = w1_bf.shape[2]
    nb = P // EXPERT_ROWS

    def row_map(j, be, nu):
        return (jnp.minimum(j, nu[0] - 1), 0)

    grid_spec = pltpu.PrefetchScalarGridSpec(
        num_scalar_prefetch=2,
        grid=(nb,),
        in_specs=[pl.BlockSpec((EXPERT_ROWS, D), row_map),
                  pl.BlockSpec((None, D, F), lambda j, be, nu: (be[j], 0, 0)),
                  pl.BlockSpec((None, D, F), lambda j, be, nu: (be[j], 0, 0)),
                  pl.BlockSpec((None, F, D), lambda j, be, nu: (be[j], 0, 0))],
        out_specs=pl.BlockSpec((EXPERT_ROWS, D), lambda j, be, nu: (j, 0)),
    )
    return pl.pallas_call(
        _expert_kernel,
        out_shape=jax.ShapeDtypeStruct((P, D), F32),
        grid_spec=grid_spec,
        compiler_params=_cparams(("arbitrary",)),
        name="expert",
    )(block_e, n_used, buf, w1_bf, w3_bf, w2_bf)


def _final_kernel(pstart_ref, idx_ref, col_ref, x1_ref, mod_ref, fg_ref, yb_ref, o_ref,
                  g_ref, sem):
    tm = x1_ref.shape[0]

    def issue(t, carry):
        for k in range(2):
            dest = pstart_ref[idx_ref[k, t]] + idx_ref[2 + k, t]
            _row_copy(yb_ref, dest, g_ref.at[k], t, sem).start()
        return carry

    lax.fori_loop(0, tm, issue, 0)

    def drain(t, carry):
        for k in range(2):
            _row_copy(yb_ref, 0, g_ref.at[k], 0, sem).wait()
        return carry

    lax.fori_loop(0, tm, drain, 0)

    w1 = col_ref[:, 4:5]
    w2 = col_ref[:, 5:6]
    y = g_ref[0] * w1 + g_ref[1] * w2
    x2 = x1_ref[...] + mod_ref[0, 5:6, :] * y
    ms = jnp.mean(x2 * x2, axis=-1, keepdims=True)
    o_ref[...] = (x2 * lax.rsqrt(ms + EPS)) * fg_ref[...]


def _final(pstart, idx, col, x1, mod3, fgain, yb, seq, tm):
    T, D = x1.shape
    tps = seq // tm
    grid_spec = pltpu.PrefetchScalarGridSpec(
        num_scalar_prefetch=1,
        grid=(T // tm,),
        in_specs=[pl.BlockSpec((8, tm), lambda i, *_: (0, i), memory_space=pltpu.SMEM),
                  pl.BlockSpec((tm, LANES), lambda i, *_: (i, 0)),
                  pl.BlockSpec((tm, D), lambda i, *_: (i, 0)),
                  pl.BlockSpec((1, N_ADA, D), lambda i, *_: (i // tps, 0, 0)),
                  pl.BlockSpec((1, D), lambda i, *_: (0, 0)),
                  pl.BlockSpec(memory_space=pl.ANY)],
        out_specs=pl.BlockSpec((tm, D), lambda i, *_: (i, 0)),
        scratch_shapes=[pltpu.VMEM((2, tm, D), F32),
                        pltpu.SemaphoreType.DMA(())],
    )
    return pl.pallas_call(
        _final_kernel,
        out_shape=jax.ShapeDtypeStruct((T, D), F32),
        grid_spec=grid_spec,
        compiler_params=_cparams(("arbitrary",)),
        name="final",
    )(pstart, idx, col, x1, mod3, fgain, yb)


def _tile(n, pref):
    t = min(n, pref)
    assert n % t == 0, (n, t)
    return t


def kernel(x, c, positions, w_ada, b_ada, norm1_gain, w_in, w_pool, pool_scale, w_branch_pool,
           w_branch_ret, w_out, norm2_gain, w_group, b_group, w_router, b_router, w1, w3, w2,
           final_gain):
    B, S, D = x.shape
    T = B * S
    assert w_ada.shape[0] == 1, "only DEPTH == 1 is supported"
    x2d = x.reshape(T, D)
    posf = positions.astype(F32).reshape(T, 1)
    for l in range(1):
        mod3 = _ada(c, w_ada[l], b_ada[l][None, :]).reshape(B, N_ADA, D)

        proj = _inproj(x2d, mod3, norm1_gain[l][None, :], w_in[l].astype(BF16), S,
                       _tile(S, 1024), 1024)
        r = _retention(posf, proj, B, S, 1)

        w_rt = jnp.concatenate(
            [w_group[l], w_router[l],
             jnp.zeros((D, LANES - N_GROUPS - N_EXPERTS), F32)], axis=1)
        wr_hi = w_rt.astype(BF16)
        wr_lo = (w_rt - wr_hi.astype(F32)).astype(BF16)
        brt = jnp.concatenate(
            [b_group[l], b_router[l], jnp.zeros((LANES - N_GROUPS - N_EXPERTS,), F32)])[None, :]
        x1, h2, logits = _mix(
            proj, r, x2d, mod3, w_pool[l].astype(BF16), pool_scale[l][None, :],
            w_branch_pool[l].astype(BF16), w_branch_ret[l].astype(BF16), w_out[l].astype(BF16),
            norm2_gain[l][None, :], wr_hi, wr_lo, brt, S, _tile(S, 256))

        col, row, cnt = _route(logits, _tile(T, 512))
        idx = row.astype(jnp.int32)
        counts = cnt[0, ROUTE_LANE0:ROUTE_LANE0 + N_EXPERTS].astype(jnp.int32)
        padded = (counts + EXPERT_ROWS - 1) // EXPERT_ROWS * EXPERT_ROWS
        pend = jnp.cumsum(padded)
        pstart = pend - padded
        n_rows = 2 * T + N_EXPERTS * EXPERT_ROWS
        nb = n_rows // EXPERT_ROWS
        n_used = (pend[-1:] // EXPERT_ROWS).astype(jnp.int32)
        block_e = jnp.clip(
            jnp.searchsorted(pend, jnp.arange(nb, dtype=jnp.int32) * EXPERT_ROWS, side='right'),
            0, N_EXPERTS - 1).astype(jnp.int32)

        buf = _dispatch(pstart, pend, n_used, idx, h2, n_rows, _tile(T, 512))
        yb = _experts(block_e, n_used, buf, w1[l].astype(BF16), w3[l].astype(BF16),
                      w2[l].astype(BF16))
        x2d = _final(pstart, idx, col, x1, mod3, final_gain[None, :], yb, S, _tile(S, 256))
    return x2d.reshape(B, S, D)
```

```python
import functools

import jax
import jax.numpy as jnp
import numpy as np
from jax import lax
from jax.experimental import pallas as pl
from jax.experimental.pallas import tpu as pltpu

F32 = jnp.float32
BF16 = jnp.bfloat16

EPS = 1e-6
ROPE_BASE = 10000.0
POOL_WINDOWS = (2, 4, 8, 16)
POOL_HALO = 16
RET_HEADS = 8
RET_CHUNK = 128
N_GROUPS = 4
EXPERTS_PER_GROUP = 8
N_EXPERTS = N_GROUPS * EXPERTS_PER_GROUP
N_ADA = 6
LANES = 128
ROUTE_LANE0 = N_GROUPS
EXPERT_ROWS = 256
ROW_DMA_UNROLL = 8
VMEM_LIMIT = 56 * 1024 * 1024


def _cparams(sem):
    return pltpu.CompilerParams(dimension_semantics=sem, vmem_limit_bytes=VMEM_LIMIT)


def _const_spec(shape):
    n = len(shape)
    return pl.BlockSpec(shape, lambda *_: (0,) * n, pipeline_mode=pl.Buffered(1))


def _ada_kernel(cb_ref, w_ref, b_ref, o_ref, cact_ref):
    cb = cb_ref[...]
    cact_ref[...] = cb * jax.nn.sigmoid(cb)
    tn = w_ref.shape[1]
    for b in range(cb_ref.shape[0]):
        for j in range(tn // LANES):
            sl = slice(j * LANES, (j + 1) * LANES)
            prod = w_ref[:, sl] * cact_ref[b]
            o_ref[b:b + 1, sl] = jnp.sum(prod, axis=0, keepdims=True) + b_ref[:, sl]


def _ada(c, w_ada, b_ada):
    B, D = c.shape
    N = w_ada.shape[1]
    tn = 1024
    cb = jnp.broadcast_to(c[:, :, None], (B, D, LANES))
    return pl.pallas_call(
        _ada_kernel,
        out_shape=jax.ShapeDtypeStruct((B, N), F32),
        grid=(N // tn,),
        in_specs=[
            pl.BlockSpec((B, D, LANES), lambda j: (0, 0, 0)),
            pl.BlockSpec((D, tn), lambda j: (0, j)),
            pl.BlockSpec((1, tn), lambda j: (0, j)),
        ],
        out_specs=pl.BlockSpec((B, tn), lambda j: (0, j)),
        scratch_shapes=[pltpu.VMEM((B, D, LANES), F32)],
        compiler_params=_cparams(("arbitrary",)),
        name="ada",
    )(cb, w_ada, b_ada)


def _norm_mod_rows(x, gain, shift, scale):
    ms = jnp.mean(x * x, axis=-1, keepdims=True)
    y = (x * lax.rsqrt(ms + EPS)) * gain
    return y * (1.0 + scale) + shift


def _inproj_kernel(x_ref, mod_ref, g_ref, w_ref, *rest, side_blocks):
    n_side = len(side_blocks)
    side_in = rest[:n_side]
    o_ref = rest[n_side]
    side_out = rest[n_side + 1:2 * n_side + 1]
    h_ref = rest[2 * n_side + 1]
    tm = x_ref.shape[0]
    rc = 64

    @pl.when(pl.program_id(1) == 0)
    def _():
        shift = mod_ref[0, 0:1, :]
        scale = mod_ref[0, 1:2, :]
        gain = g_ref[...]

        def body(c, carry):
            rows = pl.ds(pl.multiple_of(c * rc, rc), rc)
            h_ref[rows, :] = _norm_mod_rows(x_ref[rows, :], gain, shift, scale).astype(BF16)
            return carry

        lax.fori_loop(0, tm // rc, body, 0)

    o_ref[...] = jnp.dot(h_ref[...], w_ref[...], preferred_element_type=F32).astype(o_ref.dtype)

    step = pl.program_id(0) * pl.num_programs(1) + pl.program_id(1)
    for src_ref, dst_ref, n_blocks in zip(side_in, side_out, side_blocks):
        @pl.when(step < n_blocks)
        def _(src_ref=src_ref, dst_ref=dst_ref):
            dst_ref[...] = src_ref[...].astype(BF16)


SIDE_CAST_BLOCK_BYTES = 1024 * 1024


def _side_cast_rows(n_rows, n_cols, n_steps):
    rows = SIDE_CAST_BLOCK_BYTES // (4 * n_cols)
    if n_rows % rows == 0 and n_rows // rows <= n_steps:
        return rows
    return None


def _inproj(x2d, mod3, gain, w_in_bf, side, seq, tm, tn):
    T, D = x2d.shape
    N = w_in_bf.shape[1]
    tps = seq // tm
    nj = N // tn
    n_steps = (T // tm) * nj
    side_rows = [_side_cast_rows(a.shape[0], a.shape[1], n_steps) for a in side]
    if any(r is None for r in side_rows):
        out = _inproj(x2d, mod3, gain, w_in_bf, [], seq, tm, tn)
        return out[0], [a.astype(BF16) for a in side]
    side_blocks = tuple(a.shape[0] // r for a, r in zip(side, side_rows))

    def side_spec(a, r, nb):
        return pl.BlockSpec((r, a.shape[1]), lambda i, j, nb=nb: (jnp.minimum(i * nj + j, nb - 1), 0))

    side_specs = [side_spec(a, r, nb) for a, r, nb in zip(side, side_rows, side_blocks)]
    outs = pl.pallas_call(
        functools.partial(_inproj_kernel, side_blocks=side_blocks),
        out_shape=[jax.ShapeDtypeStruct((T, N), BF16)]
        + [jax.ShapeDtypeStruct(a.shape, BF16) for a in side],
        grid=(T // tm, nj),
        in_specs=[
            pl.BlockSpec((tm, D), lambda i, j: (i, 0)),
            pl.BlockSpec((1, N_ADA, D), lambda i, j: (i // tps, 0, 0)),
            pl.BlockSpec((1, D), lambda i, j: (0, 0)),
            pl.BlockSpec((D, tn), lambda i, j: (0, j)),
        ] + side_specs,
        out_specs=[pl.BlockSpec((tm, tn), lambda i, j: (i, j))] + side_specs,
        scratch_shapes=[pltpu.VMEM((tm, D), BF16)],
        compiler_params=_cparams(("arbitrary", "arbitrary")),
        name="inproj",
    )(x2d, mod3, gain, w_in_bf, *side)
    return outs[0], list(outs[1:])


def _ret_kernel(pos_ref, q_ref, k_ref, v_ref, rg_ref, invf_ref, sign_ref, mask_ref,
                qd_ref, kd_ref, cd_ref, o_ref, state_ref, *, n_chunks, k_scale):
    C = RET_CHUNK

    @pl.when(pl.program_id(1) == 0)
    def _():
        state_ref[...] = jnp.zeros_like(state_ref)

    for c in range(n_chunks):
        rows = slice(c * C, (c + 1) * C)
        ang = pos_ref[rows, :] * invf_ref[...]
        cosv = jnp.cos(ang)
        sinv = jnp.sin(ang) * sign_ref[...]
        for h in range(RET_HEADS):
            cols = slice(h * C, (h + 1) * C)
            q = q_ref[rows, cols].astype(F32)
            k = k_ref[rows, cols].astype(F32)
            qr = q * cosv + pltpu.roll(q, C // 2, 1) * sinv
            kr = (k * cosv + pltpu.roll(k, C // 2, 1) * sinv) * k_scale
            v = v_ref[rows, cols]
            s = lax.dot_general(qr.astype(BF16), kr.astype(BF16), (((1,), (1,)), ((), ())),
                                preferred_element_type=F32) * mask_ref[h]
            st = state_ref[h]
            o = jnp.dot(s.astype(BF16), v, preferred_element_type=F32)
            o = o + jnp.dot((qr * qd_ref[h]).astype(BF16), st.astype(BF16),
                            preferred_element_type=F32)
            kv = lax.dot_general((kr * kd_ref[h]).astype(BF16), v, (((0,), (0,)), ((), ())),
                                 preferred_element_type=F32)
            state_ref[h] = st * cd_ref[h] + kv
            o = o * lax.rsqrt(jnp.mean(o * o, axis=-1, keepdims=True) + EPS)
            g = rg_ref[rows, cols].astype(F32)
            o_ref[rows, cols] = (o * (g * jax.nn.sigmoid(g))).astype(o_ref.dtype)


def _retention(posf, proj, batch, seq, n_chunks):
    T = proj.shape[0]
    H, C = RET_HEADS, RET_CHUNK
    W = H * C
    ct = n_chunks * C
    spb = seq // ct
    half = C // 2
    inv_freq = ROPE_BASE ** (-jnp.arange(half, dtype=F32) / half)
    invf2 = jnp.concatenate([inv_freq, inv_freq])[None, :]
    sign = jnp.concatenate([-jnp.ones((half,), F32), jnp.ones((half,), F32)])[None, :]
    log_gamma = jnp.log1p(-jnp.exp2(-5.0 - jnp.arange(H, dtype=F32)))
    idx = jnp.arange(C, dtype=F32)
    diff = idx[:, None] - idx[None, :]
    mask = jnp.where(diff >= 0, jnp.exp(log_gamma[:, None, None] * jnp.maximum(diff, 0.0)), 0.0)
    q_decay = jnp.exp(log_gamma[:, None] * (idx + 1.0))
    k_decay = jnp.exp(log_gamma[:, None] * (C - 1.0 - idx))
    chunk_decay = jnp.exp(log_gamma * C)
    qd = jnp.broadcast_to(q_decay[:, :, None], (H, C, C))
    kd = jnp.broadcast_to(k_decay[:, :, None], (H, C, C))
    cd = jnp.broadcast_to(chunk_decay[:, None, None], (H, 1, C))

    def tok(col):
        return pl.BlockSpec((ct, W), lambda b, n, col=col: (b * spb + n, col))

    return pl.pallas_call(
        functools.partial(_ret_kernel, n_chunks=n_chunks, k_scale=float(C) ** -0.5),
        out_shape=jax.ShapeDtypeStruct((T, W), BF16),
        grid=(batch, spb),
        in_specs=[
            pl.BlockSpec((ct, 1), lambda b, n: (b * spb + n, 0)),
            tok(1), tok(2), tok(3), tok(4),
            pl.BlockSpec((1, C), lambda b, n: (0, 0)),
            pl.BlockSpec((1, C), lambda b, n: (0, 0)),
            pl.BlockSpec((H, C, C), lambda b, n: (0, 0, 0)),
            pl.BlockSpec((H, C, C), lambda b, n: (0, 0, 0)),
            pl.BlockSpec((H, C, C), lambda b, n: (0, 0, 0)),
            pl.BlockSpec((H, 1, C), lambda b, n: (0, 0, 0)),
        ],
        out_specs=pl.BlockSpec((ct, W), lambda b, n: (b * spb + n, 0)),
        scratch_shapes=[pltpu.VMEM((H, C, C), F32)],
        compiler_params=_cparams(("arbitrary", "arbitrary")),
        name="ret",
    )(posf, proj, proj, proj, proj, invf2, sign, mask, qd, kd, cd)


def _mix_kernel(a_ref, halo_ref, r_ref, ga0_ref, ga1_ref, gb0_ref, gb1_ref, x_ref, mod_ref,
                wpool_ref, pscale_ref, wbp_ref, wbr_ref, wout_ref, g2_ref,
                wrh_ref, wrl_ref, brt_ref,
                x1_ref, h2_ref, lg_ref, ext_ref, pm_ref, mg_ref, *, tiles_per_seq):
    tm = a_ref.shape[0]
    pw = a_ref.shape[1]
    gd = pw // len(POOL_WINDOWS)
    i = pl.program_id(0)
    it = i % tiles_per_seq
    halo = halo_ref[...].astype(F32)
    ext_ref[0:POOL_HALO, :] = jnp.where(it == 0, 0.0, halo)
    ext_ref[POOL_HALO:, :] = a_ref[...].astype(F32)
    tpos = it * tm + lax.broadcasted_iota(jnp.int32, (tm, 1), 0)
    for g, w in enumerate(POOL_WINDOWS):
        cols = slice(g * gd, (g + 1) * gd)
        cur = ext_ref[POOL_HALO:POOL_HALO + tm, cols]
        s = cur
        for j in range(1, w):
            s = s + ext_ref[POOL_HALO - j:POOL_HALO - j + tm, cols]
        cnt = jnp.minimum(tpos + 1, w).astype(F32)
        pooled = s / cnt - cur
        pm = jnp.dot(pooled.astype(BF16), wpool_ref[g], preferred_element_type=F32)
        pm_ref[:, cols] = (pm * pscale_ref[:, cols]).astype(BF16)

    y_pool = jnp.dot(pm_ref[...], wbp_ref[...], preferred_element_type=F32)
    y_ret = jnp.dot(r_ref[...], wbr_ref[...], preferred_element_type=F32)
    half = y_pool.shape[1] // 2
    for hh, (ga_ref, gb_ref) in enumerate(((ga0_ref, gb0_ref), (ga1_ref, gb1_ref))):
        cols = slice(hh * half, (hh + 1) * half)
        ga = jax.nn.sigmoid(ga_ref[...].astype(F32))
        gb = jax.nn.sigmoid(gb_ref[...].astype(F32))
        mg_ref[:, cols] = (ga * y_pool[:, cols] + gb * y_ret[:, cols]).astype(BF16)
    z = jnp.dot(mg_ref[...], wout_ref[...], preferred_element_type=F32)
    gate1 = mod_ref[0, 2:3, :]
    x1 = x_ref[...] + gate1 * z
    x1_ref[...] = x1
    h2 = _norm_mod_rows(x1, g2_ref[...], mod_ref[0, 3:4, :], mod_ref[0, 4:5, :])
    h2_ref[...] = h2
    h_hi = h2.astype(BF16)
    h_lo = (h2 - h_hi.astype(F32)).astype(BF16)
    lg = jnp.dot(h_hi, wrh_ref[...], preferred_element_type=F32)
    lg = lg + jnp.dot(h_lo, wrh_ref[...], preferred_element_type=F32)
    lg = lg + jnp.dot(h_hi, wrl_ref[...], preferred_element_type=F32)
    lg_ref[...] = lg + brt_ref[...]


def _mix(proj, r, x2d, mod3, wpool_bf, pscale, wbp_bf, wbr_bf, wout_bf, g2, wr_hi, wr_lo, brt,
         seq, tm):
    T, D = x2d.shape
    PW = wbp_bf.shape[0]
    tps = seq // tm
    hb = tm // POOL_HALO

    def tok(col):
        return pl.BlockSpec((tm, PW), lambda i, col=col: (i, col))

    kern = functools.partial(_mix_kernel, tiles_per_seq=tps)
    return pl.pallas_call(
        kern,
        out_shape=(jax.ShapeDtypeStruct((T, D), F32),
                   jax.ShapeDtypeStruct((T, D), F32),
                   jax.ShapeDtypeStruct((T, LANES), F32)),
        grid=(T // tm,),
        in_specs=[
            tok(0),
            pl.BlockSpec((POOL_HALO, PW), lambda i: (jnp.maximum(i * hb - 1, 0), 0)),
            pl.BlockSpec((tm, PW), lambda i: (i, 0)),
            tok(5), tok(6), tok(7), tok(8),
            pl.BlockSpec((tm, D), lambda i: (i, 0)),
            pl.BlockSpec((1, N_ADA, D), lambda i: (i // tps, 0, 0)),
            _const_spec(wpool_bf.shape),
            _const_spec(pscale.shape),
            _const_spec(wbp_bf.shape),
            _const_spec(wbr_bf.shape),
            _const_spec(wout_bf.shape),
            _const_spec(g2.shape),
            _const_spec(wr_hi.shape),
            _const_spec(wr_lo.shape),
            _const_spec(brt.shape),
        ],
        out_specs=(pl.BlockSpec((tm, D), lambda i: (i, 0)),
                   pl.BlockSpec((tm, D), lambda i: (i, 0)),
                   pl.BlockSpec((tm, LANES), lambda i: (i, 0))),
        scratch_shapes=[pltpu.VMEM((POOL_HALO + tm, PW), F32),
                        pltpu.VMEM((tm, PW), BF16),
                        pltpu.VMEM((tm, D), BF16)],
        compiler_params=_cparams(("arbitrary",)),
        name="mix",
    )(proj, proj, r, proj, proj, proj, proj, x2d, mod3, wpool_bf, pscale, wbp_bf, wbr_bf,
      wout_bf, g2, wr_hi, wr_lo, brt)


def _route_kernel(lg_ref, tri_ref, col_ref, row_ref, cnt_ref, carry_ref):
    tm = lg_ref.shape[0]

    @pl.when(pl.program_id(0) == 0)
    def _():
        carry_ref[...] = jnp.zeros_like(carry_ref)

    L = lg_ref[...]
    lane = lax.broadcasted_iota(jnp.int32, (tm, LANES), 1)
    neg = -jnp.inf
    is_g = lane < N_GROUPS
    gl = jnp.where(is_g, L, neg)
    gmax = jnp.max(gl, axis=1, keepdims=True)
    grp = jnp.min(jnp.where(gl == gmax, lane, LANES), axis=1, keepdims=True)
    gsum = jnp.sum(jnp.where(is_g, jnp.exp(gl - gmax), 0.0), axis=1, keepdims=True)
    p_grp = 1.0 / gsum
    lo = ROUTE_LANE0 + grp * EXPERTS_PER_GROUP
    el = jnp.where((lane >= lo) & (lane < lo + EXPERTS_PER_GROUP), L, neg)
    v1 = jnp.max(el, axis=1, keepdims=True)
    i1 = jnp.min(jnp.where(el == v1, lane, LANES), axis=1, keepdims=True)
    el2 = jnp.where(lane == i1, neg, el)
    v2 = jnp.max(el2, axis=1, keepdims=True)
    i2 = jnp.min(jnp.where(el2 == v2, lane, LANES), axis=1, keepdims=True)
    e = jnp.exp(v2 - v1)
    w1 = p_grp / (1.0 + e)
    w2 = p_grp * e / (1.0 + e)
    sel1 = lane == i1
    sel2 = lane == i2
    onehot = jnp.where(sel1 | sel2, 1.0, 0.0)
    cum = jnp.dot(tri_ref[...], onehot.astype(BF16), preferred_element_type=F32)
    cum = cum + carry_ref[0:1, :]
    r1 = jnp.sum(jnp.where(sel1, cum, 0.0), axis=1, keepdims=True)
    r2 = jnp.sum(jnp.where(sel2, cum, 0.0), axis=1, keepdims=True)
    carry_ref[...] = carry_ref[...] + jnp.sum(onehot, axis=0, keepdims=True)
    cnt_ref[...] = carry_ref[...]
    slab = jnp.where(lane == 0, (i1 - ROUTE_LANE0).astype(F32), 0.0)
    slab = jnp.where(lane == 1, (i2 - ROUTE_LANE0).astype(F32), slab)
    slab = jnp.where(lane == 2, r1, slab)
    slab = jnp.where(lane == 3, r2, slab)
    slab = jnp.where(lane == 4, w1, slab)
    slab = jnp.where(lane == 5, w2, slab)
    col_ref[...] = slab
    row_ref[...] = slab.T[0:8, :]


def _route(logits, tm):
    T = logits.shape[0]
    tri = jnp.tril(jnp.ones((tm, tm), F32), -1).astype(BF16)
    return pl.pallas_call(
        _route_kernel,
        out_shape=(jax.ShapeDtypeStruct((T, LANES), F32),
                   jax.ShapeDtypeStruct((8, T), F32),
                   jax.ShapeDtypeStruct((8, LANES), F32)),
        grid=(T // tm,),
        in_specs=[pl.BlockSpec((tm, LANES), lambda i: (i, 0)),
                  pl.BlockSpec((tm, tm), lambda i: (0, 0))],
        out_specs=(pl.BlockSpec((tm, LANES), lambda i: (i, 0)),
                   pl.BlockSpec((8, tm), lambda i: (0, i)),
                   pl.BlockSpec((8, LANES), lambda i: (0, 0))),
        scratch_shapes=[pltpu.VMEM((8, LANES), F32)],
        compiler_params=_cparams(("arbitrary",)),
        name="route",
    )(logits, tri)


def _row_copy(src_ref, src_row, dst_ref, dst_row, sem):
    return pltpu.make_async_copy(src_ref.at[pl.ds(src_row, 1)], dst_ref.at[pl.ds(dst_row, 1)], sem)


def _dispatch_kernel(pstart_ref, pend_ref, nu_ref, idx_ref, h2_ref, buf_ref, zero_ref, sem):
    tm = h2_ref.shape[0]
    blk = zero_ref.shape[0]
    nb = buf_ref.shape[0] // blk

    @pl.when(pl.program_id(0) == 0)
    def _():
        zero_ref[...] = jnp.zeros_like(zero_ref)

        def zero_block(start):
            cp = pltpu.make_async_copy(zero_ref, buf_ref.at[pl.ds(start, blk)], sem)
            cp.start()
            cp.wait()

        def zbody(e, carry):
            @pl.when(pend_ref[e] > pstart_ref[e])
            def _():
                zero_block(pl.multiple_of(pend_ref[e] - blk, blk))
            return carry

        lax.fori_loop(0, N_EXPERTS, zbody, 0)

        def tail(j, carry):
            zero_block(pl.multiple_of(j * blk, blk))
            return carry

        lax.fori_loop(nu_ref[0], nb, tail, 0)

    def issue(t, carry):
        for k in range(2):
            dest = pstart_ref[idx_ref[k, t]] + idx_ref[2 + k, t]
            _row_copy(h2_ref, t, buf_ref, dest, sem).start()
        return carry

    lax.fori_loop(0, tm, issue, 0, unroll=ROW_DMA_UNROLL)
    for k in range(2):
        pltpu.make_async_copy(h2_ref, buf_ref.at[pl.ds(0, tm)], sem).wait()


def _dispatch(pstart, pend, n_used, idx, h2, n_rows, tm):
    T, D = h2.shape
    grid_spec = pltpu.PrefetchScalarGridSpec(
        num_scalar_prefetch=3,
        grid=(T // tm,),
        in_specs=[pl.BlockSpec((8, tm), lambda i, *_: (0, i), memory_space=pltpu.SMEM),
                  pl.BlockSpec((tm, D), lambda i, *_: (i, 0))],
        out_specs=pl.BlockSpec(memory_space=pl.ANY),
        scratch_shapes=[pltpu.VMEM((EXPERT_ROWS, D), h2.dtype),
                        pltpu.SemaphoreType.DMA(())],
    )
    return pl.pallas_call(
        _dispatch_kernel,
        out_shape=jax.ShapeDtypeStruct((n_rows, D), h2.dtype),
        grid_spec=grid_spec,
        compiler_params=_cparams(("arbitrary",)),
        name="dispatch",
    )(pstart, pend, n_used, idx, h2)


def _expert_kernel(be_ref, nu_ref, x_ref, w1_ref, w3_ref, w2_ref, o_ref):
    @pl.when(pl.program_id(0) < nu_ref[0])
    def _():
        x = x_ref[...].astype(BF16)
        a = jnp.dot(x, w1_ref[...], preferred_element_type=F32)
        b = jnp.dot(x, w3_ref[...], preferred_element_type=F32)
        hid = (a * jax.nn.sigmoid(a)) * b
        o_ref[...] = jnp.dot(hid.astype(BF16), w2_ref[...], preferred_element_type=F32)

    @pl.when(pl.program_id(0) >= nu_ref[0])
    def _():
        o_ref[...] = jnp.zeros_like(o_ref)


def _experts(block_e, n_used, buf, w1_bf, w3_bf, w2_bf):
    P, D = buf.shape
    F = w1_bf.shape[2]
    nb = P // EXPERT_ROWS

    def row_map(j, be, nu):
        return (jnp.minimum(j, nu[0] - 1), 0)

    grid_spec = pltpu.PrefetchScalarGridSpec(
        num_scalar_prefetch=2,
        grid=(nb,),
        in_specs=[pl.BlockSpec((EXPERT_ROWS, D), row_map),
                  pl.BlockSpec((None, D, F), lambda j, be, nu: (be[j], 0, 0)),
                  pl.BlockSpec((None, D, F), lambda j, be, nu: (be[j], 0, 0)),
                  pl.BlockSpec((None, F, D), lambda j, be, nu: (be[j], 0, 0))],
        out_specs=pl.BlockSpec((EXPERT_ROWS, D), lambda j, be, nu: (j, 0)),
    )
    return pl.pallas_call(
        _expert_kernel,
        out_shape=jax.ShapeDtypeStruct((P, D), F32),
        grid_spec=grid_spec,
        compiler_params=_cparams(("arbitrary",)),
        name="expert",
    )(block_e, n_used, buf, w1_bf, w3_bf, w2_bf)


def _final_kernel(pstart_ref, idx_ref, col_ref, x1_ref, mod_ref, fg_ref, yb_ref, o_ref,
                  g_ref, sem):
    tm = x1_ref.shape[0]

    def issue(t, carry):
        for k in range(2):
            dest = pstart_ref[idx_ref[k, t]] + idx_ref[2 + k, t]
            _row_copy(yb_ref, dest, g_ref.at[k], t, sem).start()
        return carry

    lax.fori_loop(0, tm, issue, 0, unroll=ROW_DMA_UNROLL)
    for k in range(2):
        pltpu.make_async_copy(yb_ref.at[pl.ds(0, tm)], g_ref.at[k], sem).wait()

    w1 = col_ref[:, 4:5]
    w2 = col_ref[:, 5:6]
    y = g_ref[0] * w1 + g_ref[1] * w2
    x2 = x1_ref[...] + mod_ref[0, 5:6, :] * y
    ms = jnp.mean(x2 * x2, axis=-1, keepdims=True)
    o_ref[...] = (x2 * lax.rsqrt(ms + EPS)) * fg_ref[...]


def _final(pstart, idx, col, x1, mod3, fgain, yb, seq, tm):
    T, D = x1.shape
    tps = seq // tm
    grid_spec = pltpu.PrefetchScalarGridSpec(
        num_scalar_prefetch=1,
        grid=(T // tm,),
        in_specs=[pl.BlockSpec((8, tm), lambda i, *_: (0, i), memory_space=pltpu.SMEM),
                  pl.BlockSpec((tm, LANES), lambda i, *_: (i, 0)),
                  pl.BlockSpec((tm, D), lambda i, *_: (i, 0)),
                  pl.BlockSpec((1, N_ADA, D), lambda i, *_: (i // tps, 0, 0)),
                  pl.BlockSpec((1, D), lambda i, *_: (0, 0)),
                  pl.BlockSpec(memory_space=pl.ANY)],
        out_specs=pl.BlockSpec((tm, D), lambda i, *_: (i, 0)),
        scratch_shapes=[pltpu.VMEM((2, tm, D), F32),
                        pltpu.SemaphoreType.DMA(())],
    )
    return pl.pallas_call(
        _final_kernel,
        out_shape=jax.ShapeDtypeStruct((T, D), F32),
        grid_spec=grid_spec,
        compiler_params=_cparams(("arbitrary",)),
        name="final",
    )(pstart, idx, col, x1, mod3, fgain, yb)


def _tile(n, pref):
    t = min(n, pref)
    assert n % t == 0, (n, t)
    return t


def kernel(x, c, positions, w_ada, b_ada, norm1_gain, w_in, w_pool, pool_scale, w_branch_pool,
           w_branch_ret, w_out, norm2_gain, w_group, b_group, w_router, b_router, w1, w3, w2,
           final_gain):
    B, S, D = x.shape
    T = B * S
    assert w_ada.shape[0] == 1, "only DEPTH == 1 is supported"
    x2d = x.reshape(T, D)
    posf = positions.astype(F32).reshape(T, 1)
    for l in range(1):
        mod3 = _ada(c, w_ada[l], b_ada[l][None, :]).reshape(B, N_ADA, D)

        n_exp, _, d_exp = w1[l].shape
        side = [w1[l].reshape(n_exp * D, d_exp), w3[l].reshape(n_exp * D, d_exp),
                w2[l].reshape(n_exp * d_exp, D)]
        proj, (w1_bf, w3_bf, w2_bf) = _inproj(x2d, mod3, norm1_gain[l][None, :],
                                              w_in[l].astype(BF16), side, S, _tile(S, 1024), 1024)
        w1_bf = w1_bf.reshape(n_exp, D, d_exp)
        w3_bf = w3_bf.reshape(n_exp, D, d_exp)
        w2_bf = w2_bf.reshape(n_exp, d_exp, D)
        r = _retention(posf, proj, B, S, 1)

        w_rt = jnp.concatenate(
            [w_group[l], w_router[l],
             jnp.zeros((D, LANES - N_GROUPS - N_EXPERTS), F32)], axis=1)
        wr_hi = w_rt.astype(BF16)
        wr_lo = (w_rt - wr_hi.astype(F32)).astype(BF16)
        brt = jnp.concatenate(
            [b_group[l], b_router[l], jnp.zeros((LANES - N_GROUPS - N_EXPERTS,), F32)])[None, :]
        x1, h2, logits = _mix(
            proj, r, x2d, mod3, w_pool[l].astype(BF16), pool_scale[l][None, :],
            w_branch_pool[l].astype(BF16), w_branch_ret[l].astype(BF16), w_out[l].astype(BF16),
            norm2_gain[l][None, :], wr_hi, wr_lo, brt, S, _tile(S, 256))

        col, row, cnt = _route(logits, _tile(T, 512))
        idx = row.astype(jnp.int32)
        counts = cnt[0, ROUTE_LANE0:ROUTE_LANE0 + N_EXPERTS].astype(jnp.int32)
        padded = (counts + EXPERT_ROWS - 1) // EXPERT_ROWS * EXPERT_ROWS
        pend = jnp.cumsum(padded)
        pstart = pend - padded
        n_rows = 2 * T + N_EXPERTS * EXPERT_ROWS
        nb = n_rows // EXPERT_ROWS
        n_used = (pend[-1:] // EXPERT_ROWS).astype(jnp.int32)
        block_row0 = jnp.arange(nb, dtype=jnp.int32) * EXPERT_ROWS
        block_e = jnp.minimum(
            jnp.sum((pend[None, :] <= block_row0[:, None]).astype(jnp.int32), axis=1),
            N_EXPERTS - 1)

        buf = _dispatch(pstart, pend, n_used, idx, h2, n_rows, _tile(T, 512))
        yb = _experts(block_e, n_used, buf, w1_bf, w3_bf, w2_bf)
        x2d = _final(pstart, idx, col, x1, mod3, final_gain[None, :], yb, S, _tile(S, 256))
    return x2d.reshape(B, S, D)
```

```python
import functools

import jax
import jax.numpy as jnp
import numpy as np
from jax import lax
from jax.experimental import pallas as pl
from jax.experimental.pallas import tpu as pltpu

F32 = jnp.float32
BF16 = jnp.bfloat16

EPS = 1e-6
ROPE_BASE = 10000.0
POOL_WINDOWS = (2, 4, 8, 16)
POOL_HALO = 16
RET_HEADS = 8
RET_CHUNK = 128
N_GROUPS = 4
EXPERTS_PER_GROUP = 8
N_EXPERTS = N_GROUPS * EXPERTS_PER_GROUP
N_ADA = 6
LANES = 128
ROUTE_LANE0 = N_GROUPS
EXPERT_ROWS = 256
SUBLANES = 8
ROW_DMA_UNROLL = 2
VMEM_LIMIT = 56 * 1024 * 1024


def _cparams(sem):
    return pltpu.CompilerParams(dimension_semantics=sem, vmem_limit_bytes=VMEM_LIMIT)


def _const_spec(shape):
    n = len(shape)
    return pl.BlockSpec(shape, lambda *_: (0,) * n, pipeline_mode=pl.Buffered(1))


def _ada_kernel(cb_ref, w_ref, b_ref, o_ref, cact_ref):
    cb = cb_ref[...]
    cact_ref[...] = cb * jax.nn.sigmoid(cb)
    tn = w_ref.shape[1]
    for b in range(cb_ref.shape[0]):
        for j in range(tn // LANES):
            sl = slice(j * LANES, (j + 1) * LANES)
            prod = w_ref[:, sl] * cact_ref[b]
            o_ref[b:b + 1, sl] = jnp.sum(prod, axis=0, keepdims=True) + b_ref[:, sl]


def _ada(c, w_ada, b_ada):
    B, D = c.shape
    N = w_ada.shape[1]
    tn = 1024
    cb = jnp.broadcast_to(c[:, :, None], (B, D, LANES))
    return pl.pallas_call(
        _ada_kernel,
        out_shape=jax.ShapeDtypeStruct((B, N), F32),
        grid=(N // tn,),
        in_specs=[
            pl.BlockSpec((B, D, LANES), lambda j: (0, 0, 0)),
            pl.BlockSpec((D, tn), lambda j: (0, j)),
            pl.BlockSpec((1, tn), lambda j: (0, j)),
        ],
        out_specs=pl.BlockSpec((B, tn), lambda j: (0, j)),
        scratch_shapes=[pltpu.VMEM((B, D, LANES), F32)],
        compiler_params=_cparams(("arbitrary",)),
        name="ada",
    )(cb, w_ada, b_ada)


def _norm_mod_rows(x, gain, shift, scale):
    ms = jnp.mean(x * x, axis=-1, keepdims=True)
    y = (x * lax.rsqrt(ms + EPS)) * gain
    return y * (1.0 + scale) + shift


def _inproj_kernel(x_ref, mod_ref, g_ref, w_ref, *rest, side_blocks):
    n_side = len(side_blocks)
    side_in = rest[:n_side]
    o_ref = rest[n_side]
    side_out = rest[n_side + 1:2 * n_side + 1]
    h_ref, r_ref = rest[2 * n_side + 1:]
    tm, d = x_ref.shape
    rc = 32

    @pl.when(pl.program_id(1) == 0)
    def _():
        shift = mod_ref[0, 0:1, :]
        mult = g_ref[...] * (1.0 + mod_ref[0, 1:2, :])

        def stats(c, carry):
            rows = pl.ds(pl.multiple_of(c * rc, rc), rc)
            x = x_ref[rows, :]
            r = lax.rsqrt(jnp.mean(x * x, axis=-1, keepdims=True) + EPS)
            r_ref[rows, :] = jnp.broadcast_to(r, (rc, LANES))
            return carry

        lax.fori_loop(0, tm // rc, stats, 0, unroll=4)

        def apply(c, carry):
            rows = pl.ds(pl.multiple_of(c * rc, rc), rc)
            r = r_ref[rows, :]
            for j in range(d // LANES):
                cols = slice(j * LANES, (j + 1) * LANES)
                h_ref[rows, cols] = ((x_ref[rows, cols] * r) * mult[:, cols]
                                     + shift[:, cols]).astype(BF16)
            return carry

        lax.fori_loop(0, tm // rc, apply, 0)

    o_ref[...] = jnp.dot(h_ref[...], w_ref[...], preferred_element_type=F32).astype(o_ref.dtype)

    step = pl.program_id(0) * pl.num_programs(1) + pl.program_id(1)
    for src_ref, dst_ref, n_blocks in zip(side_in, side_out, side_blocks):
        @pl.when(step < n_blocks)
        def _(src_ref=src_ref, dst_ref=dst_ref):
            dst_ref[...] = src_ref[...].astype(BF16)


SIDE_CAST_BLOCK_BYTES = 1024 * 1024


def _side_cast_rows(n_rows, n_cols, n_steps):
    rows = SIDE_CAST_BLOCK_BYTES // (4 * n_cols)
    if n_rows % rows == 0 and n_rows // rows <= n_steps:
        return rows
    return None


def _inproj(x2d, mod3, gain, w_in_bf, side, seq, tm, tn):
    T, D = x2d.shape
    N = w_in_bf.shape[1]
    tps = seq // tm
    nj = N // tn
    n_steps = (T // tm) * nj
    side_rows = [_side_cast_rows(a.shape[0], a.shape[1], n_steps) for a in side]
    if any(r is None for r in side_rows):
        out = _inproj(x2d, mod3, gain, w_in_bf, [], seq, tm, tn)
        return out[0], [a.astype(BF16) for a in side]
    side_blocks = tuple(a.shape[0] // r for a, r in zip(side, side_rows))

    def side_spec(a, r, nb):
        return pl.BlockSpec((r, a.shape[1]), lambda i, j, nb=nb: (jnp.minimum(i * nj + j, nb - 1), 0))

    side_specs = [side_spec(a, r, nb) for a, r, nb in zip(side, side_rows, side_blocks)]
    outs = pl.pallas_call(
        functools.partial(_inproj_kernel, side_blocks=side_blocks),
        out_shape=[jax.ShapeDtypeStruct((T, N), BF16)]
        + [jax.ShapeDtypeStruct(a.shape, BF16) for a in side],
        grid=(T // tm, nj),
        in_specs=[
            pl.BlockSpec((tm, D), lambda i, j: (i, 0)),
            pl.BlockSpec((1, N_ADA, D), lambda i, j: (i // tps, 0, 0)),
            pl.BlockSpec((1, D), lambda i, j: (0, 0)),
            pl.BlockSpec((D, tn), lambda i, j: (0, j)),
        ] + side_specs,
        out_specs=[pl.BlockSpec((tm, tn), lambda i, j: (i, j))] + side_specs,
        scratch_shapes=[pltpu.VMEM((tm, D), BF16), pltpu.VMEM((tm, LANES), F32)],
        compiler_params=_cparams(("arbitrary", "arbitrary")),
        name="inproj",
    )(x2d, mod3, gain, w_in_bf, *side)
    return outs[0], list(outs[1:])


def _ret_kernel(pos_ref, q_ref, k_ref, v_ref, rg_ref, invf_ref, sign_ref, mask_ref,
                qd_ref, kd_ref, cd_ref, o_ref, state_ref, *, n_chunks, k_scale):
    C = RET_CHUNK

    @pl.when(pl.program_id(1) == 0)
    def _():
        state_ref[...] = jnp.zeros_like(state_ref)

    hc = C // 2
    lo_lanes = lax.broadcasted_iota(jnp.int32, (hc, C), 1) < hc

    def spread(t):
        sw = pltpu.roll(t, hc, 1)
        return jnp.concatenate([jnp.where(lo_lanes, t, sw), jnp.where(lo_lanes, sw, t)], axis=0)

    for c in range(n_chunks):
        rows = slice(c * C, (c + 1) * C)
        pos2 = jnp.where(lo_lanes, pos_ref[c * C:c * C + hc, :], pos_ref[c * C + hc:(c + 1) * C, :])
        ang = pos2 * invf_ref[...]
        cosv = spread(jnp.cos(ang))
        sinv = spread(jnp.sin(ang)) * sign_ref[...]
        for h in range(RET_HEADS):
            cols = slice(h * C, (h + 1) * C)
            q = q_ref[rows, cols].astype(F32)
            k = k_ref[rows, cols].astype(F32)
            qr = q * cosv + pltpu.roll(q, C // 2, 1) * sinv
            kr = (k * cosv + pltpu.roll(k, C // 2, 1) * sinv) * k_scale
            v = v_ref[rows, cols]
            s = lax.dot_general(qr.astype(BF16), kr.astype(BF16), (((1,), (1,)), ((), ())),
                                preferred_element_type=F32) * mask_ref[h]
            st = state_ref[h]
            lhs = jnp.concatenate([s.astype(BF16), (qr * qd_ref[h]).astype(BF16)], axis=1)
            rhs = jnp.concatenate([v, st.astype(BF16)], axis=0)
            o = jnp.dot(lhs, rhs, preferred_element_type=F32)
            kv = lax.dot_general((kr * kd_ref[h]).astype(BF16), v, (((0,), (0,)), ((), ())),
                                 preferred_element_type=F32)
            state_ref[h] = st * cd_ref[h] + kv
            o = o * lax.rsqrt(jnp.mean(o * o, axis=-1, keepdims=True) + EPS)
            g = rg_ref[rows, cols].astype(F32)
            o_ref[rows, cols] = (o * (g * jax.nn.sigmoid(g))).astype(o_ref.dtype)


def _retention(posf, proj, batch, seq, n_chunks):
    T = proj.shape[0]
    H, C = RET_HEADS, RET_CHUNK
    W = H * C
    ct = n_chunks * C
    spb = seq // ct
    half = C // 2
    inv_freq = ROPE_BASE ** (-jnp.arange(half, dtype=F32) / half)
    invf2 = jnp.concatenate([inv_freq, inv_freq])[None, :]
    sign = jnp.concatenate([-jnp.ones((half,), F32), jnp.ones((half,), F32)])[None, :]
    log_gamma = jnp.log1p(-jnp.exp2(-5.0 - jnp.arange(H, dtype=F32)))
    idx = jnp.arange(C, dtype=F32)
    diff = idx[:, None] - idx[None, :]
    mask = jnp.where(diff >= 0, jnp.exp(log_gamma[:, None, None] * jnp.maximum(diff, 0.0)), 0.0)
    q_decay = jnp.exp(log_gamma[:, None] * (idx + 1.0))
    k_decay = jnp.exp(log_gamma[:, None] * (C - 1.0 - idx))
    chunk_decay = jnp.exp(log_gamma * C)
    qd = jnp.broadcast_to(q_decay[:, :, None], (H, C, C))
    kd = jnp.broadcast_to(k_decay[:, :, None], (H, C, C))
    cd = jnp.broadcast_to(chunk_decay[:, None, None], (H, 1, C))

    def tok(col):
        return pl.BlockSpec((ct, W), lambda b, n, col=col: (b * spb + n, col))

    return pl.pallas_call(
        functools.partial(_ret_kernel, n_chunks=n_chunks, k_scale=float(C) ** -0.5),
        out_shape=jax.ShapeDtypeStruct((T, W), BF16),
        grid=(batch, spb),
        in_specs=[
            pl.BlockSpec((ct, 1), lambda b, n: (b * spb + n, 0)),
            tok(1), tok(2), tok(3), tok(4),
            pl.BlockSpec((1, C), lambda b, n: (0, 0)),
            pl.BlockSpec((1, C), lambda b, n: (0, 0)),
            pl.BlockSpec((H, C, C), lambda b, n: (0, 0, 0)),
            pl.BlockSpec((H, C, C), lambda b, n: (0, 0, 0)),
            pl.BlockSpec((H, C, C), lambda b, n: (0, 0, 0)),
            pl.BlockSpec((H, 1, C), lambda b, n: (0, 0, 0)),
        ],
        out_specs=pl.BlockSpec((ct, W), lambda b, n: (b * spb + n, 0)),
        scratch_shapes=[pltpu.VMEM((H, C, C), F32)],
        compiler_params=_cparams(("arbitrary", "arbitrary")),
        name="ret",
    )(posf, proj, proj, proj, proj, invf2, sign, mask, qd, kd, cd)


def _mix_kernel(a_ref, halo_ref, r_ref, ga0_ref, ga1_ref, gb0_ref, gb1_ref, x_ref, mod_ref,
                wpool_ref, pscale_ref, wbp_ref, wbr_ref, wout_ref, g2_ref,
                wrc_ref, brt_ref,
                x1_ref, h2_ref, lg_ref, ext_ref, pm_ref, mg_ref, *, tiles_per_seq):
    tm = a_ref.shape[0]
    pw = a_ref.shape[1]
    gd = pw // len(POOL_WINDOWS)
    i = pl.program_id(0)
    it = i % tiles_per_seq
    halo = halo_ref[...].astype(F32)
    ext_ref[0:POOL_HALO, :] = jnp.where(it == 0, 0.0, halo)
    ext_ref[POOL_HALO:, :] = a_ref[...].astype(F32)
    tpos = it * tm + lax.broadcasted_iota(jnp.int32, (tm, 1), 0)
    for g, w in enumerate(POOL_WINDOWS):
        cols = slice(g * gd, (g + 1) * gd)
        cur = ext_ref[POOL_HALO:POOL_HALO + tm, cols]
        s = cur
        for j in range(1, w):
            s = s + ext_ref[POOL_HALO - j:POOL_HALO - j + tm, cols]
        cnt = jnp.minimum(tpos + 1, w).astype(F32)
        pooled = s / cnt - cur
        pm = jnp.dot(pooled.astype(BF16), wpool_ref[g], preferred_element_type=F32)
        pm_ref[:, cols] = (pm * pscale_ref[:, cols]).astype(BF16)

    y_pool = jnp.dot(pm_ref[...], wbp_ref[...], preferred_element_type=F32)
    y_ret = jnp.dot(r_ref[...], wbr_ref[...], preferred_element_type=F32)
    half = y_pool.shape[1] // 2
    for hh, (ga_ref, gb_ref) in enumerate(((ga0_ref, gb0_ref), (ga1_ref, gb1_ref))):
        cols = slice(hh * half, (hh + 1) * half)
        ga = jax.nn.sigmoid(ga_ref[...].astype(F32))
        gb = jax.nn.sigmoid(gb_ref[...].astype(F32))
        mg_ref[:, cols] = (ga * y_pool[:, cols] + gb * y_ret[:, cols]).astype(BF16)
    z = jnp.dot(mg_ref[...], wout_ref[...], preferred_element_type=F32)
    gate1 = mod_ref[0, 2:3, :]
    x1 = x_ref[...] + gate1 * z
    x1_ref[...] = x1
    h2 = _norm_mod_rows(x1, g2_ref[...], mod_ref[0, 3:4, :], mod_ref[0, 4:5, :])
    h2_ref[...] = h2
    h_hi = h2.astype(BF16)
    h_lo = (h2 - h_hi.astype(F32)).astype(BF16)
    parts = jnp.dot(jnp.concatenate([h_hi, h_lo], axis=0), wrc_ref[...],
                    preferred_element_type=F32)
    lg = (parts[:tm, :LANES] + parts[tm:, :LANES]) + (parts[:tm, LANES:] + parts[tm:, LANES:])
    lg_ref[...] = lg + brt_ref[...]


def _mix(proj, r, x2d, mod3, wpool_bf, pscale, wbp_bf, wbr_bf, wout_bf, g2, wr_cat, brt,
         seq, tm):
    T, D = x2d.shape
    PW = wbp_bf.shape[0]
    tps = seq // tm
    hb = tm // POOL_HALO

    def tok(col):
        return pl.BlockSpec((tm, PW), lambda i, col=col: (i, col))

    kern = functools.partial(_mix_kernel, tiles_per_seq=tps)
    return pl.pallas_call(
        kern,
        out_shape=(jax.ShapeDtypeStruct((T, D), F32),
                   jax.ShapeDtypeStruct((T, D), F32),
                   jax.ShapeDtypeStruct((T, LANES), F32)),
        grid=(T // tm,),
        in_specs=[
            tok(0),
            pl.BlockSpec((POOL_HALO, PW), lambda i: (jnp.maximum(i * hb - 1, 0), 0)),
            pl.BlockSpec((tm, PW), lambda i: (i, 0)),
            tok(5), tok(6), tok(7), tok(8),
            pl.BlockSpec((tm, D), lambda i: (i, 0)),
            pl.BlockSpec((1, N_ADA, D), lambda i: (i // tps, 0, 0)),
            _const_spec(wpool_bf.shape),
            _const_spec(pscale.shape),
            _const_spec(wbp_bf.shape),
            _const_spec(wbr_bf.shape),
            _const_spec(wout_bf.shape),
            _const_spec(g2.shape),
            _const_spec(wr_cat.shape),
            _const_spec(brt.shape),
        ],
        out_specs=(pl.BlockSpec((tm, D), lambda i: (i, 0)),
                   pl.BlockSpec((tm, D), lambda i: (i, 0)),
                   pl.BlockSpec((tm, LANES), lambda i: (i, 0))),
        scratch_shapes=[pltpu.VMEM((POOL_HALO + tm, PW), F32),
                        pltpu.VMEM((tm, PW), BF16),
                        pltpu.VMEM((tm, D), BF16)],
        compiler_params=_cparams(("arbitrary",)),
        name="mix",
    )(proj, proj, r, proj, proj, proj, proj, x2d, mod3, wpool_bf, pscale, wbp_bf, wbr_bf,
      wout_bf, g2, wr_cat, brt)


def _route_kernel(lg_ref, tri_ref, col_ref, row_ref, cnt_ref, carry_ref):
    tm = lg_ref.shape[0]

    @pl.when(pl.program_id(0) == 0)
    def _():
        carry_ref[...] = jnp.zeros_like(carry_ref)

    L = lg_ref[...]
    lane = lax.broadcasted_iota(jnp.int32, (tm, LANES), 1)
    neg = -jnp.inf
    is_g = lane < N_GROUPS
    gl = jnp.where(is_g, L, neg)
    gmax = jnp.max(gl, axis=1, keepdims=True)
    grp = jnp.min(jnp.where(gl == gmax, lane, LANES), axis=1, keepdims=True)
    gsum = jnp.sum(jnp.where(is_g, jnp.exp(gl - gmax), 0.0), axis=1, keepdims=True)
    p_grp = 1.0 / gsum
    lo = ROUTE_LANE0 + grp * EXPERTS_PER_GROUP
    el = jnp.where((lane >= lo) & (lane < lo + EXPERTS_PER_GROUP), L, neg)
    v1 = jnp.max(el, axis=1, keepdims=True)
    i1 = jnp.min(jnp.where(el == v1, lane, LANES), axis=1, keepdims=True)
    el2 = jnp.where(lane == i1, neg, el)
    v2 = jnp.max(el2, axis=1, keepdims=True)
    i2 = jnp.min(jnp.where(el2 == v2, lane, LANES), axis=1, keepdims=True)
    e = jnp.exp(v2 - v1)
    w1 = p_grp / (1.0 + e)
    w2 = p_grp * e / (1.0 + e)
    sel1 = lane == i1
    sel2 = lane == i2
    onehot = jnp.where(sel1 | sel2, 1.0, 0.0)
    cum = jnp.dot(tri_ref[...], onehot.astype(BF16), preferred_element_type=F32)
    cum = cum + carry_ref[0:1, :]
    r1 = jnp.sum(jnp.where(sel1, cum, 0.0), axis=1, keepdims=True)
    r2 = jnp.sum(jnp.where(sel2, cum, 0.0), axis=1, keepdims=True)
    carry_ref[...] = carry_ref[...] + jnp.sum(onehot, axis=0, keepdims=True)
    cnt_ref[...] = carry_ref[...]
    slab = jnp.where(lane == 0, (i1 - ROUTE_LANE0).astype(F32), 0.0)
    slab = jnp.where(lane == 1, (i2 - ROUTE_LANE0).astype(F32), slab)
    slab = jnp.where(lane == 2, r1, slab)
    slab = jnp.where(lane == 3, r2, slab)
    slab = jnp.where(lane == 4, w1, slab)
    slab = jnp.where(lane == 5, w2, slab)
    col_ref[...] = slab
    row_ref[...] = slab.T[0:8, :]


def _route(logits, tm):
    T = logits.shape[0]
    tri = jnp.tril(jnp.ones((tm, tm), F32), -1).astype(BF16)
    return pl.pallas_call(
        _route_kernel,
        out_shape=(jax.ShapeDtypeStruct((T, LANES), F32),
                   jax.ShapeDtypeStruct((8, T), F32),
                   jax.ShapeDtypeStruct((8, LANES), F32)),
        grid=(T // tm,),
        in_specs=[pl.BlockSpec((tm, LANES), lambda i: (i, 0)),
                  pl.BlockSpec((tm, tm), lambda i: (0, 0))],
        out_specs=(pl.BlockSpec((tm, LANES), lambda i: (i, 0)),
                   pl.BlockSpec((8, tm), lambda i: (0, i)),
                   pl.BlockSpec((8, LANES), lambda i: (0, 0))),
        scratch_shapes=[pltpu.VMEM((8, LANES), F32)],
        compiler_params=_cparams(("arbitrary",)),
        name="route",
    )(logits, tri)


def _dest_kernel(pstart_ref, row_ref, o_ref):
    e = row_ref[0:2, :].astype(jnp.int32)
    r = row_ref[2:4, :].astype(jnp.int32)
    base = jnp.zeros_like(e)
    for k in range(N_EXPERTS):
        base = jnp.where(e == k, pstart_ref[k], base)
    o_ref[...] = jnp.zeros_like(o_ref)
    o_ref[0:2, :] = base + r


def _dest(pstart, row, tn):
    T = row.shape[1]
    grid_spec = pltpu.PrefetchScalarGridSpec(
        num_scalar_prefetch=1,
        grid=(T // tn,),
        in_specs=[pl.BlockSpec((SUBLANES, tn), lambda i, *_: (0, i))],
        out_specs=pl.BlockSpec((SUBLANES, tn), lambda i, *_: (0, i)),
    )
    return pl.pallas_call(
        _dest_kernel,
        out_shape=jax.ShapeDtypeStruct((SUBLANES, T), jnp.int32),
        grid_spec=grid_spec,
        compiler_params=_cparams(("arbitrary",)),
        name="dest",
    )(pstart, row)


def _tile_major(dest2, tm):
    T = dest2.shape[1]
    return dest2.reshape(2, T // tm, tm).transpose(1, 0, 2).reshape(-1)


def _row_ref(ref, row):
    return ref.at[lax.shift_right_logical(row, 3), pl.ds(jnp.bitwise_and(row, SUBLANES - 1), 1)]


def _dispatch_kernel(pstart_ref, pend_ref, nu_ref, dest_ref, h2_ref, buf_ref, zero_ref, sem):
    ng = h2_ref.shape[0]
    tm = ng * SUBLANES
    blk = zero_ref.shape[0]
    nb = buf_ref.shape[0] // blk

    @pl.when(pl.program_id(0) == 0)
    def _():
        zero_ref[...] = jnp.zeros_like(zero_ref)

        def zero_block(start):
            cp = pltpu.make_async_copy(zero_ref, buf_ref.at[pl.ds(start, blk)], sem)
            cp.start()
            cp.wait()

        def zbody(e, carry):
            @pl.when(pend_ref[e] > pstart_ref[e])
            def _():
                zero_block(lax.shift_right_logical(pend_ref[e], 3) - blk)
            return carry

        lax.fori_loop(0, N_EXPERTS, zbody, 0)

        def tail(j, carry):
            zero_block(j * blk)
            return carry

        lax.fori_loop(nu_ref[0], nb, tail, 0)

    def issue(g, carry):
        for u in range(SUBLANES):
            for k in range(2):
                dest = dest_ref[k * tm + g * SUBLANES + u]
                pltpu.make_async_copy(h2_ref.at[g, pl.ds(u, 1)], _row_ref(buf_ref, dest),
                                      sem).start(priority=k)
        return carry

    lax.fori_loop(0, ng, issue, 0, unroll=ROW_DMA_UNROLL)
    for k in range(2):
        pltpu.make_async_copy(h2_ref, buf_ref.at[pl.ds(0, ng)], sem).wait()


def _dispatch(pstart, pend, n_used, dest_flat, h2, n_rows, tm):
    G, _, D = h2.shape
    ng = tm // SUBLANES
    grid_spec = pltpu.PrefetchScalarGridSpec(
        num_scalar_prefetch=3,
        grid=(G // ng,),
        in_specs=[pl.BlockSpec((2 * tm,), lambda i, *_: (i,), memory_space=pltpu.SMEM),
                  pl.BlockSpec((ng, SUBLANES, D), lambda i, *_: (i, 0, 0))],
        out_specs=pl.BlockSpec(memory_space=pl.ANY),
        scratch_shapes=[pltpu.VMEM((EXPERT_ROWS // SUBLANES, SUBLANES, D), h2.dtype),
                        pltpu.SemaphoreType.DMA(())],
    )
    return pl.pallas_call(
        _dispatch_kernel,
        out_shape=jax.ShapeDtypeStruct((n_rows // SUBLANES, SUBLANES, D), h2.dtype),
        grid_spec=grid_spec,
        compiler_params=_cparams(("arbitrary",)),
        name="dispatch",
    )(pstart, pend, n_used, dest_flat, h2)


def _expert_kernel(be_ref, nu_ref, x_ref, w1_ref, w3_ref, w2_ref, o_ref):
    @pl.when(pl.program_id(0) < nu_ref[0])
    def _():
        x = x_ref[...].astype(BF16)
        a = jnp.dot(x, w1_ref[...], preferred_element_type=F32)
        b = jnp.dot(x, w3_ref[...], preferred_element_type=F32)
        hid = (a * jax.nn.sigmoid(a)) * b
        o_ref[...] = jnp.dot(hid.astype(BF16), w2_ref[...], preferred_element_type=F32)

    @pl.when(pl.program_id(0) >= nu_ref[0])
    def _():
        o_ref[...] = jnp.zeros_like(o_ref)


def _experts(block_e, n_used, buf, w1_bf, w3_bf, w2_bf):
    P, D = buf.shape
    F = w1_bf.shape[2]
    nb = P // EXPERT_ROWS

    def row_map(j, be, nu):
        return (jnp.minimum(j, nu[0] - 1), 0)

    grid_spec = pltpu.PrefetchScalarGridSpec(
        num_scalar_prefetch=2,
        grid=(nb,),
        in_specs=[pl.BlockSpec((EXPERT_ROWS, D), row_map),
                  pl.BlockSpec((None, D, F), lambda j, be, nu: (be[j], 0, 0)),
                  pl.BlockSpec((None, D, F), lambda j, be, nu: (be[j], 0, 0)),
                  pl.BlockSpec((None, F, D), lambda j, be, nu: (be[j], 0, 0))],
        out_specs=pl.BlockSpec((EXPERT_ROWS, D), lambda j, be, nu: (j, 0)),
    )
    return pl.pallas_call(
        _expert_kernel,
        out_shape=jax.ShapeDtypeStruct((P, D), F32),
        grid_spec=grid_spec,
        compiler_params=_cparams(("arbitrary",)),
        name="expert",
    )(block_e, n_used, buf, w1_bf, w3_bf, w2_bf)


def _final_kernel(dcur_ref, dnext_ref, col_ref, x1_ref, mod_ref, fg_ref, yb_ref, o_ref,
                  g_ref, sems):
    ng = x1_ref.shape[0]
    tm = ng * SUBLANES
    i = pl.program_id(0)
    buf = lax.rem(i, 2)

    def issue_tile(d_ref, b):
        def issue(g, carry):
            for u in range(SUBLANES):
                for k in range(2):
                    dest = d_ref[k * tm + g * SUBLANES + u]
                    pltpu.make_async_copy(_row_ref(yb_ref, dest), g_ref.at[b, k, g, pl.ds(u, 1)],
                                          sems.at[b]).start(priority=k)
            return carry

        lax.fori_loop(0, ng, issue, 0, unroll=ROW_DMA_UNROLL)

    @pl.when(i == 0)
    def _():
        issue_tile(dcur_ref, 0)

    @pl.when(i + 1 < pl.num_programs(0))
    def _():
        issue_tile(dnext_ref, 1 - buf)

    for k in range(2):
        pltpu.make_async_copy(yb_ref.at[pl.ds(0, ng)], g_ref.at[buf, k], sems.at[buf]).wait()

    w1 = col_ref[:, :, 4:5]
    w2 = col_ref[:, :, 5:6]
    y = g_ref[buf, 0] * w1 + g_ref[buf, 1] * w2
    x2 = x1_ref[...] + mod_ref[0, 5:6, :] * y
    ms = jnp.mean(x2 * x2, axis=-1, keepdims=True)
    o_ref[...] = (x2 * lax.rsqrt(ms + EPS)) * fg_ref[...]


def _final(dest_flat, col, x1, mod3, fgain, yb, seq, tm):
    G, _, D = x1.shape
    ng = tm // SUBLANES
    n_tiles = G // ng
    tps = seq // tm
    return pl.pallas_call(
        _final_kernel,
        out_shape=jax.ShapeDtypeStruct((G, SUBLANES, D), F32),
        grid=(n_tiles,),
        in_specs=[pl.BlockSpec((2 * tm,), lambda i: (i,), memory_space=pltpu.SMEM),
                  pl.BlockSpec((2 * tm,), lambda i: (jnp.minimum(i + 1, n_tiles - 1),),
                               memory_space=pltpu.SMEM),
                  pl.BlockSpec((ng, SUBLANES, LANES), lambda i: (i, 0, 0)),
                  pl.BlockSpec((ng, SUBLANES, D), lambda i: (i, 0, 0)),
                  pl.BlockSpec((1, N_ADA, D), lambda i: (i // tps, 0, 0)),
                  pl.BlockSpec((1, D), lambda i: (0, 0)),
                  pl.BlockSpec(memory_space=pl.ANY)],
        out_specs=pl.BlockSpec((ng, SUBLANES, D), lambda i: (i, 0, 0)),
        scratch_shapes=[pltpu.VMEM((2, 2, ng, SUBLANES, D), F32),
                        pltpu.SemaphoreType.DMA((2,))],
        compiler_params=_cparams(("arbitrary",)),
        name="final",
    )(dest_flat, dest_flat, col, x1, mod3, fgain, yb)


def _tile(n, pref):
    t = min(n, pref)
    assert n % t == 0, (n, t)
    return t


def kernel(x, c, positions, w_ada, b_ada, norm1_gain, w_in, w_pool, pool_scale, w_branch_pool,
           w_branch_ret, w_out, norm2_gain, w_group, b_group, w_router, b_router, w1, w3, w2,
           final_gain):
    B, S, D = x.shape
    T = B * S
    assert w_ada.shape[0] == 1, "only DEPTH == 1 is supported"
    x2d = x.reshape(T, D)
    posf = positions.astype(F32).reshape(T, 1)
    for l in range(1):
        mod3 = _ada(c, w_ada[l], b_ada[l][None, :]).reshape(B, N_ADA, D)

        n_exp, _, d_exp = w1[l].shape
        side = [w1[l].reshape(n_exp * D, d_exp), w3[l].reshape(n_exp * D, d_exp),
                w2[l].reshape(n_exp * d_exp, D)]
        proj, (w1_bf, w3_bf, w2_bf) = _inproj(x2d, mod3, norm1_gain[l][None, :],
                                              w_in[l].astype(BF16), side, S, _tile(S, 1024), 1024)
        w1_bf = w1_bf.reshape(n_exp, D, d_exp)
        w3_bf = w3_bf.reshape(n_exp, D, d_exp)
        w2_bf = w2_bf.reshape(n_exp, d_exp, D)
        r = _retention(posf, proj, B, S, 1)

        w_rt = jnp.concatenate(
            [w_group[l], w_router[l],
             jnp.zeros((D, LANES - N_GROUPS - N_EXPERTS), F32)], axis=1)
        wr_hi = w_rt.astype(BF16)
        wr_lo = (w_rt - wr_hi.astype(F32)).astype(BF16)
        wr_cat = jnp.concatenate([wr_hi, wr_lo], axis=1)
        brt = jnp.concatenate(
            [b_group[l], b_router[l], jnp.zeros((LANES - N_GROUPS - N_EXPERTS,), F32)])[None, :]
        x1, h2, logits = _mix(
            proj, r, x2d, mod3, w_pool[l].astype(BF16), pool_scale[l][None, :],
            w_branch_pool[l].astype(BF16), w_branch_ret[l].astype(BF16), w_out[l].astype(BF16),
            norm2_gain[l][None, :], wr_cat, brt, S, _tile(S, 256))

        col, row, cnt = _route(logits, _tile(T, 512))
        counts = cnt[0, ROUTE_LANE0:ROUTE_LANE0 + N_EXPERTS].astype(jnp.int32)
        padded = (counts + EXPERT_ROWS - 1) // EXPERT_ROWS * EXPERT_ROWS
        pend = jnp.cumsum(padded)
        pstart = pend - padded
        n_rows = 2 * T + N_EXPERTS * EXPERT_ROWS
        nb = n_rows // EXPERT_ROWS
        n_used = (pend[-1:] // EXPERT_ROWS).astype(jnp.int32)
        block_row0 = jnp.arange(nb, dtype=jnp.int32) * EXPERT_ROWS
        block_e = jnp.minimum(
            jnp.sum((pend[None, :] <= block_row0[:, None]).astype(jnp.int32), axis=1),
            N_EXPERTS - 1)

        dest2 = _dest(pstart, row, _tile(T, 4096))[:2]
        tm_d = _tile(T, 1024)
        tm_f = _tile(S, 512)
        buf = _dispatch(pstart, pend, n_used, _tile_major(dest2, tm_d),
                        h2.reshape(T // SUBLANES, SUBLANES, D), n_rows, tm_d)
        yb = _experts(block_e, n_used, buf.reshape(n_rows, D), w1_bf, w3_bf, w2_bf)
        out = _final(_tile_major(dest2, tm_f), col.reshape(T // SUBLANES, SUBLANES, LANES),
                     x1.reshape(T // SUBLANES, SUBLANES, D), mod3, final_gain[None, :],
                     yb.reshape(n_rows // SUBLANES, SUBLANES, D), S, tm_f)
    return out.reshape(B, S, D)
```

```python
import functools

import jax
import jax.numpy as jnp
from jax import lax
from jax.experimental import pallas as pl
from jax.experimental.pallas import tpu as pltpu

F32 = jnp.float32
BF16 = jnp.bfloat16

EPS = 1e-6
ROPE_BASE = 10000.0
POOL_WINDOWS = (2, 4, 8, 16)
POOL_HALO = 16
RET_HEADS = 8
RET_CHUNK = 128
N_GROUPS = 4
EXPERTS_PER_GROUP = 8
N_EXPERTS = N_GROUPS * EXPERTS_PER_GROUP
N_ADA = 6
LANES = 128
ROUTE_LANE0 = N_GROUPS
EXPERT_ROWS = 256
SUBLANES = 8
ROW_DMA_UNROLL = 2
VMEM_LIMIT = 56 * 1024 * 1024


def _cparams(sem):
    return pltpu.CompilerParams(dimension_semantics=sem, vmem_limit_bytes=VMEM_LIMIT)


def _const_spec(shape):
    n = len(shape)
    return pl.BlockSpec(shape, lambda *_: (0,) * n, pipeline_mode=pl.Buffered(1))


def _ada_kernel(cb_ref, w_ref, b_ref, o_ref, cact_ref):
    cb = cb_ref[...]
    cact_ref[...] = cb * jax.nn.sigmoid(cb)
    tn = w_ref.shape[1]
    for b in range(cb_ref.shape[0]):
        for j in range(tn // LANES):
            sl = slice(j * LANES, (j + 1) * LANES)
            prod = w_ref[:, sl] * cact_ref[b]
            o_ref[b:b + 1, sl] = jnp.sum(prod, axis=0, keepdims=True) + b_ref[:, sl]


def _ada(c, w_ada, b_ada):
    B, D = c.shape
    N = w_ada.shape[1]
    tn = 1024
    cb = jnp.broadcast_to(c[:, :, None], (B, D, LANES))
    return pl.pallas_call(
        _ada_kernel,
        out_shape=jax.ShapeDtypeStruct((B, N), F32),
        grid=(N // tn,),
        in_specs=[
            pl.BlockSpec((B, D, LANES), lambda j: (0, 0, 0)),
            pl.BlockSpec((D, tn), lambda j: (0, j)),
            pl.BlockSpec((1, tn), lambda j: (0, j)),
        ],
        out_specs=pl.BlockSpec((B, tn), lambda j: (0, j)),
        scratch_shapes=[pltpu.VMEM((B, D, LANES), F32)],
        compiler_params=_cparams(("arbitrary",)),
        name="ada",
    )(cb, w_ada, b_ada)


def _norm_mod_rows(x, gain, shift, scale):
    ms = jnp.mean(x * x, axis=-1, keepdims=True)
    y = (x * lax.rsqrt(ms + EPS)) * gain
    return y * (1.0 + scale) + shift


def _inproj_kernel(x_ref, mod_ref, g_ref, w_ref, *rest, side_blocks):
    n_side = len(side_blocks)
    side_in = rest[:n_side]
    o_ref = rest[n_side]
    side_out = rest[n_side + 1:2 * n_side + 1]
    h_ref, r_ref = rest[2 * n_side + 1:]
    tm, d = x_ref.shape
    rc = 32

    @pl.when(pl.program_id(1) == 0)
    def _():
        shift = mod_ref[0, 0:1, :]
        mult = g_ref[...] * (1.0 + mod_ref[0, 1:2, :])

        def stats(c, carry):
            rows = pl.ds(pl.multiple_of(c * rc, rc), rc)
            x = x_ref[rows, :]
            r = lax.rsqrt(jnp.mean(x * x, axis=-1, keepdims=True) + EPS)
            r_ref[rows, :] = jnp.broadcast_to(r, (rc, LANES))
            return carry

        lax.fori_loop(0, tm // rc, stats, 0, unroll=4)

        def apply(c, carry):
            rows = pl.ds(pl.multiple_of(c * rc, rc), rc)
            r = r_ref[rows, :]
            for j in range(d // LANES):
                cols = slice(j * LANES, (j + 1) * LANES)
                h_ref[rows, cols] = ((x_ref[rows, cols] * r) * mult[:, cols]
                                     + shift[:, cols]).astype(BF16)
            return carry

        lax.fori_loop(0, tm // rc, apply, 0)

    o_ref[...] = jnp.dot(h_ref[...], w_ref[...], preferred_element_type=F32).astype(o_ref.dtype)

    step = pl.program_id(0) * pl.num_programs(1) + pl.program_id(1)
    for src_ref, dst_ref, n_blocks in zip(side_in, side_out, side_blocks):
        @pl.when(step < n_blocks)
        def _(src_ref=src_ref, dst_ref=dst_ref):
            dst_ref[...] = src_ref[...].astype(BF16)


SIDE_CAST_BLOCK_BYTES = 1024 * 1024


def _side_cast_rows(n_rows, n_cols, n_steps):
    rows = SIDE_CAST_BLOCK_BYTES // (4 * n_cols)
    if n_rows % rows == 0 and n_rows // rows <= n_steps:
        return rows
    return None


def _inproj(x2d, mod3, gain, w_in_bf, side, seq, tm, tn):
    T, D = x2d.shape
    N = w_in_bf.shape[1]
    tps = seq // tm
    nj = N // tn
    n_steps = (T // tm) * nj
    side_rows = [_side_cast_rows(a.shape[0], a.shape[1], n_steps) for a in side]
    if any(r is None for r in side_rows):
        out = _inproj(x2d, mod3, gain, w_in_bf, [], seq, tm, tn)
        return out[0], [a.astype(BF16) for a in side]
    side_blocks = tuple(a.shape[0] // r for a, r in zip(side, side_rows))

    def side_spec(a, r, nb):
        return pl.BlockSpec((r, a.shape[1]), lambda i, j, nb=nb: (jnp.minimum(i * nj + j, nb - 1), 0))

    side_specs = [side_spec(a, r, nb) for a, r, nb in zip(side, side_rows, side_blocks)]
    outs = pl.pallas_call(
        functools.partial(_inproj_kernel, side_blocks=side_blocks),
        out_shape=[jax.ShapeDtypeStruct((T, N), BF16)]
        + [jax.ShapeDtypeStruct(a.shape, BF16) for a in side],
        grid=(T // tm, nj),
        in_specs=[
            pl.BlockSpec((tm, D), lambda i, j: (i, 0)),
            pl.BlockSpec((1, N_ADA, D), lambda i, j: (i // tps, 0, 0)),
            pl.BlockSpec((1, D), lambda i, j: (0, 0)),
            pl.BlockSpec((D, tn), lambda i, j: (0, j)),
        ] + side_specs,
        out_specs=[pl.BlockSpec((tm, tn), lambda i, j: (i, j))] + side_specs,
        scratch_shapes=[pltpu.VMEM((tm, D), BF16), pltpu.VMEM((tm, LANES), F32)],
        compiler_params=_cparams(("arbitrary", "arbitrary")),
        name="inproj",
    )(x2d, mod3, gain, w_in_bf, *side)
    return outs[0], list(outs[1:])


def _ret_kernel(pos0_ref, q0_ref, k0_ref, posn_ref, qn_ref, kn_ref, v_ref, rg_ref,
                invf_ref, sign_ref, mask_ref, qd_ref, kd_ref, cd_ref, o_ref, state_ref,
                *staging, k_scale):
    C = RET_CHUNK
    hc = C // 2
    n = pl.program_id(1)
    lo_lanes = lax.broadcasted_iota(jnp.int32, (hc, C), 1) < hc
    sets = (staging[0:4], staging[4:8])

    def spread(t):
        sw = pltpu.roll(t, hc, 1)
        return jnp.concatenate([jnp.where(lo_lanes, t, sw), jnp.where(lo_lanes, sw, t)], axis=0)

    def stage(pos_ref, q_ref, k_ref, dst):
        qr_ref, qs_ref, kr_ref, ks_ref = dst
        pos2 = jnp.where(lo_lanes, pos_ref[0:hc, :], pos_ref[hc:C, :])
        ang = pos2 * invf_ref[...]
        cosv = spread(jnp.cos(ang))
        sinv = spread(jnp.sin(ang)) * sign_ref[...]
        for h in range(RET_HEADS):
            cols = slice(h * C, (h + 1) * C)
            q = q_ref[:, cols].astype(F32)
            k = k_ref[:, cols].astype(F32)
            qr = q * cosv + pltpu.roll(q, hc, 1) * sinv
            kr = (k * cosv + pltpu.roll(k, hc, 1) * sinv) * k_scale
            qr_ref[:, cols] = qr.astype(BF16)
            qs_ref[:, cols] = (qr * qd_ref[h]).astype(BF16)
            kr_ref[:, cols] = kr.astype(BF16)
            ks_ref[:, cols] = (kr * kd_ref[h]).astype(BF16)

    def recur(src):
        qr_ref, qs_ref, kr_ref, ks_ref = src
        for h in range(RET_HEADS):
            cols = slice(h * C, (h + 1) * C)
            v = v_ref[:, cols]
            s = lax.dot_general(qr_ref[:, cols], kr_ref[:, cols], (((1,), (1,)), ((), ())),
                                preferred_element_type=F32) * mask_ref[h]
            st = state_ref[h]
            lhs = jnp.concatenate([s.astype(BF16), qs_ref[:, cols]], axis=1)
            rhs = jnp.concatenate([v, st.astype(BF16)], axis=0)
            o = jnp.dot(lhs, rhs, preferred_element_type=F32)
            kv = lax.dot_general(ks_ref[:, cols], v, (((0,), (0,)), ((), ())),
                                 preferred_element_type=F32)
            state_ref[h] = st * cd_ref[h] + kv
            o = o * lax.rsqrt(jnp.mean(o * o, axis=-1, keepdims=True) + EPS)
            g = rg_ref[:, cols].astype(F32)
            o_ref[:, cols] = (o * (g * jax.nn.sigmoid(g))).astype(o_ref.dtype)

    @pl.when(n == 0)
    def _():
        state_ref[...] = jnp.zeros_like(state_ref)
        stage(pos0_ref, q0_ref, k0_ref, sets[0])

    for parity in range(2):
        @pl.when(lax.rem(n, 2) == parity)
        def _(parity=parity):
            stage(posn_ref, qn_ref, kn_ref, sets[1 - parity])
            recur(sets[parity])


def _retention(posf, proj, batch, seq):
    T = proj.shape[0]
    H, C = RET_HEADS, RET_CHUNK
    W = H * C
    ct = C
    spb = seq // ct
    half = C // 2
    inv_freq = ROPE_BASE ** (-jnp.arange(half, dtype=F32) / half)
    invf2 = jnp.concatenate([inv_freq, inv_freq])[None, :]
    sign = jnp.concatenate([-jnp.ones((half,), F32), jnp.ones((half,), F32)])[None, :]
    log_gamma = jnp.log1p(-jnp.exp2(-5.0 - jnp.arange(H, dtype=F32)))
    idx = jnp.arange(C, dtype=F32)
    diff = idx[:, None] - idx[None, :]
    mask = jnp.where(diff >= 0, jnp.exp(log_gamma[:, None, None] * jnp.maximum(diff, 0.0)), 0.0)
    q_decay = jnp.exp(log_gamma[:, None] * (idx + 1.0))
    k_decay = jnp.exp(log_gamma[:, None] * (C - 1.0 - idx))
    chunk_decay = jnp.exp(log_gamma * C)
    qd = jnp.broadcast_to(q_decay[:, :, None], (H, C, C))
    kd = jnp.broadcast_to(k_decay[:, :, None], (H, C, C))
    cd = jnp.broadcast_to(chunk_decay[:, None, None], (H, 1, C))

    def tok(col):
        return pl.BlockSpec((ct, W), lambda b, n, col=col: (b * spb + n, col))

    def first(col):
        return pl.BlockSpec((ct, W), lambda b, n, col=col: (b * spb, col))

    def nxt(col):
        return pl.BlockSpec((ct, W),
                            lambda b, n, col=col: (b * spb + jnp.minimum(n + 1, spb - 1), col))

    return pl.pallas_call(
        functools.partial(_ret_kernel, k_scale=float(C) ** -0.5),
        out_shape=jax.ShapeDtypeStruct((T, W), BF16),
        grid=(batch, spb),
        in_specs=[
            pl.BlockSpec((ct, 1), lambda b, n: (b * spb, 0)),
            first(1), first(2),
            pl.BlockSpec((ct, 1), lambda b, n: (b * spb + jnp.minimum(n + 1, spb - 1), 0)),
            nxt(1), nxt(2),
            tok(3), tok(4),
            pl.BlockSpec((1, C), lambda b, n: (0, 0)),
            pl.BlockSpec((1, C), lambda b, n: (0, 0)),
            pl.BlockSpec((H, C, C), lambda b, n: (0, 0, 0)),
            pl.BlockSpec((H, C, C), lambda b, n: (0, 0, 0)),
            pl.BlockSpec((H, C, C), lambda b, n: (0, 0, 0)),
            pl.BlockSpec((H, 1, C), lambda b, n: (0, 0, 0)),
        ],
        out_specs=pl.BlockSpec((ct, W), lambda b, n: (b * spb + n, 0)),
        scratch_shapes=[pltpu.VMEM((H, C, C), F32)] + [pltpu.VMEM((C, W), BF16)] * 8,
        compiler_params=_cparams(("arbitrary", "arbitrary")),
        name="ret",
    )(posf, proj, proj, posf, proj, proj, proj, proj, invf2, sign, mask, qd, kd, cd)


def _mix_kernel(a_ref, halo_ref, r_ref, ga0_ref, ga1_ref, gb0_ref, gb1_ref, x_ref, mod_ref,
                wpool_ref, pscale_ref, wbp_ref, wbr_ref, wout_ref, g2_ref,
                wrc_ref, brt_ref,
                x1_ref, h2_ref, lg_ref, ext_ref, pm_ref, mg_ref, *, tiles_per_seq):
    tm = a_ref.shape[0]
    pw = a_ref.shape[1]
    gd = pw // len(POOL_WINDOWS)
    i = pl.program_id(0)
    it = i % tiles_per_seq
    halo = halo_ref[...].astype(F32)
    ext_ref[0:POOL_HALO, :] = jnp.where(it == 0, 0.0, halo)
    ext_ref[POOL_HALO:, :] = a_ref[...].astype(F32)
    tpos = it * tm + lax.broadcasted_iota(jnp.int32, (tm, 1), 0)
    for g, w in enumerate(POOL_WINDOWS):
        cols = slice(g * gd, (g + 1) * gd)
        cur = ext_ref[POOL_HALO:POOL_HALO + tm, cols]
        s = cur
        for j in range(1, w):
            s = s + ext_ref[POOL_HALO - j:POOL_HALO - j + tm, cols]
        cnt = jnp.minimum(tpos + 1, w).astype(F32)
        pooled = s / cnt - cur
        pm = jnp.dot(pooled.astype(BF16), wpool_ref[g], preferred_element_type=F32)
        pm_ref[:, cols] = (pm * pscale_ref[:, cols]).astype(BF16)

    y_pool = jnp.dot(pm_ref[...], wbp_ref[...], preferred_element_type=F32)
    y_ret = jnp.dot(r_ref[...], wbr_ref[...], preferred_element_type=F32)
    half = y_pool.shape[1] // 2
    for hh, (ga_ref, gb_ref) in enumerate(((ga0_ref, gb0_ref), (ga1_ref, gb1_ref))):
        cols = slice(hh * half, (hh + 1) * half)
        ga = jax.nn.sigmoid(ga_ref[...].astype(F32))
        gb = jax.nn.sigmoid(gb_ref[...].astype(F32))
        mg_ref[:, cols] = (ga * y_pool[:, cols] + gb * y_ret[:, cols]).astype(BF16)
    z = jnp.dot(mg_ref[...], wout_ref[...], preferred_element_type=F32)
    gate1 = mod_ref[0, 2:3, :]
    x1 = x_ref[...] + gate1 * z
    x1_ref[...] = x1
    h2 = _norm_mod_rows(x1, g2_ref[...], mod_ref[0, 3:4, :], mod_ref[0, 4:5, :])
    h2_ref[...] = h2
    h_hi = h2.astype(BF16)
    h_lo = (h2 - h_hi.astype(F32)).astype(BF16)
    parts = jnp.dot(jnp.concatenate([h_hi, h_lo], axis=0), wrc_ref[...],
                    preferred_element_type=F32)
    lg = (parts[:tm, :LANES] + parts[tm:, :LANES]) + (parts[:tm, LANES:] + parts[tm:, LANES:])
    lg_ref[...] = lg + brt_ref[...]


def _mix(proj, r, x2d, mod3, wpool_bf, pscale, wbp_bf, wbr_bf, wout_bf, g2, wr_cat, brt,
         seq, tm):
    T, D = x2d.shape
    PW = wbp_bf.shape[0]
    tps = seq // tm
    hb = tm // POOL_HALO

    def tok(col):
        return pl.BlockSpec((tm, PW), lambda i, col=col: (i, col))

    kern = functools.partial(_mix_kernel, tiles_per_seq=tps)
    return pl.pallas_call(
        kern,
        out_shape=(jax.ShapeDtypeStruct((T, D), F32),
                   jax.ShapeDtypeStruct((T, D), F32),
                   jax.ShapeDtypeStruct((T, LANES), F32)),
        grid=(T // tm,),
        in_specs=[
            tok(0),
            pl.BlockSpec((POOL_HALO, PW), lambda i: (jnp.maximum(i * hb - 1, 0), 0)),
            pl.BlockSpec((tm, PW), lambda i: (i, 0)),
            tok(5), tok(6), tok(7), tok(8),
            pl.BlockSpec((tm, D), lambda i: (i, 0)),
            pl.BlockSpec((1, N_ADA, D), lambda i: (i // tps, 0, 0)),
            _const_spec(wpool_bf.shape),
            _const_spec(pscale.shape),
            _const_spec(wbp_bf.shape),
            _const_spec(wbr_bf.shape),
            _const_spec(wout_bf.shape),
            _const_spec(g2.shape),
            _const_spec(wr_cat.shape),
            _const_spec(brt.shape),
        ],
        out_specs=(pl.BlockSpec((tm, D), lambda i: (i, 0)),
                   pl.BlockSpec((tm, D), lambda i: (i, 0)),
                   pl.BlockSpec((tm, LANES), lambda i: (i, 0))),
        scratch_shapes=[pltpu.VMEM((POOL_HALO + tm, PW), F32),
                        pltpu.VMEM((tm, PW), BF16),
                        pltpu.VMEM((tm, D), BF16)],
        compiler_params=_cparams(("arbitrary",)),
        name="mix",
    )(proj, proj, r, proj, proj, proj, proj, x2d, mod3, wpool_bf, pscale, wbp_bf, wbr_bf,
      wout_bf, g2, wr_cat, brt)


def _route_kernel(lg_ref, tri_ref, col_ref, row_ref, cnt_ref, carry_ref):
    tm = lg_ref.shape[0]

    @pl.when(pl.program_id(0) == 0)
    def _():
        carry_ref[...] = jnp.zeros_like(carry_ref)

    L = lg_ref[...]
    lane = lax.broadcasted_iota(jnp.int32, (tm, LANES), 1)
    neg = -jnp.inf
    is_g = lane < N_GROUPS
    gl = jnp.where(is_g, L, neg)
    gmax = jnp.max(gl, axis=1, keepdims=True)
    grp = jnp.min(jnp.where(gl == gmax, lane, LANES), axis=1, keepdims=True)
    gsum = jnp.sum(jnp.where(is_g, jnp.exp(gl - gmax), 0.0), axis=1, keepdims=True)
    p_grp = 1.0 / gsum
    lo = ROUTE_LANE0 + grp * EXPERTS_PER_GROUP
    el = jnp.where((lane >= lo) & (lane < lo + EXPERTS_PER_GROUP), L, neg)
    v1 = jnp.max(el, axis=1, keepdims=True)
    i1 = jnp.min(jnp.where(el == v1, lane, LANES), axis=1, keepdims=True)
    el2 = jnp.where(lane == i1, neg, el)
    v2 = jnp.max(el2, axis=1, keepdims=True)
    i2 = jnp.min(jnp.where(el2 == v2, lane, LANES), axis=1, keepdims=True)
    e = jnp.exp(v2 - v1)
    w1 = p_grp / (1.0 + e)
    w2 = p_grp * e / (1.0 + e)
    sel1 = lane == i1
    sel2 = lane == i2
    onehot = jnp.where(sel1 | sel2, 1.0, 0.0)
    cum = jnp.dot(tri_ref[...], onehot.astype(BF16), preferred_element_type=F32)
    cum = cum + carry_ref[0:1, :]
    r1 = jnp.sum(jnp.where(sel1, cum, 0.0), axis=1, keepdims=True)
    r2 = jnp.sum(jnp.where(sel2, cum, 0.0), axis=1, keepdims=True)
    carry_ref[...] = carry_ref[...] + jnp.sum(onehot, axis=0, keepdims=True)
    cnt_ref[...] = carry_ref[...]
    slab = jnp.where(lane == 0, (i1 - ROUTE_LANE0).astype(F32), 0.0)
    slab = jnp.where(lane == 1, (i2 - ROUTE_LANE0).astype(F32), slab)
    slab = jnp.where(lane == 2, r1, slab)
    slab = jnp.where(lane == 3, r2, slab)
    slab = jnp.where(lane == 4, w1, slab)
    slab = jnp.where(lane == 5, w2, slab)
    col_ref[...] = slab
    row_ref[...] = slab.T[0:8, :]


def _route(logits, tm):
    T = logits.shape[0]
    tri = jnp.tril(jnp.ones((tm, tm), F32), -1).astype(BF16)
    return pl.pallas_call(
        _route_kernel,
        out_shape=(jax.ShapeDtypeStruct((T, LANES), F32),
                   jax.ShapeDtypeStruct((8, T), F32),
                   jax.ShapeDtypeStruct((8, LANES), F32)),
        grid=(T // tm,),
        in_specs=[pl.BlockSpec((tm, LANES), lambda i: (i, 0)),
                  pl.BlockSpec((tm, tm), lambda i: (0, 0))],
        out_specs=(pl.BlockSpec((tm, LANES), lambda i: (i, 0)),
                   pl.BlockSpec((8, tm), lambda i: (0, i)),
                   pl.BlockSpec((8, LANES), lambda i: (0, 0))),
        scratch_shapes=[pltpu.VMEM((8, LANES), F32)],
        compiler_params=_cparams(("arbitrary",)),
        name="route",
    )(logits, tri)


def _dest_kernel(pstart_ref, row_ref, o_ref):
    e = row_ref[0:2, :].astype(jnp.int32)
    r = row_ref[2:4, :].astype(jnp.int32)
    base = jnp.zeros_like(e)
    for k in range(N_EXPERTS):
        base = jnp.where(e == k, pstart_ref[k], base)
    o_ref[...] = jnp.zeros_like(o_ref)
    o_ref[0:2, :] = base + r


def _dest(pstart, row, tn):
    T = row.shape[1]
    grid_spec = pltpu.PrefetchScalarGridSpec(
        num_scalar_prefetch=1,
        grid=(T // tn,),
        in_specs=[pl.BlockSpec((SUBLANES, tn), lambda i, *_: (0, i))],
        out_specs=pl.BlockSpec((SUBLANES, tn), lambda i, *_: (0, i)),
    )
    return pl.pallas_call(
        _dest_kernel,
        out_shape=jax.ShapeDtypeStruct((SUBLANES, T), jnp.int32),
        grid_spec=grid_spec,
        compiler_params=_cparams(("arbitrary",)),
        name="dest",
    )(pstart, row)


def _tile_major(dest2, tm):
    T = dest2.shape[1]
    return dest2.reshape(2, T // tm, tm).transpose(1, 0, 2).reshape(-1)


def _row_ref(ref, row):
    return ref.at[lax.shift_right_logical(row, 3), pl.ds(jnp.bitwise_and(row, SUBLANES - 1), 1)]


def _dispatch_kernel(pstart_ref, pend_ref, nu_ref, dest_ref, h2_ref, buf_ref, zero_ref, sem):
    ng = h2_ref.shape[0]
    tm = ng * SUBLANES
    blk = zero_ref.shape[0]
    nb = buf_ref.shape[0] // blk

    @pl.when(pl.program_id(0) == 0)
    def _():
        zero_ref[...] = jnp.zeros_like(zero_ref)

        def zero_block(start):
            cp = pltpu.make_async_copy(zero_ref, buf_ref.at[pl.ds(start, blk)], sem)
            cp.start()
            cp.wait()

        def zbody(e, carry):
            @pl.when(pend_ref[e] > pstart_ref[e])
            def _():
                zero_block(lax.shift_right_logical(pend_ref[e], 3) - blk)
            return carry

        lax.fori_loop(0, N_EXPERTS, zbody, 0)

        def tail(j, carry):
            zero_block(j * blk)
            return carry

        lax.fori_loop(nu_ref[0], nb, tail, 0)

    def issue(g, carry):
        for u in range(SUBLANES):
            for k in range(2):
                dest = dest_ref[k * tm + g * SUBLANES + u]
                pltpu.make_async_copy(h2_ref.at[g, pl.ds(u, 1)], _row_ref(buf_ref, dest),
                                      sem).start(priority=k)
        return carry

    lax.fori_loop(0, ng, issue, 0, unroll=ROW_DMA_UNROLL)
    for k in range(2):
        pltpu.make_async_copy(h2_ref, buf_ref.at[pl.ds(0, ng)], sem).wait()


def _dispatch(pstart, pend, n_used, dest_flat, h2, n_rows, tm):
    G, _, D = h2.shape
    ng = tm // SUBLANES
    grid_spec = pltpu.PrefetchScalarGridSpec(
        num_scalar_prefetch=3,
        grid=(G // ng,),
        in_specs=[pl.BlockSpec((2 * tm,), lambda i, *_: (i,), memory_space=pltpu.SMEM),
                  pl.BlockSpec((ng, SUBLANES, D), lambda i, *_: (i, 0, 0))],
        out_specs=pl.BlockSpec(memory_space=pl.ANY),
        scratch_shapes=[pltpu.VMEM((EXPERT_ROWS // SUBLANES, SUBLANES, D), h2.dtype),
                        pltpu.SemaphoreType.DMA(())],
    )
    return pl.pallas_call(
        _dispatch_kernel,
        out_shape=jax.ShapeDtypeStruct((n_rows // SUBLANES, SUBLANES, D), h2.dtype),
        grid_spec=grid_spec,
        compiler_params=_cparams(("arbitrary",)),
        name="dispatch",
    )(pstart, pend, n_used, dest_flat, h2)


def _expert_kernel(be_ref, nu_ref, x_ref, w1_ref, w3_ref, w2_ref, o_ref):
    @pl.when(pl.program_id(0) < nu_ref[0])
    def _():
        x = x_ref[...].astype(BF16)
        a = jnp.dot(x, w1_ref[...], preferred_element_type=F32)
        b = jnp.dot(x, w3_ref[...], preferred_element_type=F32)
        hid = (a * jax.nn.sigmoid(a)) * b
        o_ref[...] = jnp.dot(hid.astype(BF16), w2_ref[...], preferred_element_type=F32)

    @pl.when(pl.program_id(0) >= nu_ref[0])
    def _():
        o_ref[...] = jnp.zeros_like(o_ref)


def _experts(block_e, n_used, buf, w1_bf, w3_bf, w2_bf):
    P, D = buf.shape
    F = w1_bf.shape[2]
    nb = P // EXPERT_ROWS

    def row_map(j, be, nu):
        return (jnp.minimum(j, nu[0] - 1), 0)

    grid_spec = pltpu.PrefetchScalarGridSpec(
        num_scalar_prefetch=2,
        grid=(nb,),
        in_specs=[pl.BlockSpec((EXPERT_ROWS, D), row_map),
                  pl.BlockSpec((None, D, F), lambda j, be, nu: (be[j], 0, 0)),
                  pl.BlockSpec((None, D, F), lambda j, be, nu: (be[j], 0, 0)),
                  pl.BlockSpec((None, F, D), lambda j, be, nu: (be[j], 0, 0))],
        out_specs=pl.BlockSpec((EXPERT_ROWS, D), lambda j, be, nu: (j, 0)),
    )
    return pl.pallas_call(
        _expert_kernel,
        out_shape=jax.ShapeDtypeStruct((P, D), F32),
        grid_spec=grid_spec,
        compiler_params=_cparams(("arbitrary",)),
        name="expert",
    )(block_e, n_used, buf, w1_bf, w3_bf, w2_bf)


def _final_kernel(dcur_ref, dnext_ref, col_ref, x1_ref, mod_ref, fg_ref, yb_ref, o_ref,
                  g_ref, sems):
    ng = x1_ref.shape[0]
    tm = ng * SUBLANES
    i = pl.program_id(0)
    buf = lax.rem(i, 2)

    def issue_tile(d_ref, b):
        def issue(g, carry):
            for u in range(SUBLANES):
                for k in range(2):
                    dest = d_ref[k * tm + g * SUBLANES + u]
                    pltpu.make_async_copy(_row_ref(yb_ref, dest), g_ref.at[b, k, g, pl.ds(u, 1)],
                                          sems.at[b]).start(priority=k)
            return carry

        lax.fori_loop(0, ng, issue, 0, unroll=ROW_DMA_UNROLL)

    @pl.when(i == 0)
    def _():
        issue_tile(dcur_ref, 0)

    @pl.when(i + 1 < pl.num_programs(0))
    def _():
        issue_tile(dnext_ref, 1 - buf)

    for k in range(2):
        pltpu.make_async_copy(yb_ref.at[pl.ds(0, ng)], g_ref.at[buf, k], sems.at[buf]).wait()

    w1 = col_ref[:, :, 4:5]
    w2 = col_ref[:, :, 5:6]
    y = g_ref[buf, 0] * w1 + g_ref[buf, 1] * w2
    x2 = x1_ref[...] + mod_ref[0, 5:6, :] * y
    ms = jnp.mean(x2 * x2, axis=-1, keepdims=True)
    o_ref[...] = (x2 * lax.rsqrt(ms + EPS)) * fg_ref[...]


def _final(dest_flat, col, x1, mod3, fgain, yb, seq, tm):
    G, _, D = x1.shape
    ng = tm // SUBLANES
    n_tiles = G // ng
    tps = seq // tm
    return pl.pallas_call(
        _final_kernel,
        out_shape=jax.ShapeDtypeStruct((G, SUBLANES, D), F32),
        grid=(n_tiles,),
        in_specs=[pl.BlockSpec((2 * tm,), lambda i: (i,), memory_space=pltpu.SMEM),
                  pl.BlockSpec((2 * tm,), lambda i: (jnp.minimum(i + 1, n_tiles - 1),),
                               memory_space=pltpu.SMEM),
                  pl.BlockSpec((ng, SUBLANES, LANES), lambda i: (i, 0, 0)),
                  pl.BlockSpec((ng, SUBLANES, D), lambda i: (i, 0, 0)),
                  pl.BlockSpec((1, N_ADA, D), lambda i: (i // tps, 0, 0)),
                  pl.BlockSpec((1, D), lambda i: (0, 0)),
                  pl.BlockSpec(memory_space=pl.ANY)],
        out_specs=pl.BlockSpec((ng, SUBLANES, D), lambda i: (i, 0, 0)),
        scratch_shapes=[pltpu.VMEM((2, 2, ng, SUBLANES, D), F32),
                        pltpu.SemaphoreType.DMA((2,))],
        compiler_params=_cparams(("arbitrary",)),
        name="final",
    )(dest_flat, dest_flat, col, x1, mod3, fgain, yb)


def _tile(n, pref):
    t = min(n, pref)
    assert n % t == 0, (n, t)
    return t


def kernel(x, c, positions, w_ada, b_ada, norm1_gain, w_in, w_pool, pool_scale, w_branch_pool,
           w_branch_ret, w_out, norm2_gain, w_group, b_group, w_router, b_router, w1, w3, w2,
           final_gain):
    B, S, D = x.shape
    T = B * S
    assert w_ada.shape[0] == 1, "only DEPTH == 1 is supported"
    x2d = x.reshape(T, D)
    posf = positions.astype(F32).reshape(T, 1)
    for l in range(1):
        mod3 = _ada(c, w_ada[l], b_ada[l][None, :]).reshape(B, N_ADA, D)

        n_exp, _, d_exp = w1[l].shape
        side = [w1[l].reshape(n_exp * D, d_exp), w3[l].reshape(n_exp * D, d_exp),
                w2[l].reshape(n_exp * d_exp, D)]
        proj, (w1_bf, w3_bf, w2_bf) = _inproj(x2d, mod3, norm1_gain[l][None, :],
                                              w_in[l].astype(BF16), side, S, _tile(S, 1024), 1024)
        w1_bf = w1_bf.reshape(n_exp, D, d_exp)
        w3_bf = w3_bf.reshape(n_exp, D, d_exp)
        w2_bf = w2_bf.reshape(n_exp, d_exp, D)
        r = _retention(posf, proj, B, S)

        w_rt = jnp.concatenate(
            [w_group[l], w_router[l],
             jnp.zeros((D, LANES - N_GROUPS - N_EXPERTS), F32)], axis=1)
        wr_hi = w_rt.astype(BF16)
        wr_lo = (w_rt - wr_hi.astype(F32)).astype(BF16)
        wr_cat = jnp.concatenate([wr_hi, wr_lo], axis=1)
        brt = jnp.concatenate(
            [b_group[l], b_router[l], jnp.zeros((LANES - N_GROUPS - N_EXPERTS,), F32)])[None, :]
        x1, h2, logits = _mix(
            proj, r, x2d, mod3, w_pool[l].astype(BF16), pool_scale[l][None, :],
            w_branch_pool[l].astype(BF16), w_branch_ret[l].astype(BF16), w_out[l].astype(BF16),
            norm2_gain[l][None, :], wr_cat, brt, S, _tile(S, 256))

        col, row, cnt = _route(logits, _tile(T, 512))
        counts = cnt[0, ROUTE_LANE0:ROUTE_LANE0 + N_EXPERTS].astype(jnp.int32)
        padded = (counts + EXPERT_ROWS - 1) // EXPERT_ROWS * EXPERT_ROWS
        pend = jnp.cumsum(padded)
        pstart = pend - padded
        n_rows = 2 * T + N_EXPERTS * EXPERT_ROWS
        nb = n_rows // EXPERT_ROWS
        n_used = (pend[-1:] // EXPERT_ROWS).astype(jnp.int32)
        block_row0 = jnp.arange(nb, dtype=jnp.int32) * EXPERT_ROWS
        block_e = jnp.minimum(
            jnp.sum((pend[None, :] <= block_row0[:, None]).astype(jnp.int32), axis=1),
            N_EXPERTS - 1)

        dest2 = _dest(pstart, row, _tile(T, 4096))[:2]
        tm_d = _tile(T, 1024)
        tm_f = _tile(S, 512)
        buf = _dispatch(pstart, pend, n_used, _tile_major(dest2, tm_d),
                        h2.reshape(T // SUBLANES, SUBLANES, D), n_rows, tm_d)
        yb = _experts(block_e, n_used, buf.reshape(n_rows, D), w1_bf, w3_bf, w2_bf)
        out = _final(_tile_major(dest2, tm_f), col.reshape(T // SUBLANES, SUBLANES, LANES),
                     x1.reshape(T // SUBLANES, SUBLANES, D), mod3, final_gain[None, :],
                     yb.reshape(n_rows // SUBLANES, SUBLANES, D), S, tm_f)
    return out.reshape(B, S, D)
```

```python
import functools

import jax
import jax.numpy as jnp
from jax import lax
from jax.experimental import pallas as pl
from jax.experimental.pallas import tpu as pltpu

F32 = jnp.float32
BF16 = jnp.bfloat16

EPS = 1e-6
ROPE_BASE = 10000.0
POOL_WINDOWS = (2, 4, 8, 16)
POOL_HALO = 16
RET_HEADS = 8
RET_CHUNK = 128
N_GROUPS = 4
EXPERTS_PER_GROUP = 8
N_EXPERTS = N_GROUPS * EXPERTS_PER_GROUP
N_ADA = 6
LANES = 128
ROUTE_LANE0 = N_GROUPS
EXPERT_ROWS = 256
SUBLANES = 8
ROW_DMA_UNROLL = 2
VMEM_LIMIT = 56 * 1024 * 1024


def _cparams(sem):
    return pltpu.CompilerParams(dimension_semantics=sem, vmem_limit_bytes=VMEM_LIMIT)


def _const_spec(shape):
    n = len(shape)
    return pl.BlockSpec(shape, lambda *_: (0,) * n, pipeline_mode=pl.Buffered(1))


def _ada_kernel(cb_ref, w_ref, b_ref, o_ref, cact_ref):
    cb = cb_ref[...]
    cact_ref[...] = cb * jax.nn.sigmoid(cb)
    tn = w_ref.shape[1]
    for b in range(cb_ref.shape[0]):
        for j in range(tn // LANES):
            sl = slice(j * LANES, (j + 1) * LANES)
            prod = w_ref[:, sl] * cact_ref[b]
            o_ref[b:b + 1, sl] = jnp.sum(prod, axis=0, keepdims=True) + b_ref[:, sl]


def _ada(c, w_ada, b_ada):
    B, D = c.shape
    N = w_ada.shape[1]
    tn = 1024
    cb = jnp.broadcast_to(c[:, :, None], (B, D, LANES))
    return pl.pallas_call(
        _ada_kernel,
        out_shape=jax.ShapeDtypeStruct((B, N), F32),
        grid=(N // tn,),
        in_specs=[
            pl.BlockSpec((B, D, LANES), lambda j: (0, 0, 0)),
            pl.BlockSpec((D, tn), lambda j: (0, j)),
            pl.BlockSpec((1, tn), lambda j: (0, j)),
        ],
        out_specs=pl.BlockSpec((B, tn), lambda j: (0, j)),
        scratch_shapes=[pltpu.VMEM((B, D, LANES), F32)],
        compiler_params=_cparams(("arbitrary",)),
        name="ada",
    )(cb, w_ada, b_ada)


def _norm_mod_rows(x, gain, shift, scale):
    ms = jnp.mean(x * x, axis=-1, keepdims=True)
    y = (x * lax.rsqrt(ms + EPS)) * gain
    return y * (1.0 + scale) + shift


def _inproj_kernel(x_ref, mod_ref, g_ref, w_ref, *rest, side_blocks):
    n_side = len(side_blocks)
    side_in = rest[:n_side]
    o_ref = rest[n_side]
    side_out = rest[n_side + 1:2 * n_side + 1]
    h_ref, r_ref = rest[2 * n_side + 1:]
    tm, d = x_ref.shape
    rc = 32

    @pl.when(pl.program_id(1) == 0)
    def _():
        shift = mod_ref[0, 0:1, :]
        mult = g_ref[...] * (1.0 + mod_ref[0, 1:2, :])

        def stats(c, carry):
            rows = pl.ds(pl.multiple_of(c * rc, rc), rc)
            x = x_ref[rows, :]
            r = lax.rsqrt(jnp.mean(x * x, axis=-1, keepdims=True) + EPS)
            r_ref[rows, :] = jnp.broadcast_to(r, (rc, LANES))
            return carry

        lax.fori_loop(0, tm // rc, stats, 0, unroll=4)

        def apply(c, carry):
            rows = pl.ds(pl.multiple_of(c * rc, rc), rc)
            r = r_ref[rows, :]
            for j in range(d // LANES):
                cols = slice(j * LANES, (j + 1) * LANES)
                h_ref[rows, cols] = ((x_ref[rows, cols] * r) * mult[:, cols]
                                     + shift[:, cols]).astype(BF16)
            return carry

        lax.fori_loop(0, tm // rc, apply, 0)

    o_ref[...] = jnp.dot(h_ref[...], w_ref[...], preferred_element_type=F32).astype(o_ref.dtype)

    step = pl.program_id(0) * pl.num_programs(1) + pl.program_id(1)
    for src_ref, dst_ref, n_blocks in zip(side_in, side_out, side_blocks):
        @pl.when(step < n_blocks)
        def _(src_ref=src_ref, dst_ref=dst_ref):
            dst_ref[...] = src_ref[...].astype(BF16)


SIDE_CAST_BLOCK_BYTES = 1024 * 1024


def _side_cast_rows(n_rows, n_cols, n_steps):
    rows = SIDE_CAST_BLOCK_BYTES // (4 * n_cols)
    if n_rows % rows == 0 and n_rows // rows <= n_steps:
        return rows
    return None


def _inproj(x2d, mod3, gain, w_in_bf, side, seq, tm, tn):
    T, D = x2d.shape
    N = w_in_bf.shape[1]
    tps = seq // tm
    nj = N // tn
    n_steps = (T // tm) * nj
    side_rows = [_side_cast_rows(a.shape[0], a.shape[1], n_steps) for a in side]
    if any(r is None for r in side_rows):
        out = _inproj(x2d, mod3, gain, w_in_bf, [], seq, tm, tn)
        return out[0], [a.astype(BF16) for a in side]
    side_blocks = tuple(a.shape[0] // r for a, r in zip(side, side_rows))

    def side_spec(a, r, nb):
        return pl.BlockSpec((r, a.shape[1]), lambda i, j, nb=nb: (jnp.minimum(i * nj + j, nb - 1), 0))

    side_specs = [side_spec(a, r, nb) for a, r, nb in zip(side, side_rows, side_blocks)]
    outs = pl.pallas_call(
        functools.partial(_inproj_kernel, side_blocks=side_blocks),
        out_shape=[jax.ShapeDtypeStruct((T, N), BF16)]
        + [jax.ShapeDtypeStruct(a.shape, BF16) for a in side],
        grid=(T // tm, nj),
        in_specs=[
            pl.BlockSpec((tm, D), lambda i, j: (i, 0)),
            pl.BlockSpec((1, N_ADA, D), lambda i, j: (i // tps, 0, 0)),
            pl.BlockSpec((1, D), lambda i, j: (0, 0)),
            pl.BlockSpec((D, tn), lambda i, j: (0, j)),
        ] + side_specs,
        out_specs=[pl.BlockSpec((tm, tn), lambda i, j: (i, j))] + side_specs,
        scratch_shapes=[pltpu.VMEM((tm, D), BF16), pltpu.VMEM((tm, LANES), F32)],
        compiler_params=_cparams(("arbitrary", "arbitrary")),
        name="inproj",
    )(x2d, mod3, gain, w_in_bf, *side)
    return outs[0], list(outs[1:])


def _ret_kernel(pos_ref, q_ref, k_ref, v_ref, rg_ref, invf_ref, sign_ref, mask_ref,
                qd_ref, kd_ref, cd_ref, o_ref, state_ref, *, k_scale):
    C = RET_CHUNK
    hc = C // 2
    nseq = q_ref.shape[0]

    @pl.when(pl.program_id(1) == 0)
    def _():
        state_ref[...] = jnp.zeros_like(state_ref)

    lo_lanes = lax.broadcasted_iota(jnp.int32, (hc, C), 1) < hc

    def spread(t):
        sw = pltpu.roll(t, hc, 1)
        return jnp.concatenate([jnp.where(lo_lanes, t, sw), jnp.where(lo_lanes, sw, t)], axis=0)

    trig = []
    for b in range(nseq):
        pos2 = jnp.where(lo_lanes, pos_ref[b, 0:hc, :], pos_ref[b, hc:C, :])
        ang = pos2 * invf_ref[...]
        trig.append((spread(jnp.cos(ang)), spread(jnp.sin(ang)) * sign_ref[...]))

    for h in range(RET_HEADS):
        cols = slice(h * C, (h + 1) * C)
        for b in range(nseq):
            cosv, sinv = trig[b]
            q = q_ref[b, :, cols].astype(F32)
            k = k_ref[b, :, cols].astype(F32)
            qr = q * cosv + pltpu.roll(q, hc, 1) * sinv
            kr = (k * cosv + pltpu.roll(k, hc, 1) * sinv) * k_scale
            v = v_ref[b, :, cols]
            s = lax.dot_general(qr.astype(BF16), kr.astype(BF16), (((1,), (1,)), ((), ())),
                                preferred_element_type=F32) * mask_ref[h]
            st = state_ref[b, h]
            lhs = jnp.concatenate([s.astype(BF16), (qr * qd_ref[h]).astype(BF16)], axis=1)
            rhs = jnp.concatenate([v, st.astype(BF16)], axis=0)
            o = jnp.dot(lhs, rhs, preferred_element_type=F32)
            kv = lax.dot_general((kr * kd_ref[h]).astype(BF16), v, (((0,), (0,)), ((), ())),
                                 preferred_element_type=F32)
            state_ref[b, h] = st * cd_ref[h] + kv
            o = o * lax.rsqrt(jnp.mean(o * o, axis=-1, keepdims=True) + EPS)
            g = rg_ref[b, :, cols].astype(F32)
            o_ref[b, :, cols] = (o * (g * jax.nn.sigmoid(g))).astype(o_ref.dtype)


RET_SEQS_PER_STEP = 2


def _retention(pos3, proj3):
    batch, seq, _ = proj3.shape
    H, C = RET_HEADS, RET_CHUNK
    W = H * C
    nseq = RET_SEQS_PER_STEP if batch % RET_SEQS_PER_STEP == 0 else 1
    spb = seq // C
    half = C // 2
    inv_freq = ROPE_BASE ** (-jnp.arange(half, dtype=F32) / half)
    invf2 = jnp.concatenate([inv_freq, inv_freq])[None, :]
    sign = jnp.concatenate([-jnp.ones((half,), F32), jnp.ones((half,), F32)])[None, :]
    log_gamma = jnp.log1p(-jnp.exp2(-5.0 - jnp.arange(H, dtype=F32)))
    idx = jnp.arange(C, dtype=F32)
    diff = idx[:, None] - idx[None, :]
    mask = jnp.where(diff >= 0, jnp.exp(log_gamma[:, None, None] * jnp.maximum(diff, 0.0)), 0.0)
    q_decay = jnp.exp(log_gamma[:, None] * (idx + 1.0))
    k_decay = jnp.exp(log_gamma[:, None] * (C - 1.0 - idx))
    chunk_decay = jnp.exp(log_gamma * C)
    qd = jnp.broadcast_to(q_decay[:, :, None], (H, C, C))
    kd = jnp.broadcast_to(k_decay[:, :, None], (H, C, C))
    cd = jnp.broadcast_to(chunk_decay[:, None, None], (H, 1, C))

    def tok(col):
        return pl.BlockSpec((nseq, C, W), lambda b, n, col=col: (b, n, col))

    return pl.pallas_call(
        functools.partial(_ret_kernel, k_scale=float(C) ** -0.5),
        out_shape=jax.ShapeDtypeStruct((batch, seq, W), BF16),
        grid=(batch // nseq, spb),
        in_specs=[
            pl.BlockSpec((nseq, C, 1), lambda b, n: (b, n, 0)),
            tok(1), tok(2), tok(3), tok(4),
            pl.BlockSpec((1, C), lambda b, n: (0, 0)),
            pl.BlockSpec((1, C), lambda b, n: (0, 0)),
            pl.BlockSpec((H, C, C), lambda b, n: (0, 0, 0)),
            pl.BlockSpec((H, C, C), lambda b, n: (0, 0, 0)),
            pl.BlockSpec((H, C, C), lambda b, n: (0, 0, 0)),
            pl.BlockSpec((H, 1, C), lambda b, n: (0, 0, 0)),
        ],
        out_specs=pl.BlockSpec((nseq, C, W), lambda b, n: (b, n, 0)),
        scratch_shapes=[pltpu.VMEM((nseq, H, C, C), F32)],
        compiler_params=_cparams(("arbitrary", "arbitrary")),
        name="ret",
    )(pos3, proj3, proj3, proj3, proj3, invf2, sign, mask, qd, kd, cd)


def _mix_kernel(a_ref, halo_ref, r_ref, ga0_ref, ga1_ref, gb0_ref, gb1_ref, x_ref, mod_ref,
                wpool_ref, pscale_ref, wbp_ref, wbr_ref, wout_ref, g2_ref,
                wrc_ref, brt_ref,
                x1_ref, h2_ref, lg_ref, ext_ref, pm_ref, mg_ref, *, tiles_per_seq):
    tm = a_ref.shape[0]
    pw = a_ref.shape[1]
    gd = pw // len(POOL_WINDOWS)
    i = pl.program_id(0)
    it = i % tiles_per_seq
    halo = halo_ref[...].astype(F32)
    ext_ref[0:POOL_HALO, :] = jnp.where(it == 0, 0.0, halo)
    ext_ref[POOL_HALO:, :] = a_ref[...].astype(F32)
    tpos = it * tm + lax.broadcasted_iota(jnp.int32, (tm, 1), 0)
    for g, w in enumerate(POOL_WINDOWS):
        cols = slice(g * gd, (g + 1) * gd)
        cur = ext_ref[POOL_HALO:POOL_HALO + tm, cols]
        s = cur
        for j in range(1, w):
            s = s + ext_ref[POOL_HALO - j:POOL_HALO - j + tm, cols]
        cnt = jnp.minimum(tpos + 1, w).astype(F32)
        pooled = s / cnt - cur
        pm = jnp.dot(pooled.astype(BF16), wpool_ref[g], preferred_element_type=F32)
        pm_ref[:, cols] = (pm * pscale_ref[:, cols]).astype(BF16)

    y_pool = jnp.dot(pm_ref[...], wbp_ref[...], preferred_element_type=F32)
    y_ret = jnp.dot(r_ref[...], wbr_ref[...], preferred_element_type=F32)
    half = y_pool.shape[1] // 2
    for hh, (ga_ref, gb_ref) in enumerate(((ga0_ref, gb0_ref), (ga1_ref, gb1_ref))):
        cols = slice(hh * half, (hh + 1) * half)
        ga = jax.nn.sigmoid(ga_ref[...].astype(F32))
        gb = jax.nn.sigmoid(gb_ref[...].astype(F32))
        mg_ref[:, cols] = (ga * y_pool[:, cols] + gb * y_ret[:, cols]).astype(BF16)
    z = jnp.dot(mg_ref[...], wout_ref[...], preferred_element_type=F32)
    gate1 = mod_ref[0, 2:3, :]
    x1 = x_ref[...] + gate1 * z
    x1_ref[...] = x1
    h2 = _norm_mod_rows(x1, g2_ref[...], mod_ref[0, 3:4, :], mod_ref[0, 4:5, :])
    h2_ref[...] = h2
    h_hi = h2.astype(BF16)
    h_lo = (h2 - h_hi.astype(F32)).astype(BF16)
    parts = jnp.dot(jnp.concatenate([h_hi, h_lo], axis=0), wrc_ref[...],
                    preferred_element_type=F32)
    lg = (parts[:tm, :LANES] + parts[tm:, :LANES]) + (parts[:tm, LANES:] + parts[tm:, LANES:])
    lg_ref[...] = lg + brt_ref[...]


def _mix(proj, r, x2d, mod3, wpool_bf, pscale, wbp_bf, wbr_bf, wout_bf, g2, wr_cat, brt,
         seq, tm):
    T, D = x2d.shape
    PW = wbp_bf.shape[0]
    tps = seq // tm
    hb = tm // POOL_HALO

    def tok(col):
        return pl.BlockSpec((tm, PW), lambda i, col=col: (i, col))

    kern = functools.partial(_mix_kernel, tiles_per_seq=tps)
    return pl.pallas_call(
        kern,
        out_shape=(jax.ShapeDtypeStruct((T, D), F32),
                   jax.ShapeDtypeStruct((T, D), F32),
                   jax.ShapeDtypeStruct((T, LANES), F32)),
        grid=(T // tm,),
        in_specs=[
            tok(0),
            pl.BlockSpec((POOL_HALO, PW), lambda i: (jnp.maximum(i * hb - 1, 0), 0)),
            pl.BlockSpec((tm, PW), lambda i: (i, 0)),
            tok(5), tok(6), tok(7), tok(8),
            pl.BlockSpec((tm, D), lambda i: (i, 0)),
            pl.BlockSpec((1, N_ADA, D), lambda i: (i // tps, 0, 0)),
            _const_spec(wpool_bf.shape),
            _const_spec(pscale.shape),
            _const_spec(wbp_bf.shape),
            _const_spec(wbr_bf.shape),
            _const_spec(wout_bf.shape),
            _const_spec(g2.shape),
            _const_spec(wr_cat.shape),
            _const_spec(brt.shape),
        ],
        out_specs=(pl.BlockSpec((tm, D), lambda i: (i, 0)),
                   pl.BlockSpec((tm, D), lambda i: (i, 0)),
                   pl.BlockSpec((tm, LANES), lambda i: (i, 0))),
        scratch_shapes=[pltpu.VMEM((POOL_HALO + tm, PW), F32),
                        pltpu.VMEM((tm, PW), BF16),
                        pltpu.VMEM((tm, D), BF16)],
        compiler_params=_cparams(("arbitrary",)),
        name="mix",
    )(proj, proj, r, proj, proj, proj, proj, x2d, mod3, wpool_bf, pscale, wbp_bf, wbr_bf,
      wout_bf, g2, wr_cat, brt)


def _route_kernel(lg_ref, tri_ref, col_ref, row_ref, cnt_ref, carry_ref):
    tm = lg_ref.shape[0]

    @pl.when(pl.program_id(0) == 0)
    def _():
        carry_ref[...] = jnp.zeros_like(carry_ref)

    L = lg_ref[...]
    lane = lax.broadcasted_iota(jnp.int32, (tm, LANES), 1)
    neg = -jnp.inf
    is_g = lane < N_GROUPS
    gl = jnp.where(is_g, L, neg)
    gmax = jnp.max(gl, axis=1, keepdims=True)
    grp = jnp.min(jnp.where(gl == gmax, lane, LANES), axis=1, keepdims=True)
    gsum = jnp.sum(jnp.where(is_g, jnp.exp(gl - gmax), 0.0), axis=1, keepdims=True)
    p_grp = 1.0 / gsum
    lo = ROUTE_LANE0 + grp * EXPERTS_PER_GROUP
    el = jnp.where((lane >= lo) & (lane < lo + EXPERTS_PER_GROUP), L, neg)
    v1 = jnp.max(el, axis=1, keepdims=True)
    i1 = jnp.min(jnp.where(el == v1, lane, LANES), axis=1, keepdims=True)
    el2 = jnp.where(lane == i1, neg, el)
    v2 = jnp.max(el2, axis=1, keepdims=True)
    i2 = jnp.min(jnp.where(el2 == v2, lane, LANES), axis=1, keepdims=True)
    e = jnp.exp(v2 - v1)
    w1 = p_grp / (1.0 + e)
    w2 = p_grp * e / (1.0 + e)
    sel1 = lane == i1
    sel2 = lane == i2
    onehot = jnp.where(sel1 | sel2, 1.0, 0.0)
    cum = jnp.dot(tri_ref[...], onehot.astype(BF16), preferred_element_type=F32)
    cum = cum + carry_ref[0:1, :]
    r1 = jnp.sum(jnp.where(sel1, cum, 0.0), axis=1, keepdims=True)
    r2 = jnp.sum(jnp.where(sel2, cum, 0.0), axis=1, keepdims=True)
    carry_ref[...] = carry_ref[...] + jnp.sum(onehot, axis=0, keepdims=True)
    cnt_ref[...] = carry_ref[...]
    slab = jnp.where(lane == 0, (i1 - ROUTE_LANE0).astype(F32), 0.0)
    slab = jnp.where(lane == 1, (i2 - ROUTE_LANE0).astype(F32), slab)
    slab = jnp.where(lane == 2, r1, slab)
    slab = jnp.where(lane == 3, r2, slab)
    slab = jnp.where(lane == 4, w1, slab)
    slab = jnp.where(lane == 5, w2, slab)
    col_ref[...] = slab
    row_ref[...] = slab.T[0:8, :]


def _route(logits, tm):
    T = logits.shape[0]
    tri = jnp.tril(jnp.ones((tm, tm), F32), -1).astype(BF16)
    return pl.pallas_call(
        _route_kernel,
        out_shape=(jax.ShapeDtypeStruct((T, LANES), F32),
                   jax.ShapeDtypeStruct((8, T), F32),
                   jax.ShapeDtypeStruct((8, LANES), F32)),
        grid=(T // tm,),
        in_specs=[pl.BlockSpec((tm, LANES), lambda i: (i, 0)),
                  pl.BlockSpec((tm, tm), lambda i: (0, 0))],
        out_specs=(pl.BlockSpec((tm, LANES), lambda i: (i, 0)),
                   pl.BlockSpec((8, tm), lambda i: (0, i)),
                   pl.BlockSpec((8, LANES), lambda i: (0, 0))),
        scratch_shapes=[pltpu.VMEM((8, LANES), F32)],
        compiler_params=_cparams(("arbitrary",)),
        name="route",
    )(logits, tri)


def _dest_kernel(pstart_ref, row_ref, o_ref):
    e = row_ref[0:2, :].astype(jnp.int32)
    r = row_ref[2:4, :].astype(jnp.int32)
    base = jnp.zeros_like(e)
    for k in range(N_EXPERTS):
        base = jnp.where(e == k, pstart_ref[k], base)
    o_ref[...] = jnp.zeros_like(o_ref)
    o_ref[0:2, :] = base + r


def _dest(pstart, row, tn):
    T = row.shape[1]
    grid_spec = pltpu.PrefetchScalarGridSpec(
        num_scalar_prefetch=1,
        grid=(T // tn,),
        in_specs=[pl.BlockSpec((SUBLANES, tn), lambda i, *_: (0, i))],
        out_specs=pl.BlockSpec((SUBLANES, tn), lambda i, *_: (0, i)),
    )
    return pl.pallas_call(
        _dest_kernel,
        out_shape=jax.ShapeDtypeStruct((SUBLANES, T), jnp.int32),
        grid_spec=grid_spec,
        compiler_params=_cparams(("arbitrary",)),
        name="dest",
    )(pstart, row)


def _tile_major(dest2, tm):
    T = dest2.shape[1]
    return dest2.reshape(2, T // tm, tm).transpose(1, 0, 2).reshape(-1)


def _row_ref(ref, row):
    return ref.at[lax.shift_right_logical(row, 3), pl.ds(jnp.bitwise_and(row, SUBLANES - 1), 1)]


def _dispatch_kernel(pstart_ref, pend_ref, nu_ref, dest_ref, h2_ref, buf_ref, zero_ref, sem):
    ng = h2_ref.shape[0]
    tm = ng * SUBLANES
    blk = zero_ref.shape[0]
    nb = buf_ref.shape[0] // blk

    @pl.when(pl.program_id(0) == 0)
    def _():
        zero_ref[...] = jnp.zeros_like(zero_ref)

        def zero_block(start):
            cp = pltpu.make_async_copy(zero_ref, buf_ref.at[pl.ds(start, blk)], sem)
            cp.start()
            cp.wait()

        def zbody(e, carry):
            @pl.when(pend_ref[e] > pstart_ref[e])
            def _():
                zero_block(lax.shift_right_logical(pend_ref[e], 3) - blk)
            return carry

        lax.fori_loop(0, N_EXPERTS, zbody, 0)

        def tail(j, carry):
            zero_block(j * blk)
            return carry

        lax.fori_loop(nu_ref[0], nb, tail, 0)

    def issue(g, carry):
        for u in range(SUBLANES):
            for k in range(2):
                dest = dest_ref[k * tm + g * SUBLANES + u]
                pltpu.make_async_copy(h2_ref.at[g, pl.ds(u, 1)], _row_ref(buf_ref, dest),
                                      sem).start(priority=k)
        return carry

    lax.fori_loop(0, ng, issue, 0, unroll=ROW_DMA_UNROLL)
    for k in range(2):
        pltpu.make_async_copy(h2_ref, buf_ref.at[pl.ds(0, ng)], sem).wait()


def _dispatch(pstart, pend, n_used, dest_flat, h2, n_rows, tm):
    G, _, D = h2.shape
    ng = tm // SUBLANES
    grid_spec = pltpu.PrefetchScalarGridSpec(
        num_scalar_prefetch=3,
        grid=(G // ng,),
        in_specs=[pl.BlockSpec((2 * tm,), lambda i, *_: (i,), memory_space=pltpu.SMEM),
                  pl.BlockSpec((ng, SUBLANES, D), lambda i, *_: (i, 0, 0))],
        out_specs=pl.BlockSpec(memory_space=pl.ANY),
        scratch_shapes=[pltpu.VMEM((EXPERT_ROWS // SUBLANES, SUBLANES, D), h2.dtype),
                        pltpu.SemaphoreType.DMA(())],
    )
    return pl.pallas_call(
        _dispatch_kernel,
        out_shape=jax.ShapeDtypeStruct((n_rows // SUBLANES, SUBLANES, D), h2.dtype),
        grid_spec=grid_spec,
        compiler_params=_cparams(("arbitrary",)),
        name="dispatch",
    )(pstart, pend, n_used, dest_flat, h2)


def _expert_kernel(be_ref, nu_ref, x_ref, w1_ref, w3_ref, w2_ref, o_ref):
    @pl.when(pl.program_id(0) < nu_ref[0])
    def _():
        x = x_ref[...].astype(BF16)
        a = jnp.dot(x, w1_ref[...], preferred_element_type=F32)
        b = jnp.dot(x, w3_ref[...], preferred_element_type=F32)
        hid = (a * jax.nn.sigmoid(a)) * b
        o_ref[...] = jnp.dot(hid.astype(BF16), w2_ref[...], preferred_element_type=F32)

    @pl.when(pl.program_id(0) >= nu_ref[0])
    def _():
        o_ref[...] = jnp.zeros_like(o_ref)


def _experts(block_e, n_used, buf, w1_bf, w3_bf, w2_bf):
    P, D = buf.shape
    F = w1_bf.shape[2]
    nb = P // EXPERT_ROWS

    def row_map(j, be, nu):
        return (jnp.minimum(j, nu[0] - 1), 0)

    grid_spec = pltpu.PrefetchScalarGridSpec(
        num_scalar_prefetch=2,
        grid=(nb,),
        in_specs=[pl.BlockSpec((EXPERT_ROWS, D), row_map),
                  pl.BlockSpec((None, D, F), lambda j, be, nu: (be[j], 0, 0)),
                  pl.BlockSpec((None, D, F), lambda j, be, nu: (be[j], 0, 0)),
                  pl.BlockSpec((None, F, D), lambda j, be, nu: (be[j], 0, 0))],
        out_specs=pl.BlockSpec((EXPERT_ROWS, D), lambda j, be, nu: (j, 0)),
    )
    return pl.pallas_call(
        _expert_kernel,
        out_shape=jax.ShapeDtypeStruct((P, D), F32),
        grid_spec=grid_spec,
        compiler_params=_cparams(("arbitrary",)),
        name="expert",
    )(block_e, n_used, buf, w1_bf, w3_bf, w2_bf)


def _final_kernel(dcur_ref, dnext_ref, col_ref, x1_ref, mod_ref, fg_ref, yb_ref, o_ref,
                  g_ref, sems):
    ng = x1_ref.shape[0]
    tm = ng * SUBLANES
    i = pl.program_id(0)
    buf = lax.rem(i, 2)

    def issue_tile(d_ref, b):
        def issue(g, carry):
            for u in range(SUBLANES):
                for k in range(2):
                    dest = d_ref[k * tm + g * SUBLANES + u]
                    pltpu.make_async_copy(_row_ref(yb_ref, dest), g_ref.at[b, k, g, pl.ds(u, 1)],
                                          sems.at[b]).start(priority=k)
            return carry

        lax.fori_loop(0, ng, issue, 0, unroll=ROW_DMA_UNROLL)

    @pl.when(i == 0)
    def _():
        issue_tile(dcur_ref, 0)

    @pl.when(i + 1 < pl.num_programs(0))
    def _():
        issue_tile(dnext_ref, 1 - buf)

    for k in range(2):
        pltpu.make_async_copy(yb_ref.at[pl.ds(0, ng)], g_ref.at[buf, k], sems.at[buf]).wait()

    w1 = col_ref[:, :, 4:5]
    w2 = col_ref[:, :, 5:6]
    y = g_ref[buf, 0] * w1 + g_ref[buf, 1] * w2
    x2 = x1_ref[...] + mod_ref[0, 5:6, :] * y
    ms = jnp.mean(x2 * x2, axis=-1, keepdims=True)
    o_ref[...] = (x2 * lax.rsqrt(ms + EPS)) * fg_ref[...]


def _final(dest_flat, col, x1, mod3, fgain, yb, seq, tm):
    G, _, D = x1.shape
    ng = tm // SUBLANES
    n_tiles = G // ng
    tps = seq // tm
    return pl.pallas_call(
        _final_kernel,
        out_shape=jax.ShapeDtypeStruct((G, SUBLANES, D), F32),
        grid=(n_tiles,),
        in_specs=[pl.BlockSpec((2 * tm,), lambda i: (i,), memory_space=pltpu.SMEM),
                  pl.BlockSpec((2 * tm,), lambda i: (jnp.minimum(i + 1, n_tiles - 1),),
                               memory_space=pltpu.SMEM),
                  pl.BlockSpec((ng, SUBLANES, LANES), lambda i: (i, 0, 0)),
                  pl.BlockSpec((ng, SUBLANES, D), lambda i: (i, 0, 0)),
                  pl.BlockSpec((1, N_ADA, D), lambda i: (i // tps, 0, 0)),
                  pl.BlockSpec((1, D), lambda i: (0, 0)),
                  pl.BlockSpec(memory_space=pl.ANY)],
        out_specs=pl.BlockSpec((ng, SUBLANES, D), lambda i: (i, 0, 0)),
        scratch_shapes=[pltpu.VMEM((2, 2, ng, SUBLANES, D), F32),
                        pltpu.SemaphoreType.DMA((2,))],
        compiler_params=_cparams(("arbitrary",)),
        name="final",
    )(dest_flat, dest_flat, col, x1, mod3, fgain, yb)


def _tile(n, pref):
    t = min(n, pref)
    assert n % t == 0, (n, t)
    return t


def kernel(x, c, positions, w_ada, b_ada, norm1_gain, w_in, w_pool, pool_scale, w_branch_pool,
           w_branch_ret, w_out, norm2_gain, w_group, b_group, w_router, b_router, w1, w3, w2,
           final_gain):
    B, S, D = x.shape
    T = B * S
    assert w_ada.shape[0] == 1, "only DEPTH == 1 is supported"
    x2d = x.reshape(T, D)
    posf = positions.astype(F32).reshape(T, 1)
    for l in range(1):
        mod3 = _ada(c, w_ada[l], b_ada[l][None, :]).reshape(B, N_ADA, D)

        n_exp, _, d_exp = w1[l].shape
        side = [w1[l].reshape(n_exp * D, d_exp), w3[l].reshape(n_exp * D, d_exp),
                w2[l].reshape(n_exp * d_exp, D)]
        proj, (w1_bf, w3_bf, w2_bf) = _inproj(x2d, mod3, norm1_gain[l][None, :],
                                              w_in[l].astype(BF16), side, S, _tile(S, 1024), 1024)
        w1_bf = w1_bf.reshape(n_exp, D, d_exp)
        w3_bf = w3_bf.reshape(n_exp, D, d_exp)
        w2_bf = w2_bf.reshape(n_exp, d_exp, D)
        r = _retention(posf.reshape(B, S, 1), proj.reshape(B, S, -1)).reshape(T, -1)

        w_rt = jnp.concatenate(
            [w_group[l], w_router[l],
             jnp.zeros((D, LANES - N_GROUPS - N_EXPERTS), F32)], axis=1)
        wr_hi = w_rt.astype(BF16)
        wr_lo = (w_rt - wr_hi.astype(F32)).astype(BF16)
        wr_cat = jnp.concatenate([wr_hi, wr_lo], axis=1)
        brt = jnp.concatenate(
            [b_group[l], b_router[l], jnp.zeros((LANES - N_GROUPS - N_EXPERTS,), F32)])[None, :]
        x1, h2, logits = _mix(
            proj, r, x2d, mod3, w_pool[l].astype(BF16), pool_scale[l][None, :],
            w_branch_pool[l].astype(BF16), w_branch_ret[l].astype(BF16), w_out[l].astype(BF16),
            norm2_gain[l][None, :], wr_cat, brt, S, _tile(S, 256))

        col, row, cnt = _route(logits, _tile(T, 512))
        counts = cnt[0, ROUTE_LANE0:ROUTE_LANE0 + N_EXPERTS].astype(jnp.int32)
        padded = (counts + EXPERT_ROWS - 1) // EXPERT_ROWS * EXPERT_ROWS
        pend = jnp.cumsum(padded)
        pstart = pend - padded
        n_rows = 2 * T + N_EXPERTS * EXPERT_ROWS
        nb = n_rows // EXPERT_ROWS
        n_used = (pend[-1:] // EXPERT_ROWS).astype(jnp.int32)
        block_row0 = jnp.arange(nb, dtype=jnp.int32) * EXPERT_ROWS
        block_e = jnp.minimum(
            jnp.sum((pend[None, :] <= block_row0[:, None]).astype(jnp.int32), axis=1),
            N_EXPERTS - 1)

        dest2 = _dest(pstart, row, _tile(T, 4096))[:2]
        tm_d = _tile(T, 1024)
        tm_f = _tile(S, 512)
        buf = _dispatch(pstart, pend, n_used, _tile_major(dest2, tm_d),
                        h2.reshape(T // SUBLANES, SUBLANES, D), n_rows, tm_d)
        yb = _experts(block_e, n_used, buf.reshape(n_rows, D), w1_bf, w3_bf, w2_bf)
        out = _final(_tile_major(dest2, tm_f), col.reshape(T // SUBLANES, SUBLANES, LANES),
                     x1.reshape(T // SUBLANES, SUBLANES, D), mod3, final_gain[None, :],
                     yb.reshape(n_rows // SUBLANES, SUBLANES, D), S, tm_f)
    return out.reshape(B, S, D)
```

```python
import functools

import jax
import jax.numpy as jnp
from jax import lax
from jax.experimental import pallas as pl
from jax.experimental.pallas import tpu as pltpu

F32 = jnp.float32
BF16 = jnp.bfloat16

EPS = 1e-6
ROPE_BASE = 10000.0
POOL_WINDOWS = (2, 4, 8, 16)
POOL_HALO = 16
RET_HEADS = 8
RET_CHUNK = 128
N_GROUPS = 4
EXPERTS_PER_GROUP = 8
N_EXPERTS = N_GROUPS * EXPERTS_PER_GROUP
N_ADA = 6
LANES = 128
ROUTE_LANE0 = N_GROUPS
EXPERT_ROWS = 256
SUBLANES = 8
ROW_DMA_UNROLL = 2
VMEM_LIMIT = 56 * 1024 * 1024


def _cparams(sem):
    return pltpu.CompilerParams(dimension_semantics=sem, vmem_limit_bytes=VMEM_LIMIT)


def _const_spec(shape):
    n = len(shape)
    return pl.BlockSpec(shape, lambda *_: (0,) * n, pipeline_mode=pl.Buffered(1))


def _ada_kernel(cb_ref, w_ref, b_ref, o_ref, cact_ref):
    cb = cb_ref[...]
    cact_ref[...] = cb * jax.nn.sigmoid(cb)
    tn = w_ref.shape[1]
    for b in range(cb_ref.shape[0]):
        for j in range(tn // LANES):
            sl = slice(j * LANES, (j + 1) * LANES)
            prod = w_ref[:, sl] * cact_ref[b]
            o_ref[b:b + 1, sl] = jnp.sum(prod, axis=0, keepdims=True) + b_ref[:, sl]


def _ada(c, w_ada, b_ada):
    B, D = c.shape
    N = w_ada.shape[1]
    tn = 1024
    cb = jnp.broadcast_to(c[:, :, None], (B, D, LANES))
    return pl.pallas_call(
        _ada_kernel,
        out_shape=jax.ShapeDtypeStruct((B, N), F32),
        grid=(N // tn,),
        in_specs=[
            pl.BlockSpec((B, D, LANES), lambda j: (0, 0, 0)),
            pl.BlockSpec((D, tn), lambda j: (0, j)),
            pl.BlockSpec((1, tn), lambda j: (0, j)),
        ],
        out_specs=pl.BlockSpec((B, tn), lambda j: (0, j)),
        scratch_shapes=[pltpu.VMEM((B, D, LANES), F32)],
        compiler_params=_cparams(("arbitrary",)),
        name="ada",
    )(cb, w_ada, b_ada)


def _norm_mod_rows(x, gain, shift, scale):
    ms = jnp.mean(x * x, axis=-1, keepdims=True)
    y = (x * lax.rsqrt(ms + EPS)) * gain
    return y * (1.0 + scale) + shift


def _inproj_kernel(x_ref, mod_ref, g_ref, w_ref, *rest, side_blocks):
    n_side = len(side_blocks)
    side_in = rest[:n_side]
    o_ref = rest[n_side]
    side_out = rest[n_side + 1:2 * n_side + 1]
    h_ref, r_ref = rest[2 * n_side + 1:]
    tm, d = x_ref.shape
    rc = 32

    @pl.when(pl.program_id(1) == 0)
    def _():
        shift = mod_ref[0, 0:1, :]
        mult = g_ref[...] * (1.0 + mod_ref[0, 1:2, :])

        def stats(c, carry):
            rows = pl.ds(pl.multiple_of(c * rc, rc), rc)
            x = x_ref[rows, :]
            r = lax.rsqrt(jnp.mean(x * x, axis=-1, keepdims=True) + EPS)
            r_ref[rows, :] = jnp.broadcast_to(r, (rc, LANES))
            return carry

        lax.fori_loop(0, tm // rc, stats, 0, unroll=4)

        def apply(c, carry):
            rows = pl.ds(pl.multiple_of(c * rc, rc), rc)
            r = r_ref[rows, :]
            for j in range(d // LANES):
                cols = slice(j * LANES, (j + 1) * LANES)
                h_ref[rows, cols] = ((x_ref[rows, cols] * r) * mult[:, cols]
                                     + shift[:, cols]).astype(BF16)
            return carry

        lax.fori_loop(0, tm // rc, apply, 0)

    o_ref[...] = jnp.dot(h_ref[...], w_ref[...], preferred_element_type=F32).astype(o_ref.dtype)

    step = pl.program_id(0) * pl.num_programs(1) + pl.program_id(1)
    for src_ref, dst_ref, n_blocks in zip(side_in, side_out, side_blocks):
        @pl.when(step < n_blocks)
        def _(src_ref=src_ref, dst_ref=dst_ref):
            dst_ref[...] = src_ref[...].astype(BF16)


SIDE_CAST_BLOCK_BYTES = 1024 * 1024


def _side_cast_rows(n_rows, n_cols, n_steps):
    rows = SIDE_CAST_BLOCK_BYTES // (4 * n_cols)
    if n_rows % rows == 0 and n_rows // rows <= n_steps:
        return rows
    return None


def _inproj(x2d, mod3, gain, w_in_bf, side, seq, tm, tn):
    T, D = x2d.shape
    N = w_in_bf.shape[1]
    tps = seq // tm
    nj = N // tn
    n_steps = (T // tm) * nj
    side_rows = [_side_cast_rows(a.shape[0], a.shape[1], n_steps) for a in side]
    if any(r is None for r in side_rows):
        out = _inproj(x2d, mod3, gain, w_in_bf, [], seq, tm, tn)
        return out[0], [a.astype(BF16) for a in side]
    side_blocks = tuple(a.shape[0] // r for a, r in zip(side, side_rows))

    def side_spec(a, r, nb):
        return pl.BlockSpec((r, a.shape[1]), lambda i, j, nb=nb: (jnp.minimum(i * nj + j, nb - 1), 0))

    side_specs = [side_spec(a, r, nb) for a, r, nb in zip(side, side_rows, side_blocks)]
    outs = pl.pallas_call(
        functools.partial(_inproj_kernel, side_blocks=side_blocks),
        out_shape=[jax.ShapeDtypeStruct((T, N), BF16)]
        + [jax.ShapeDtypeStruct(a.shape, BF16) for a in side],
        grid=(T // tm, nj),
        in_specs=[
            pl.BlockSpec((tm, D), lambda i, j: (i, 0)),
            pl.BlockSpec((1, N_ADA, D), lambda i, j: (i // tps, 0, 0)),
            pl.BlockSpec((1, D), lambda i, j: (0, 0)),
            pl.BlockSpec((D, tn), lambda i, j: (0, j)),
        ] + side_specs,
        out_specs=[pl.BlockSpec((tm, tn), lambda i, j: (i, j))] + side_specs,
        scratch_shapes=[pltpu.VMEM((tm, D), BF16), pltpu.VMEM((tm, LANES), F32)],
        compiler_params=_cparams(("arbitrary", "arbitrary")),
        name="inproj",
    )(x2d, mod3, gain, w_in_bf, *side)
    return outs[0], list(outs[1:])


def _ret_kernel(pos_ref, q_ref, k_ref, v_ref, rg_ref, invf_ref, sign_ref, mask_ref,
                qd_ref, kd_ref, cd_ref, o_ref, state_ref, *, k_scale):
    C = RET_CHUNK
    hc = C // 2
    nseq = q_ref.shape[0]

    @pl.when(pl.program_id(1) == 0)
    def _():
        state_ref[...] = jnp.zeros_like(state_ref)

    lo_lanes = lax.broadcasted_iota(jnp.int32, (hc, C), 1) < hc

    def spread(t):
        sw = pltpu.roll(t, hc, 1)
        return jnp.concatenate([jnp.where(lo_lanes, t, sw), jnp.where(lo_lanes, sw, t)], axis=0)

    trig = []
    for b in range(nseq):
        pos2 = jnp.where(lo_lanes, pos_ref[b, 0:hc, :], pos_ref[b, hc:C, :])
        ang = pos2 * invf_ref[...]
        trig.append((spread(jnp.cos(ang)), spread(jnp.sin(ang)) * sign_ref[...]))

    for h in range(RET_HEADS):
        cols = slice(h * C, (h + 1) * C)
        for b in range(nseq):
            cosv, sinv = trig[b]
            q = q_ref[b, :, cols].astype(F32)
            k = k_ref[b, :, cols].astype(F32)
            qr = q * cosv + pltpu.roll(q, hc, 1) * sinv
            kr = (k * cosv + pltpu.roll(k, hc, 1) * sinv) * k_scale
            v = v_ref[b, :, cols]
            s = lax.dot_general(qr.astype(BF16), kr.astype(BF16), (((1,), (1,)), ((), ())),
                                preferred_element_type=F32) * mask_ref[h]
            st = state_ref[b, h]
            lhs = jnp.concatenate([s.astype(BF16), (qr * qd_ref[h]).astype(BF16)], axis=1)
            rhs = jnp.concatenate([v, st.astype(BF16)], axis=0)
            o = jnp.dot(lhs, rhs, preferred_element_type=F32)
            kv = lax.dot_general((kr * kd_ref[h]).astype(BF16), v, (((0,), (0,)), ((), ())),
                                 preferred_element_type=F32)
            state_ref[b, h] = st * cd_ref[h] + kv
            o = o * lax.rsqrt(jnp.mean(o * o, axis=-1, keepdims=True) + EPS)
            g = rg_ref[b, :, cols].astype(F32)
            o_ref[b, :, cols] = (o * (g * jax.nn.sigmoid(g))).astype(o_ref.dtype)


RET_SEQS_PER_STEP = 2


def _retention(pos3, proj3):
    batch, seq, _ = proj3.shape
    H, C = RET_HEADS, RET_CHUNK
    W = H * C
    nseq = RET_SEQS_PER_STEP if batch % RET_SEQS_PER_STEP == 0 else 1
    spb = seq // C
    half = C // 2
    inv_freq = ROPE_BASE ** (-jnp.arange(half, dtype=F32) / half)
    invf2 = jnp.concatenate([inv_freq, inv_freq])[None, :]
    sign = jnp.concatenate([-jnp.ones((half,), F32), jnp.ones((half,), F32)])[None, :]
    log_gamma = jnp.log1p(-jnp.exp2(-5.0 - jnp.arange(H, dtype=F32)))
    idx = jnp.arange(C, dtype=F32)
    diff = idx[:, None] - idx[None, :]
    mask = jnp.where(diff >= 0, jnp.exp(log_gamma[:, None, None] * jnp.maximum(diff, 0.0)), 0.0)
    q_decay = jnp.exp(log_gamma[:, None] * (idx + 1.0))
    k_decay = jnp.exp(log_gamma[:, None] * (C - 1.0 - idx))
    chunk_decay = jnp.exp(log_gamma * C)
    qd = jnp.broadcast_to(q_decay[:, :, None], (H, C, C))
    kd = jnp.broadcast_to(k_decay[:, :, None], (H, C, C))
    cd = jnp.broadcast_to(chunk_decay[:, None, None], (H, 1, C))

    def tok(col):
        return pl.BlockSpec((nseq, C, W), lambda b, n, col=col: (b, n, col))

    return pl.pallas_call(
        functools.partial(_ret_kernel, k_scale=float(C) ** -0.5),
        out_shape=jax.ShapeDtypeStruct((batch, seq, W), BF16),
        grid=(batch // nseq, spb),
        in_specs=[
            pl.BlockSpec((nseq, C, 1), lambda b, n: (b, n, 0)),
            tok(1), tok(2), tok(3), tok(4),
            pl.BlockSpec((1, C), lambda b, n: (0, 0)),
            pl.BlockSpec((1, C), lambda b, n: (0, 0)),
            pl.BlockSpec((H, C, C), lambda b, n: (0, 0, 0)),
            pl.BlockSpec((H, C, C), lambda b, n: (0, 0, 0)),
            pl.BlockSpec((H, C, C), lambda b, n: (0, 0, 0)),
            pl.BlockSpec((H, 1, C), lambda b, n: (0, 0, 0)),
        ],
        out_specs=pl.BlockSpec((nseq, C, W), lambda b, n: (b, n, 0)),
        scratch_shapes=[pltpu.VMEM((nseq, H, C, C), F32)],
        compiler_params=_cparams(("arbitrary", "arbitrary")),
        name="ret",
    )(pos3, proj3, proj3, proj3, proj3, invf2, sign, mask, qd, kd, cd)


def _mix_kernel(a_ref, halo_ref, r_ref, ga0_ref, ga1_ref, gb0_ref, gb1_ref, x_ref, mod_ref,
                wpool_ref, pscale_ref, wbp_ref, wbr_ref, wout_ref, g2_ref,
                wrc_ref, brt_ref,
                x1_ref, h2_ref, lg_ref, ext_ref, pm_ref, mg_ref, *, tiles_per_seq):
    tm = a_ref.shape[0]
    pw = a_ref.shape[1]
    gd = pw // len(POOL_WINDOWS)
    i = pl.program_id(0)
    it = i % tiles_per_seq
    halo = halo_ref[...].astype(F32)
    ext_ref[0:POOL_HALO, :] = jnp.where(it == 0, 0.0, halo)
    ext_ref[POOL_HALO:, :] = a_ref[...].astype(F32)
    tpos = it * tm + lax.broadcasted_iota(jnp.int32, (tm, 1), 0)
    for g, w in enumerate(POOL_WINDOWS):
        cols = slice(g * gd, (g + 1) * gd)
        cur = ext_ref[POOL_HALO:POOL_HALO + tm, cols]
        s = cur
        for j in range(1, w):
            s = s + ext_ref[POOL_HALO - j:POOL_HALO - j + tm, cols]
        cnt = jnp.minimum(tpos + 1, w).astype(F32)
        pooled = s / cnt - cur
        pm = jnp.dot(pooled.astype(BF16), wpool_ref[g], preferred_element_type=F32)
        pm_ref[:, cols] = (pm * pscale_ref[:, cols]).astype(BF16)

    y_pool = jnp.dot(pm_ref[...], wbp_ref[...], preferred_element_type=F32)
    y_ret = jnp.dot(r_ref[...], wbr_ref[...], preferred_element_type=F32)
    half = y_pool.shape[1] // 2
    for hh, (ga_ref, gb_ref) in enumerate(((ga0_ref, gb0_ref), (ga1_ref, gb1_ref))):
        cols = slice(hh * half, (hh + 1) * half)
        ga = jax.nn.sigmoid(ga_ref[...].astype(F32))
        gb = jax.nn.sigmoid(gb_ref[...].astype(F32))
        mg_ref[:, cols] = (ga * y_pool[:, cols] + gb * y_ret[:, cols]).astype(BF16)
    z = jnp.dot(mg_ref[...], wout_ref[...], preferred_element_type=F32)
    gate1 = mod_ref[0, 2:3, :]
    x1 = x_ref[...] + gate1 * z
    x1_ref[...] = x1
    h2 = _norm_mod_rows(x1, g2_ref[...], mod_ref[0, 3:4, :], mod_ref[0, 4:5, :])
    h2_ref[...] = h2
    h_hi = h2.astype(BF16)
    h_lo = (h2 - h_hi.astype(F32)).astype(BF16)
    parts = jnp.dot(jnp.concatenate([h_hi, h_lo], axis=0), wrc_ref[...],
                    preferred_element_type=F32)
    lg = (parts[:tm, :LANES] + parts[tm:, :LANES]) + (parts[:tm, LANES:] + parts[tm:, LANES:])
    lg_ref[...] = lg + brt_ref[...]


def _mix(proj, r, x2d, mod3, wpool_bf, pscale, wbp_bf, wbr_bf, wout_bf, g2, wr_cat, brt,
         seq, tm):
    T, D = x2d.shape
    PW = wbp_bf.shape[0]
    tps = seq // tm
    hb = tm // POOL_HALO

    def tok(col):
        return pl.BlockSpec((tm, PW), lambda i, col=col: (i, col))

    kern = functools.partial(_mix_kernel, tiles_per_seq=tps)
    return pl.pallas_call(
        kern,
        out_shape=(jax.ShapeDtypeStruct((T, D), F32),
                   jax.ShapeDtypeStruct((T, D), F32),
                   jax.ShapeDtypeStruct((T, LANES), F32)),
        grid=(T // tm,),
        in_specs=[
            tok(0),
            pl.BlockSpec((POOL_HALO, PW), lambda i: (jnp.maximum(i * hb - 1, 0), 0)),
            pl.BlockSpec((tm, PW), lambda i: (i, 0)),
            tok(5), tok(6), tok(7), tok(8),
            pl.BlockSpec((tm, D), lambda i: (i, 0)),
            pl.BlockSpec((1, N_ADA, D), lambda i: (i // tps, 0, 0)),
            _const_spec(wpool_bf.shape),
            _const_spec(pscale.shape),
            _const_spec(wbp_bf.shape),
            _const_spec(wbr_bf.shape),
            _const_spec(wout_bf.shape),
            _const_spec(g2.shape),
            _const_spec(wr_cat.shape),
            _const_spec(brt.shape),
        ],
        out_specs=(pl.BlockSpec((tm, D), lambda i: (i, 0)),
                   pl.BlockSpec((tm, D), lambda i: (i, 0)),
                   pl.BlockSpec((tm, LANES), lambda i: (i, 0))),
        scratch_shapes=[pltpu.VMEM((POOL_HALO + tm, PW), F32),
                        pltpu.VMEM((tm, PW), BF16),
                        pltpu.VMEM((tm, D), BF16)],
        compiler_params=_cparams(("arbitrary",)),
        name="mix",
    )(proj, proj, r, proj, proj, proj, proj, x2d, mod3, wpool_bf, pscale, wbp_bf, wbr_bf,
      wout_bf, g2, wr_cat, brt)


def _route_kernel(lg_ref, tri_ref, col_ref, row_ref, cnt_ref, carry_ref):
    tm = lg_ref.shape[0]

    @pl.when(pl.program_id(0) == 0)
    def _():
        carry_ref[...] = jnp.zeros_like(carry_ref)

    L = lg_ref[...]
    lane = lax.broadcasted_iota(jnp.int32, (tm, LANES), 1).astype(F32)
    neg = -jnp.inf
    none = float(LANES)
    is_g = lane < N_GROUPS
    gl = jnp.where(is_g, L, neg)
    gmax = jnp.max(gl, axis=1, keepdims=True)
    grp = jnp.min(jnp.where(gl == gmax, lane, none), axis=1, keepdims=True)
    gsum = jnp.sum(jnp.where(is_g, jnp.exp(gl - gmax), 0.0), axis=1, keepdims=True)
    p_grp = 1.0 / gsum
    lo = ROUTE_LANE0 + grp * EXPERTS_PER_GROUP
    el = jnp.where((lane >= lo) & (lane < lo + EXPERTS_PER_GROUP), L, neg)
    v1 = jnp.max(el, axis=1, keepdims=True)
    i1 = jnp.min(jnp.where(el == v1, lane, none), axis=1, keepdims=True)
    el2 = jnp.where(lane == i1, neg, el)
    v2 = jnp.max(el2, axis=1, keepdims=True)
    i2 = jnp.min(jnp.where(el2 == v2, lane, none), axis=1, keepdims=True)
    e = jnp.exp(v2 - v1)
    w1 = p_grp / (1.0 + e)
    w2 = p_grp * e / (1.0 + e)
    sel1 = lane == i1
    sel2 = lane == i2
    onehot = jnp.where(sel1 | sel2, 1.0, 0.0)
    cum = jnp.dot(tri_ref[...], onehot.astype(BF16), preferred_element_type=F32)
    cum = cum + carry_ref[0:1, :]
    r1 = jnp.sum(jnp.where(sel1, cum, 0.0), axis=1, keepdims=True)
    r2 = jnp.sum(jnp.where(sel2, cum, 0.0), axis=1, keepdims=True)
    carry_ref[...] = carry_ref[...] + jnp.sum(onehot, axis=0, keepdims=True)
    cnt_ref[...] = carry_ref[...]
    slab = jnp.where(lane == 0, i1 - ROUTE_LANE0, 0.0)
    slab = jnp.where(lane == 1, i2 - ROUTE_LANE0, slab)
    slab = jnp.where(lane == 2, r1, slab)
    slab = jnp.where(lane == 3, r2, slab)
    slab = jnp.where(lane == 4, w1, slab)
    slab = jnp.where(lane == 5, w2, slab)
    col_ref[...] = slab
    row_ref[...] = slab.T[0:8, :]


def _route(logits, tm):
    T = logits.shape[0]
    tri = jnp.tril(jnp.ones((tm, tm), F32), -1).astype(BF16)
    return pl.pallas_call(
        _route_kernel,
        out_shape=(jax.ShapeDtypeStruct((T, LANES), F32),
                   jax.ShapeDtypeStruct((8, T), F32),
                   jax.ShapeDtypeStruct((8, LANES), F32)),
        grid=(T // tm,),
        in_specs=[pl.BlockSpec((tm, LANES), lambda i: (i, 0)),
                  pl.BlockSpec((tm, tm), lambda i: (0, 0))],
        out_specs=(pl.BlockSpec((tm, LANES), lambda i: (i, 0)),
                   pl.BlockSpec((8, tm), lambda i: (0, i)),
                   pl.BlockSpec((8, LANES), lambda i: (0, 0))),
        scratch_shapes=[pltpu.VMEM((8, LANES), F32)],
        compiler_params=_cparams(("arbitrary",)),
        name="route",
    )(logits, tri)


def _dest_kernel(pstart_ref, row_ref, o_ref):
    e = row_ref[0:2, :].astype(jnp.int32)
    r = row_ref[2:4, :].astype(jnp.int32)
    base = jnp.zeros_like(e)
    for k in range(N_EXPERTS):
        base = jnp.where(e == k, pstart_ref[k], base)
    o_ref[...] = jnp.zeros_like(o_ref)
    o_ref[0:2, :] = base + r


def _dest(pstart, row, tn):
    T = row.shape[1]
    grid_spec = pltpu.PrefetchScalarGridSpec(
        num_scalar_prefetch=1,
        grid=(T // tn,),
        in_specs=[pl.BlockSpec((SUBLANES, tn), lambda i, *_: (0, i))],
        out_specs=pl.BlockSpec((SUBLANES, tn), lambda i, *_: (0, i)),
    )
    return pl.pallas_call(
        _dest_kernel,
        out_shape=jax.ShapeDtypeStruct((SUBLANES, T), jnp.int32),
        grid_spec=grid_spec,
        compiler_params=_cparams(("arbitrary",)),
        name="dest",
    )(pstart, row)


def _tile_major(dest2, tm):
    T = dest2.shape[1]
    return dest2.reshape(2, T // tm, tm).transpose(1, 0, 2).reshape(-1)


def _row_ref(ref, row):
    return ref.at[lax.shift_right_logical(row, 3), pl.ds(jnp.bitwise_and(row, SUBLANES - 1), 1)]


def _dispatch_kernel(pstart_ref, pend_ref, nu_ref, dest_ref, h2_ref, buf_ref, zero_ref, sem):
    ng = h2_ref.shape[0]
    tm = ng * SUBLANES
    blk = zero_ref.shape[0]
    nb = buf_ref.shape[0] // blk

    @pl.when(pl.program_id(0) == 0)
    def _():
        zero_ref[...] = jnp.zeros_like(zero_ref)

        def zero_copy(start):
            return pltpu.make_async_copy(zero_ref, buf_ref.at[pl.ds(start, blk)], sem)

        def zbody(e, n_started):
            used = pend_ref[e] > pstart_ref[e]

            @pl.when(used)
            def _():
                zero_copy(lax.shift_right_logical(pend_ref[e], 3) - blk).start()
            return n_started + used.astype(jnp.int32)

        n_tails = lax.fori_loop(0, N_EXPERTS, zbody, 0)

        def tail(j, carry):
            zero_copy(j * blk).start()
            return carry

        lax.fori_loop(nu_ref[0], nb, tail, 0)

        def drain(i, carry):
            zero_copy(0).wait()
            return carry

        lax.fori_loop(0, n_tails + (nb - nu_ref[0]), drain, 0)

    def issue(g, carry):
        for u in range(SUBLANES):
            for k in range(2):
                dest = dest_ref[k * tm + g * SUBLANES + u]
                pltpu.make_async_copy(h2_ref.at[g, pl.ds(u, 1)], _row_ref(buf_ref, dest),
                                      sem).start(priority=k)
        return carry

    lax.fori_loop(0, ng, issue, 0, unroll=ROW_DMA_UNROLL)
    for k in range(2):
        pltpu.make_async_copy(h2_ref, buf_ref.at[pl.ds(0, ng)], sem).wait()


def _dispatch(pstart, pend, n_used, dest_flat, h2, n_rows, tm):
    G, _, D = h2.shape
    ng = tm // SUBLANES
    grid_spec = pltpu.PrefetchScalarGridSpec(
        num_scalar_prefetch=3,
        grid=(G // ng,),
        in_specs=[pl.BlockSpec((2 * tm,), lambda i, *_: (i,), memory_space=pltpu.SMEM),
                  pl.BlockSpec((ng, SUBLANES, D), lambda i, *_: (i, 0, 0))],
        out_specs=pl.BlockSpec(memory_space=pl.ANY),
        scratch_shapes=[pltpu.VMEM((EXPERT_ROWS // SUBLANES, SUBLANES, D), h2.dtype),
                        pltpu.SemaphoreType.DMA(())],
    )
    return pl.pallas_call(
        _dispatch_kernel,
        out_shape=jax.ShapeDtypeStruct((n_rows // SUBLANES, SUBLANES, D), h2.dtype),
        grid_spec=grid_spec,
        compiler_params=_cparams(("arbitrary",)),
        name="dispatch",
    )(pstart, pend, n_used, dest_flat, h2)


def _expert_kernel(be_ref, nu_ref, x_ref, w1_ref, w3_ref, w2_ref, o_ref):
    @pl.when(pl.program_id(0) < nu_ref[0])
    def _():
        x = x_ref[...].astype(BF16)
        a = jnp.dot(x, w1_ref[...], preferred_element_type=F32)
        b = jnp.dot(x, w3_ref[...], preferred_element_type=F32)
        hid = (a * jax.nn.sigmoid(a)) * b
        o_ref[...] = jnp.dot(hid.astype(BF16), w2_ref[...], preferred_element_type=F32)

    @pl.when(pl.program_id(0) >= nu_ref[0])
    def _():
        o_ref[...] = jnp.zeros_like(o_ref)


def _experts(block_e, n_used, buf, w1_bf, w3_bf, w2_bf):
    P, D = buf.shape
    F = w1_bf.shape[2]
    nb = P // EXPERT_ROWS

    def row_map(j, be, nu):
        return (jnp.minimum(j, nu[0] - 1), 0)

    grid_spec = pltpu.PrefetchScalarGridSpec(
        num_scalar_prefetch=2,
        grid=(nb,),
        in_specs=[pl.BlockSpec((EXPERT_ROWS, D), row_map),
                  pl.BlockSpec((None, D, F), lambda j, be, nu: (be[j], 0, 0)),
                  pl.BlockSpec((None, D, F), lambda j, be, nu: (be[j], 0, 0)),
                  pl.BlockSpec((None, F, D), lambda j, be, nu: (be[j], 0, 0))],
        out_specs=pl.BlockSpec((EXPERT_ROWS, D), lambda j, be, nu: (j, 0)),
    )
    return pl.pallas_call(
        _expert_kernel,
        out_shape=jax.ShapeDtypeStruct((P, D), F32),
        grid_spec=grid_spec,
        compiler_params=_cparams(("arbitrary",)),
        name="expert",
    )(block_e, n_used, buf, w1_bf, w3_bf, w2_bf)


def _final_kernel(dcur_ref, dnext_ref, col_ref, x1_ref, mod_ref, fg_ref, yb_ref, o_ref,
                  g_ref, sems):
    ng = x1_ref.shape[0]
    tm = ng * SUBLANES
    i = pl.program_id(0)
    buf = lax.rem(i, 2)

    def issue_tile(d_ref, b):
        def issue(g, carry):
            for u in range(SUBLANES):
                for k in range(2):
                    dest = d_ref[k * tm + g * SUBLANES + u]
                    pltpu.make_async_copy(_row_ref(yb_ref, dest), g_ref.at[b, k, g, pl.ds(u, 1)],
                                          sems.at[b]).start(priority=k)
            return carry

        lax.fori_loop(0, ng, issue, 0, unroll=ROW_DMA_UNROLL)

    @pl.when(i == 0)
    def _():
        issue_tile(dcur_ref, 0)

    @pl.when(i + 1 < pl.num_programs(0))
    def _():
        issue_tile(dnext_ref, 1 - buf)

    for k in range(2):
        pltpu.make_async_copy(yb_ref.at[pl.ds(0, ng)], g_ref.at[buf, k], sems.at[buf]).wait()

    w1 = col_ref[:, :, 4:5]
    w2 = col_ref[:, :, 5:6]
    y = g_ref[buf, 0] * w1 + g_ref[buf, 1] * w2
    x2 = x1_ref[...] + mod_ref[0, 5:6, :] * y
    ms = jnp.mean(x2 * x2, axis=-1, keepdims=True)
    o_ref[...] = (x2 * lax.rsqrt(ms + EPS)) * fg_ref[...]


def _final(dest_flat, col, x1, mod3, fgain, yb, seq, tm):
    G, _, D = x1.shape
    ng = tm // SUBLANES
    n_tiles = G // ng
    tps = seq // tm
    return pl.pallas_call(
        _final_kernel,
        out_shape=jax.ShapeDtypeStruct((G, SUBLANES, D), F32),
        grid=(n_tiles,),
        in_specs=[pl.BlockSpec((2 * tm,), lambda i: (i,), memory_space=pltpu.SMEM),
                  pl.BlockSpec((2 * tm,), lambda i: (jnp.minimum(i + 1, n_tiles - 1),),
                               memory_space=pltpu.SMEM),
                  pl.BlockSpec((ng, SUBLANES, LANES), lambda i: (i, 0, 0)),
                  pl.BlockSpec((ng, SUBLANES, D), lambda i: (i, 0, 0)),
                  pl.BlockSpec((1, N_ADA, D), lambda i: (i // tps, 0, 0)),
                  pl.BlockSpec((1, D), lambda i: (0, 0)),
                  pl.BlockSpec(memory_space=pl.ANY)],
        out_specs=pl.BlockSpec((ng, SUBLANES, D), lambda i: (i, 0, 0)),
        scratch_shapes=[pltpu.VMEM((2, 2, ng, SUBLANES, D), F32),
                        pltpu.SemaphoreType.DMA((2,))],
        compiler_params=_cparams(("arbitrary",)),
        name="final",
    )(dest_flat, dest_flat, col, x1, mod3, fgain, yb)


def _tile(n, pref):
    t = min(n, pref)
    assert n % t == 0, (n, t)
    return t


def kernel(x, c, positions, w_ada, b_ada, norm1_gain, w_in, w_pool, pool_scale, w_branch_pool,
           w_branch_ret, w_out, norm2_gain, w_group, b_group, w_router, b_router, w1, w3, w2,
           final_gain):
    B, S, D = x.shape
    T = B * S
    assert w_ada.shape[0] == 1, "only DEPTH == 1 is supported"
    x2d = x.reshape(T, D)
    posf = positions.astype(F32).reshape(T, 1)
    for l in range(1):
        mod3 = _ada(c, w_ada[l], b_ada[l][None, :]).reshape(B, N_ADA, D)

        n_exp, _, d_exp = w1[l].shape
        side = [w1[l].reshape(n_exp * D, d_exp), w3[l].reshape(n_exp * D, d_exp),
                w2[l].reshape(n_exp * d_exp, D)]
        proj, (w1_bf, w3_bf, w2_bf) = _inproj(x2d, mod3, norm1_gain[l][None, :],
                                              w_in[l].astype(BF16), side, S, _tile(S, 1024), 1024)
        w1_bf = w1_bf.reshape(n_exp, D, d_exp)
        w3_bf = w3_bf.reshape(n_exp, D, d_exp)
        w2_bf = w2_bf.reshape(n_exp, d_exp, D)
        r = _retention(posf.reshape(B, S, 1), proj.reshape(B, S, -1)).reshape(T, -1)

        w_rt = jnp.concatenate(
            [w_group[l], w_router[l],
             jnp.zeros((D, LANES - N_GROUPS - N_EXPERTS), F32)], axis=1)
        wr_hi = w_rt.astype(BF16)
        wr_lo = (w_rt - wr_hi.astype(F32)).astype(BF16)
        wr_cat = jnp.concatenate([wr_hi, wr_lo], axis=1)
        brt = jnp.concatenate(
            [b_group[l], b_router[l], jnp.zeros((LANES - N_GROUPS - N_EXPERTS,), F32)])[None, :]
        x1, h2, logits = _mix(
            proj, r, x2d, mod3, w_pool[l].astype(BF16), pool_scale[l][None, :],
            w_branch_pool[l].astype(BF16), w_branch_ret[l].astype(BF16), w_out[l].astype(BF16),
            norm2_gain[l][None, :], wr_cat, brt, S, _tile(S, 256))

        col, row, cnt = _route(logits, _tile(T, 512))
        counts = cnt[0, ROUTE_LANE0:ROUTE_LANE0 + N_EXPERTS].astype(jnp.int32)
        padded = (counts + EXPERT_ROWS - 1) // EXPERT_ROWS * EXPERT_ROWS
        pend = jnp.cumsum(padded)
        pstart = pend - padded
        n_rows = 2 * T + N_EXPERTS * EXPERT_ROWS
        nb = n_rows // EXPERT_ROWS
        n_used = (pend[-1:] // EXPERT_ROWS).astype(jnp.int32)
        block_row0 = jnp.arange(nb, dtype=jnp.int32) * EXPERT_ROWS
        block_e = jnp.minimum(
            jnp.sum((pend[None, :] <= block_row0[:, None]).astype(jnp.int32), axis=1),
            N_EXPERTS - 1)

        dest2 = _dest(pstart, row, _tile(T, 4096))[:2]
        tm_d = _tile(T, 1024)
        tm_f = _tile(S, 512)
        buf = _dispatch(pstart, pend, n_used, _tile_major(dest2, tm_d),
                        h2.reshape(T // SUBLANES, SUBLANES, D), n_rows, tm_d)
        yb = _experts(block_e, n_used, buf.reshape(n_rows, D), w1_bf, w3_bf, w2_bf)
        out = _final(_tile_major(dest2, tm_f), col.reshape(T // SUBLANES, SUBLANES, LANES),
                     x1.reshape(T // SUBLANES, SUBLANES, D), mod3, final_gain[None, :],
                     yb.reshape(n_rows // SUBLANES, SUBLANES, D), S, tm_f)
    return out.reshape(B, S, D)
```

```python
import functools

import jax
import jax.numpy as jnp
from jax import lax
from jax.experimental import pallas as pl
from jax.experimental.pallas import tpu as pltpu

F32 = jnp.float32
BF16 = jnp.bfloat16

EPS = 1e-6
ROPE_BASE = 10000.0
POOL_WINDOWS = (2, 4, 8, 16)
POOL_HALO = 32
RET_HEADS = 8
RET_CHUNK = 128
N_GROUPS = 4
EXPERTS_PER_GROUP = 8
N_EXPERTS = N_GROUPS * EXPERTS_PER_GROUP
N_ADA = 6
LANES = 128
ROUTE_LANE0 = N_GROUPS
EXPERT_ROWS = 256
SUBLANES = 8
ROW_DMA_UNROLL = 2
VMEM_LIMIT = 56 * 1024 * 1024


def _cparams(sem):
    return pltpu.CompilerParams(dimension_semantics=sem, vmem_limit_bytes=VMEM_LIMIT)


def _const_spec(shape):
    n = len(shape)
    return pl.BlockSpec(shape, lambda *_: (0,) * n, pipeline_mode=pl.Buffered(1))


def _ada_kernel(cb_ref, w_ref, b_ref, o_ref, cact_ref):
    cb = cb_ref[...]
    cact_ref[...] = cb * jax.nn.sigmoid(cb)
    tn = w_ref.shape[1]
    for b in range(cb_ref.shape[0]):
        for j in range(tn // LANES):
            sl = slice(j * LANES, (j + 1) * LANES)
            prod = w_ref[:, sl] * cact_ref[b]
            o_ref[b:b + 1, sl] = jnp.sum(prod, axis=0, keepdims=True) + b_ref[:, sl]


def _ada(c, w_ada, b_ada):
    B, D = c.shape
    N = w_ada.shape[1]
    tn = 1024
    cb = jnp.broadcast_to(c[:, :, None], (B, D, LANES))
    return pl.pallas_call(
        _ada_kernel,
        out_shape=jax.ShapeDtypeStruct((B, N), F32),
        grid=(N // tn,),
        in_specs=[
            pl.BlockSpec((B, D, LANES), lambda j: (0, 0, 0)),
            pl.BlockSpec((D, tn), lambda j: (0, j)),
            pl.BlockSpec((1, tn), lambda j: (0, j)),
        ],
        out_specs=pl.BlockSpec((B, tn), lambda j: (0, j)),
        scratch_shapes=[pltpu.VMEM((B, D, LANES), F32)],
        compiler_params=_cparams(("arbitrary",)),
        name="ada",
    )(cb, w_ada, b_ada)


def _norm_mod_rows(x, gain, shift, scale):
    ms = jnp.mean(x * x, axis=-1, keepdims=True)
    y = (x * lax.rsqrt(ms + EPS)) * gain
    return y * (1.0 + scale) + shift


def _inproj_kernel(x_ref, mod_ref, g_ref, w_ref, *rest, side_blocks):
    n_side = len(side_blocks)
    side_in = rest[:n_side]
    o_ref = rest[n_side]
    side_out = rest[n_side + 1:2 * n_side + 1]
    h_ref, r_ref = rest[2 * n_side + 1:]
    tm, d = x_ref.shape
    rc = 32

    @pl.when(pl.program_id(1) == 0)
    def _():
        shift = mod_ref[0, 0:1, :]
        mult = g_ref[...] * (1.0 + mod_ref[0, 1:2, :])

        def stats(c, carry):
            rows = pl.ds(pl.multiple_of(c * rc, rc), rc)
            x = x_ref[rows, :]
            r = lax.rsqrt(jnp.mean(x * x, axis=-1, keepdims=True) + EPS)
            r_ref[rows, :] = jnp.broadcast_to(r, (rc, LANES))
            return carry

        lax.fori_loop(0, tm // rc, stats, 0, unroll=4)

        def apply(c, carry):
            rows = pl.ds(pl.multiple_of(c * rc, rc), rc)
            r = r_ref[rows, :]
            for j in range(d // LANES):
                cols = slice(j * LANES, (j + 1) * LANES)
                h_ref[rows, cols] = ((x_ref[rows, cols] * r) * mult[:, cols]
                                     + shift[:, cols]).astype(BF16)
            return carry

        lax.fori_loop(0, tm // rc, apply, 0)

    for src_ref, dst_ref in zip(side_in, side_out):
        dst_ref[...] = src_ref[...].astype(BF16)

    o_ref[...] = jnp.dot(h_ref[...], w_ref[...], preferred_element_type=F32).astype(o_ref.dtype)


SIDE_CAST_BLOCK_BYTES = 1024 * 1024


def _side_cast_rows(n_rows, n_cols, n_steps):
    rows = SIDE_CAST_BLOCK_BYTES // (4 * n_cols)
    if n_rows % rows == 0 and n_rows // rows <= n_steps:
        return rows
    return None


def _inproj(x2d, mod3, gain, w_in_bf, side, seq, tm, tn):
    T, D = x2d.shape
    N = w_in_bf.shape[1]
    tps = seq // tm
    nj = N // tn
    n_steps = (T // tm) * nj
    side_rows = [_side_cast_rows(a.shape[0], a.shape[1], n_steps) for a in side]
    if any(r is None for r in side_rows):
        out = _inproj(x2d, mod3, gain, w_in_bf, [], seq, tm, tn)
        return out[0], [a.astype(BF16) for a in side]
    side_blocks = tuple(a.shape[0] // r for a, r in zip(side, side_rows))

    def side_spec(a, r, nb):
        return pl.BlockSpec((r, a.shape[1]), lambda i, j, nb=nb: (jnp.minimum(i * nj + j, nb - 1), 0))

    side_specs = [side_spec(a, r, nb) for a, r, nb in zip(side, side_rows, side_blocks)]
    outs = pl.pallas_call(
        functools.partial(_inproj_kernel, side_blocks=side_blocks),
        out_shape=[jax.ShapeDtypeStruct((T, N), BF16)]
        + [jax.ShapeDtypeStruct(a.shape, BF16) for a in side],
        grid=(T // tm, nj),
        in_specs=[
            pl.BlockSpec((tm, D), lambda i, j: (i, 0)),
            pl.BlockSpec((1, N_ADA, D), lambda i, j: (i // tps, 0, 0)),
            pl.BlockSpec((1, D), lambda i, j: (0, 0)),
            pl.BlockSpec((D, tn), lambda i, j: (0, j)),
        ] + side_specs,
        out_specs=[pl.BlockSpec((tm, tn), lambda i, j: (i, j))] + side_specs,
        scratch_shapes=[pltpu.VMEM((tm, D), BF16), pltpu.VMEM((tm, LANES), F32)],
        compiler_params=_cparams(("arbitrary", "arbitrary")),
        name="inproj",
    )(x2d, mod3, gain, w_in_bf, *side)
    return outs[0], list(outs[1:])


def _ret_kernel(pos_ref, q_ref, k_ref, v_ref, rg_ref, invf_ref, sign_ref, mask_ref,
                qd_ref, kd_ref, cd_ref, o_ref, state_ref, *, k_scale):
    C = RET_CHUNK
    hc = C // 2
    nseq = q_ref.shape[0]

    @pl.when(pl.program_id(1) == 0)
    def _():
        state_ref[...] = jnp.zeros_like(state_ref)

    lo_lanes = lax.broadcasted_iota(jnp.int32, (hc, C), 1) < hc

    def spread(t):
        sw = pltpu.roll(t, hc, 1)
        return jnp.concatenate([jnp.where(lo_lanes, t, sw), jnp.where(lo_lanes, sw, t)], axis=0)

    trig = []
    for b in range(nseq):
        pos2 = jnp.where(lo_lanes, pos_ref[b, 0:hc, :], pos_ref[b, hc:C, :])
        ang = pos2 * invf_ref[...]
        trig.append((spread(jnp.cos(ang)), spread(jnp.sin(ang)) * sign_ref[...]))

    for h in range(RET_HEADS):
        cols = slice(h * C, (h + 1) * C)
        for b in range(nseq):
            cosv, sinv = trig[b]
            q = q_ref[b, :, cols].astype(F32)
            k = k_ref[b, :, cols].astype(F32)
            qr = q * cosv + pltpu.roll(q, hc, 1) * sinv
            kr = (k * cosv + pltpu.roll(k, hc, 1) * sinv) * k_scale
            v = v_ref[b, :, cols]
            s = lax.dot_general(qr.astype(BF16), kr.astype(BF16), (((1,), (1,)), ((), ())),
                                preferred_element_type=F32) * mask_ref[h]
            st = state_ref[b, h]
            lhs = jnp.concatenate([s.astype(BF16), (qr * qd_ref[h]).astype(BF16)], axis=1)
            rhs = jnp.concatenate([v, st.astype(BF16)], axis=0)
            o = jnp.dot(lhs, rhs, preferred_element_type=F32)
            kv = lax.dot_general((kr * kd_ref[h]).astype(BF16), v, (((0,), (0,)), ((), ())),
                                 preferred_element_type=F32)
            state_ref[b, h] = st * cd_ref[h] + kv
            o = o * lax.rsqrt(jnp.mean(o * o, axis=-1, keepdims=True) + EPS)
            g = rg_ref[b, :, cols].astype(F32)
            o_ref[b, :, cols] = (o * (g * jax.nn.sigmoid(g))).astype(o_ref.dtype)


RET_SEQS_PER_STEP = 2


def _retention(pos3, proj3):
    batch, seq, _ = proj3.shape
    H, C = RET_HEADS, RET_CHUNK
    W = H * C
    nseq = RET_SEQS_PER_STEP if batch % RET_SEQS_PER_STEP == 0 else 1
    spb = seq // C
    half = C // 2
    inv_freq = ROPE_BASE ** (-jnp.arange(half, dtype=F32) / half)
    invf2 = jnp.concatenate([inv_freq, inv_freq])[None, :]
    sign = jnp.concatenate([-jnp.ones((half,), F32), jnp.ones((half,), F32)])[None, :]
    log_gamma = jnp.log1p(-jnp.exp2(-5.0 - jnp.arange(H, dtype=F32)))
    idx = jnp.arange(C, dtype=F32)
    diff = idx[:, None] - idx[None, :]
    mask = jnp.where(diff >= 0, jnp.exp(log_gamma[:, None, None] * jnp.maximum(diff, 0.0)), 0.0)
    q_decay = jnp.exp(log_gamma[:, None] * (idx + 1.0))
    k_decay = jnp.exp(log_gamma[:, None] * (C - 1.0 - idx))
    chunk_decay = jnp.exp(log_gamma * C)
    qd = jnp.broadcast_to(q_decay[:, :, None], (H, C, C))
    kd = jnp.broadcast_to(k_decay[:, :, None], (H, C, C))
    cd = jnp.broadcast_to(chunk_decay[:, None, None], (H, 1, C))

    def tok(col):
        return pl.BlockSpec((nseq, C, W), lambda b, n, col=col: (b, n, col))

    return pl.pallas_call(
        functools.partial(_ret_kernel, k_scale=float(C) ** -0.5),
        out_shape=jax.ShapeDtypeStruct((batch, seq, W), BF16),
        grid=(batch // nseq, spb),
        in_specs=[
            pl.BlockSpec((nseq, C, 1), lambda b, n: (b, n, 0)),
            tok(1), tok(2), tok(3), tok(4),
            pl.BlockSpec((1, C), lambda b, n: (0, 0)),
            pl.BlockSpec((1, C), lambda b, n: (0, 0)),
            pl.BlockSpec((H, C, C), lambda b, n: (0, 0, 0)),
            pl.BlockSpec((H, C, C), lambda b, n: (0, 0, 0)),
            pl.BlockSpec((H, C, C), lambda b, n: (0, 0, 0)),
            pl.BlockSpec((H, 1, C), lambda b, n: (0, 0, 0)),
        ],
        out_specs=pl.BlockSpec((nseq, C, W), lambda b, n: (b, n, 0)),
        scratch_shapes=[pltpu.VMEM((nseq, H, C, C), F32)],
        compiler_params=_cparams(("arbitrary", "arbitrary")),
        name="ret",
    )(pos3, proj3, proj3, proj3, proj3, invf2, sign, mask, qd, kd, cd)


def _mix_kernel(a_ref, halo_ref, r_ref, ga0_ref, ga1_ref, gb0_ref, gb1_ref, x_ref, mod_ref,
                wpool_ref, pscale_ref, wbp_ref, wbr_ref, wout_ref, g2_ref,
                wrc_ref, brt_ref,
                x1_ref, h2_ref, lg_ref, ext_ref, lvl_ref, pm_ref, mg_ref, yr_ref, *,
                tiles_per_seq):
    tm = a_ref.shape[0]
    pw = a_ref.shape[1]
    gd = pw // len(POOL_WINDOWS)
    i = pl.program_id(0)
    it = i % tiles_per_seq
    yr_ref[...] = jnp.dot(r_ref[...], wbr_ref[...], preferred_element_type=F32)
    halo = halo_ref[...].astype(F32)
    ext_ref[0:POOL_HALO, :] = jnp.where(it == 0, 0.0, halo)
    ext_ref[POOL_HALO:, :] = a_ref[...].astype(F32)
    tpos = it * tm + lax.broadcasted_iota(jnp.int32, (tm, 1), 0)
    top = POOL_HALO + tm
    for g, w in enumerate(POOL_WINDOWS):
        cols = slice(g * gd, (g + 1) * gd)
        cur = ext_ref[POOL_HALO:top, cols]
        n_levels = w.bit_length() - 1
        prev_ref, prev_cols = ext_ref, cols
        for k in range(1, n_levels + 1):
            lo = POOL_HALO if k == n_levels else SUBLANES * k
            sh = 1 << (k - 1)
            s = prev_ref[lo:top, prev_cols] + prev_ref[lo - sh:top - sh, prev_cols]
            if k < n_levels:
                prev_ref, prev_cols = lvl_ref.at[k % 2], slice(None)
                prev_ref[lo:top, :] = s
        cnt = jnp.minimum(tpos + 1, w).astype(F32)
        pooled = s / cnt - cur
        pm = jnp.dot(pooled.astype(BF16), wpool_ref[g], preferred_element_type=F32)
        pm_ref[:, cols] = (pm * pscale_ref[:, cols]).astype(BF16)

    y_pool = jnp.dot(pm_ref[...], wbp_ref[...], preferred_element_type=F32)
    half = y_pool.shape[1] // 2
    for hh, (ga_ref, gb_ref) in enumerate(((ga0_ref, gb0_ref), (ga1_ref, gb1_ref))):
        cols = slice(hh * half, (hh + 1) * half)
        ga = jax.nn.sigmoid(ga_ref[...].astype(F32))
        gb = jax.nn.sigmoid(gb_ref[...].astype(F32))
        mg_ref[:, cols] = (ga * y_pool[:, cols] + gb * yr_ref[:, cols]).astype(BF16)
    z = jnp.dot(mg_ref[...], wout_ref[...], preferred_element_type=F32)
    gate1 = mod_ref[0, 2:3, :]
    x1 = x_ref[...] + gate1 * z
    x1_ref[...] = x1
    h2 = _norm_mod_rows(x1, g2_ref[...], mod_ref[0, 3:4, :], mod_ref[0, 4:5, :])
    h2_ref[...] = h2
    h_hi = h2.astype(BF16)
    h_lo = (h2 - h_hi.astype(F32)).astype(BF16)
    parts = jnp.dot(jnp.concatenate([h_hi, h_lo], axis=0), wrc_ref[...],
                    preferred_element_type=F32)
    lg = (parts[:tm, :LANES] + parts[tm:, :LANES]) + (parts[:tm, LANES:] + parts[tm:, LANES:])
    lg_ref[...] = lg + brt_ref[...]


def _mix(proj, r, x2d, mod3, wpool_bf, pscale, wbp_bf, wbr_bf, wout_bf, g2, wr_cat, brt,
         seq, tm):
    T, D = x2d.shape
    PW = wbp_bf.shape[0]
    tps = seq // tm
    hb = tm // POOL_HALO

    def tok(col):
        return pl.BlockSpec((tm, PW), lambda i, col=col: (i, col))

    kern = functools.partial(_mix_kernel, tiles_per_seq=tps)
    return pl.pallas_call(
        kern,
        out_shape=(jax.ShapeDtypeStruct((T, D), F32),
                   jax.ShapeDtypeStruct((T, D), F32),
                   jax.ShapeDtypeStruct((T, LANES), F32)),
        grid=(T // tm,),
        in_specs=[
            tok(0),
            pl.BlockSpec((POOL_HALO, PW), lambda i: (jnp.maximum(i * hb - 1, 0), 0)),
            pl.BlockSpec((tm, PW), lambda i: (i, 0)),
            tok(5), tok(6), tok(7), tok(8),
            pl.BlockSpec((tm, D), lambda i: (i, 0)),
            pl.BlockSpec((1, N_ADA, D), lambda i: (i // tps, 0, 0)),
            _const_spec(wpool_bf.shape),
            _const_spec(pscale.shape),
            _const_spec(wbp_bf.shape),
            _const_spec(wbr_bf.shape),
            _const_spec(wout_bf.shape),
            _const_spec(g2.shape),
            _const_spec(wr_cat.shape),
            _const_spec(brt.shape),
        ],
        out_specs=(pl.BlockSpec((tm, D), lambda i: (i, 0)),
                   pl.BlockSpec((tm, D), lambda i: (i, 0)),
                   pl.BlockSpec((tm, LANES), lambda i: (i, 0))),
        scratch_shapes=[pltpu.VMEM((POOL_HALO + tm, PW), F32),
                        pltpu.VMEM((2, POOL_HALO + tm, PW // len(POOL_WINDOWS)), F32),
                        pltpu.VMEM((tm, PW), BF16),
                        pltpu.VMEM((tm, D), BF16),
                        pltpu.VMEM((tm, D), F32)],
        compiler_params=_cparams(("arbitrary",)),
        name="mix",
    )(proj, proj, r, proj, proj, proj, proj, x2d, mod3, wpool_bf, pscale, wbp_bf, wbr_bf,
      wout_bf, g2, wr_cat, brt)


def _route_kernel(lg_ref, tri_ref, col_ref, row_ref, cnt_ref, carry_ref):
    tm = lg_ref.shape[0]

    @pl.when(pl.program_id(0) == 0)
    def _():
        carry_ref[...] = jnp.zeros_like(carry_ref)

    L = lg_ref[...]
    lane = lax.broadcasted_iota(jnp.int32, (tm, LANES), 1).astype(F32)
    neg = -jnp.inf
    none = float(LANES)
    is_g = lane < N_GROUPS
    gl = jnp.where(is_g, L, neg)
    gmax = jnp.max(gl, axis=1, keepdims=True)
    grp = jnp.min(jnp.where(gl == gmax, lane, none), axis=1, keepdims=True)
    gsum = jnp.sum(jnp.where(is_g, jnp.exp(gl - gmax), 0.0), axis=1, keepdims=True)
    p_grp = 1.0 / gsum
    lo = ROUTE_LANE0 + grp * EXPERTS_PER_GROUP
    el = jnp.where((lane >= lo) & (lane < lo + EXPERTS_PER_GROUP), L, neg)
    v1 = jnp.max(el, axis=1, keepdims=True)
    i1 = jnp.min(jnp.where(el == v1, lane, none), axis=1, keepdims=True)
    el2 = jnp.where(lane == i1, neg, el)
    v2 = jnp.max(el2, axis=1, keepdims=True)
    i2 = jnp.min(jnp.where(el2 == v2, lane, none), axis=1, keepdims=True)
    e = jnp.exp(v2 - v1)
    w1 = p_grp / (1.0 + e)
    w2 = p_grp * e / (1.0 + e)
    sel1 = lane == i1
    sel2 = lane == i2
    onehot = jnp.where(sel1 | sel2, 1.0, 0.0)
    cum = jnp.dot(tri_ref[...], onehot.astype(BF16), preferred_element_type=F32)
    cum = cum + carry_ref[0:1, :]
    r1 = jnp.sum(jnp.where(sel1, cum, 0.0), axis=1, keepdims=True)
    r2 = jnp.sum(jnp.where(sel2, cum, 0.0), axis=1, keepdims=True)
    carry_ref[...] = carry_ref[...] + jnp.sum(onehot, axis=0, keepdims=True)
    cnt_ref[...] = carry_ref[...]
    slab = jnp.where(lane == 0, i1 - ROUTE_LANE0, 0.0)
    slab = jnp.where(lane == 1, i2 - ROUTE_LANE0, slab)
    slab = jnp.where(lane == 2, r1, slab)
    slab = jnp.where(lane == 3, r2, slab)
    slab = jnp.where(lane == 4, w1, slab)
    slab = jnp.where(lane == 5, w2, slab)
    col_ref[...] = slab
    row_ref[...] = slab.T[0:8, :]


def _route(logits, tm):
    T = logits.shape[0]
    tri = jnp.tril(jnp.ones((tm, tm), F32), -1).astype(BF16)
    return pl.pallas_call(
        _route_kernel,
        out_shape=(jax.ShapeDtypeStruct((T, LANES), F32),
                   jax.ShapeDtypeStruct((8, T), F32),
                   jax.ShapeDtypeStruct((8, LANES), F32)),
        grid=(T // tm,),
        in_specs=[pl.BlockSpec((tm, LANES), lambda i: (i, 0)),
                  pl.BlockSpec((tm, tm), lambda i: (0, 0))],
        out_specs=(pl.BlockSpec((tm, LANES), lambda i: (i, 0)),
                   pl.BlockSpec((8, tm), lambda i: (0, i)),
                   pl.BlockSpec((8, LANES), lambda i: (0, 0))),
        scratch_shapes=[pltpu.VMEM((8, LANES), F32)],
        compiler_params=_cparams(("arbitrary",)),
        name="route",
    )(logits, tri)


def _dest_kernel(pstart_ref, row_ref, o_ref):
    e = row_ref[0:2, :].astype(jnp.int32)
    r = row_ref[2:4, :].astype(jnp.int32)
    base = jnp.zeros_like(e)
    for k in range(N_EXPERTS):
        base = jnp.where(e == k, pstart_ref[k], base)
    o_ref[...] = jnp.zeros_like(o_ref)
    o_ref[0:2, :] = base + r


def _dest(pstart, row, tn):
    T = row.shape[1]
    grid_spec = pltpu.PrefetchScalarGridSpec(
        num_scalar_prefetch=1,
        grid=(T // tn,),
        in_specs=[pl.BlockSpec((SUBLANES, tn), lambda i, *_: (0, i))],
        out_specs=pl.BlockSpec((SUBLANES, tn), lambda i, *_: (0, i)),
    )
    return pl.pallas_call(
        _dest_kernel,
        out_shape=jax.ShapeDtypeStruct((SUBLANES, T), jnp.int32),
        grid_spec=grid_spec,
        compiler_params=_cparams(("arbitrary",)),
        name="dest",
    )(pstart, row)


def _tile_major(dest2, tm):
    T = dest2.shape[1]
    return dest2.reshape(2, T // tm, tm).transpose(1, 0, 2).reshape(-1)


def _row_ref(ref, row):
    return ref.at[lax.shift_right_logical(row, 3), pl.ds(jnp.bitwise_and(row, SUBLANES - 1), 1)]


def _dispatch_kernel(pstart_ref, pend_ref, nu_ref, dest_ref, h2_ref, buf_ref, zero_ref, sem):
    ng = h2_ref.shape[0]
    tm = ng * SUBLANES
    blk = zero_ref.shape[0]
    nb = buf_ref.shape[0] // blk

    @pl.when(pl.program_id(0) == 0)
    def _():
        zero_ref[...] = jnp.zeros_like(zero_ref)

        def zero_copy(start):
            return pltpu.make_async_copy(zero_ref, buf_ref.at[pl.ds(start, blk)], sem)

        def zbody(e, n_started):
            used = pend_ref[e] > pstart_ref[e]

            @pl.when(used)
            def _():
                zero_copy(lax.shift_right_logical(pend_ref[e], 3) - blk).start()
            return n_started + used.astype(jnp.int32)

        n_tails = lax.fori_loop(0, N_EXPERTS, zbody, 0)

        def tail(j, carry):
            zero_copy(j * blk).start()
            return carry

        lax.fori_loop(nu_ref[0], nb, tail, 0)

        def drain(i, carry):
            zero_copy(0).wait()
            return carry

        lax.fori_loop(0, n_tails + (nb - nu_ref[0]), drain, 0)

    def issue(g, carry):
        for u in range(SUBLANES):
            for k in range(2):
                dest = dest_ref[k * tm + g * SUBLANES + u]
                pltpu.make_async_copy(h2_ref.at[g, pl.ds(u, 1)], _row_ref(buf_ref, dest),
                                      sem).start(priority=k)
        return carry

    lax.fori_loop(0, ng, issue, 0, unroll=ROW_DMA_UNROLL)
    for k in range(2):
        pltpu.make_async_copy(h2_ref, buf_ref.at[pl.ds(0, ng)], sem).wait()


def _dispatch(pstart, pend, n_used, dest_flat, h2, n_rows, tm):
    G, _, D = h2.shape
    ng = tm // SUBLANES
    grid_spec = pltpu.PrefetchScalarGridSpec(
        num_scalar_prefetch=3,
        grid=(G // ng,),
        in_specs=[pl.BlockSpec((2 * tm,), lambda i, *_: (i,), memory_space=pltpu.SMEM),
                  pl.BlockSpec((ng, SUBLANES, D), lambda i, *_: (i, 0, 0))],
        out_specs=pl.BlockSpec(memory_space=pl.ANY),
        scratch_shapes=[pltpu.VMEM((EXPERT_ROWS // SUBLANES, SUBLANES, D), h2.dtype),
                        pltpu.SemaphoreType.DMA(())],
    )
    return pl.pallas_call(
        _dispatch_kernel,
        out_shape=jax.ShapeDtypeStruct((n_rows // SUBLANES, SUBLANES, D), h2.dtype),
        grid_spec=grid_spec,
        compiler_params=_cparams(("arbitrary",)),
        name="dispatch",
    )(pstart, pend, n_used, dest_flat, h2)


def _expert_kernel(be_ref, nu_ref, x_ref, w1_ref, w3_ref, w2_ref, o_ref):
    @pl.when(pl.program_id(0) < nu_ref[0])
    def _():
        x = x_ref[...].astype(BF16)
        a = jnp.dot(x, w1_ref[...], preferred_element_type=F32)
        b = jnp.dot(x, w3_ref[...], preferred_element_type=F32)
        hid = (a * jax.nn.sigmoid(a)) * b
        o_ref[...] = jnp.dot(hid.astype(BF16), w2_ref[...], preferred_element_type=F32)

    @pl.when(pl.program_id(0) >= nu_ref[0])
    def _():
        o_ref[...] = jnp.zeros_like(o_ref)


def _experts(block_e, n_used, buf, w1_bf, w3_bf, w2_bf):
    P, D = buf.shape
    F = w1_bf.shape[2]
    nb = P // EXPERT_ROWS

    def row_map(j, be, nu):
        return (jnp.minimum(j, nu[0] - 1), 0)

    grid_spec = pltpu.PrefetchScalarGridSpec(
        num_scalar_prefetch=2,
        grid=(nb,),
        in_specs=[pl.BlockSpec((EXPERT_ROWS, D), row_map),
                  pl.BlockSpec((None, D, F), lambda j, be, nu: (be[j], 0, 0)),
                  pl.BlockSpec((None, D, F), lambda j, be, nu: (be[j], 0, 0)),
                  pl.BlockSpec((None, F, D), lambda j, be, nu: (be[j], 0, 0))],
        out_specs=pl.BlockSpec((EXPERT_ROWS, D), lambda j, be, nu: (j, 0)),
    )
    return pl.pallas_call(
        _expert_kernel,
        out_shape=jax.ShapeDtypeStruct((P, D), F32),
        grid_spec=grid_spec,
        compiler_params=_cparams(("arbitrary",)),
        name="expert",
    )(block_e, n_used, buf, w1_bf, w3_bf, w2_bf)


def _final_kernel(dcur_ref, dnext_ref, col_ref, x1_ref, mod_ref, fg_ref, yb_ref, o_ref,
                  g_ref, sems):
    ng = x1_ref.shape[0]
    tm = ng * SUBLANES
    i = pl.program_id(0)
    buf = lax.rem(i, 2)

    def issue_tile(d_ref, b):
        def issue(g, carry):
            for u in range(SUBLANES):
                for k in range(2):
                    dest = d_ref[k * tm + g * SUBLANES + u]
                    pltpu.make_async_copy(_row_ref(yb_ref, dest), g_ref.at[b, k, g, pl.ds(u, 1)],
                                          sems.at[b]).start(priority=k)
            return carry

        lax.fori_loop(0, ng, issue, 0, unroll=ROW_DMA_UNROLL)

    @pl.when(i == 0)
    def _():
        issue_tile(dcur_ref, 0)

    @pl.when(i + 1 < pl.num_programs(0))
    def _():
        issue_tile(dnext_ref, 1 - buf)

    for k in range(2):
        pltpu.make_async_copy(yb_ref.at[pl.ds(0, ng)], g_ref.at[buf, k], sems.at[buf]).wait()

    w1 = col_ref[:, :, 4:5]
    w2 = col_ref[:, :, 5:6]
    y = g_ref[buf, 0] * w1 + g_ref[buf, 1] * w2
    x2 = x1_ref[...] + mod_ref[0, 5:6, :] * y
    ms = jnp.mean(x2 * x2, axis=-1, keepdims=True)
    o_ref[...] = (x2 * lax.rsqrt(ms + EPS)) * fg_ref[...]


def _final(dest_flat, col, x1, mod3, fgain, yb, seq, tm):
    G, _, D = x1.shape
    ng = tm // SUBLANES
    n_tiles = G // ng
    tps = seq // tm
    return pl.pallas_call(
        _final_kernel,
        out_shape=jax.ShapeDtypeStruct((G, SUBLANES, D), F32),
        grid=(n_tiles,),
        in_specs=[pl.BlockSpec((2 * tm,), lambda i: (i,), memory_space=pltpu.SMEM),
                  pl.BlockSpec((2 * tm,), lambda i: (jnp.minimum(i + 1, n_tiles - 1),),
                               memory_space=pltpu.SMEM),
                  pl.BlockSpec((ng, SUBLANES, LANES), lambda i: (i, 0, 0)),
                  pl.BlockSpec((ng, SUBLANES, D), lambda i: (i, 0, 0)),
                  pl.BlockSpec((1, N_ADA, D), lambda i: (i // tps, 0, 0)),
                  pl.BlockSpec((1, D), lambda i: (0, 0)),
                  pl.BlockSpec(memory_space=pl.ANY)],
        out_specs=pl.BlockSpec((ng, SUBLANES, D), lambda i: (i, 0, 0)),
        scratch_shapes=[pltpu.VMEM((2, 2, ng, SUBLANES, D), F32),
                        pltpu.SemaphoreType.DMA((2,))],
        compiler_params=_cparams(("arbitrary",)),
        name="final",
    )(dest_flat, dest_flat, col, x1, mod3, fgain, yb)


def _tile(n, pref):
    t = min(n, pref)
    assert n % t == 0, (n, t)
    return t


def kernel(x, c, positions, w_ada, b_ada, norm1_gain, w_in, w_pool, pool_scale, w_branch_pool,
           w_branch_ret, w_out, norm2_gain, w_group, b_group, w_router, b_router, w1, w3, w2,
           final_gain):
    B, S, D = x.shape
    T = B * S
    assert w_ada.shape[0] == 1, "only DEPTH == 1 is supported"
    x2d = x.reshape(T, D)
    posf = positions.astype(F32).reshape(T, 1)
    for l in range(1):
        mod3 = _ada(c, w_ada[l], b_ada[l][None, :]).reshape(B, N_ADA, D)

        n_exp, _, d_exp = w1[l].shape
        side = [w1[l].reshape(n_exp * D, d_exp), w3[l].reshape(n_exp * D, d_exp),
                w2[l].reshape(n_exp * d_exp, D)]
        proj, (w1_bf, w3_bf, w2_bf) = _inproj(x2d, mod3, norm1_gain[l][None, :],
                                              w_in[l].astype(BF16), side, S, _tile(S, 1024), 1024)
        w1_bf = w1_bf.reshape(n_exp, D, d_exp)
        w3_bf = w3_bf.reshape(n_exp, D, d_exp)
        w2_bf = w2_bf.reshape(n_exp, d_exp, D)
        r = _retention(posf.reshape(B, S, 1), proj.reshape(B, S, -1)).reshape(T, -1)

        w_rt = jnp.concatenate(
            [w_group[l], w_router[l],
             jnp.zeros((D, LANES - N_GROUPS - N_EXPERTS), F32)], axis=1)
        wr_hi = w_rt.astype(BF16)
        wr_lo = (w_rt - wr_hi.astype(F32)).astype(BF16)
        wr_cat = jnp.concatenate([wr_hi, wr_lo], axis=1)
        brt = jnp.concatenate(
            [b_group[l], b_router[l], jnp.zeros((LANES - N_GROUPS - N_EXPERTS,), F32)])[None, :]
        x1, h2, logits = _mix(
            proj, r, x2d, mod3, w_pool[l].astype(BF16), pool_scale[l][None, :],
            w_branch_pool[l].astype(BF16), w_branch_ret[l].astype(BF16), w_out[l].astype(BF16),
            norm2_gain[l][None, :], wr_cat, brt, S, _tile(S, 256))

        col, row, cnt = _route(logits, _tile(T, 512))
        counts = cnt[0, ROUTE_LANE0:ROUTE_LANE0 + N_EXPERTS].astype(jnp.int32)
        padded = (counts + EXPERT_ROWS - 1) // EXPERT_ROWS * EXPERT_ROWS
        pend = jnp.cumsum(padded)
        pstart = pend - padded
        n_rows = 2 * T + N_EXPERTS * EXPERT_ROWS
        nb = n_rows // EXPERT_ROWS
        n_used = (pend[-1:] // EXPERT_ROWS).astype(jnp.int32)
        block_row0 = jnp.arange(nb, dtype=jnp.int32) * EXPERT_ROWS
        block_e = jnp.minimum(
            jnp.sum((pend[None, :] <= block_row0[:, None]).astype(jnp.int32), axis=1),
            N_EXPERTS - 1)

        dest2 = _dest(pstart, row, _tile(T, 4096))[:2]
        tm_d = _tile(T, 1024)
        tm_f = _tile(S, 512)
        buf = _dispatch(pstart, pend, n_used, _tile_major(dest2, tm_d),
                        h2.reshape(T // SUBLANES, SUBLANES, D), n_rows, tm_d)
        yb = _experts(block_e, n_used, buf.reshape(n_rows, D), w1_bf, w3_bf, w2_bf)
        out = _final(_tile_major(dest2, tm_f), col.reshape(T // SUBLANES, SUBLANES, LANES),
                     x1.reshape(T // SUBLANES, SUBLANES, D), mod3, final_gain[None, :],
                     yb.reshape(n_rows // SUBLANES, SUBLANES, D), S, tm_f)
    return out.reshape(B, S, D)
```

```python
import functools

import jax
import jax.numpy as jnp
from jax import lax
from jax.experimental import pallas as pl
from jax.experimental.pallas import tpu as pltpu

F32 = jnp.float32
BF16 = jnp.bfloat16

EPS = 1e-6
ROPE_BASE = 10000.0
POOL_WINDOWS = (2, 4, 8, 16)
POOL_HALO = 32
RET_HEADS = 8
RET_CHUNK = 128
N_GROUPS = 4
EXPERTS_PER_GROUP = 8
N_EXPERTS = N_GROUPS * EXPERTS_PER_GROUP
N_ADA = 6
LANES = 128
ROUTE_LANE0 = N_GROUPS
EXPERT_ROWS = 256
INPROJ_TN = 1536
SUBLANES = 8
ROW_DMA_UNROLL = 2
VMEM_LIMIT = 56 * 1024 * 1024


def _cparams(sem):
    return pltpu.CompilerParams(dimension_semantics=sem, vmem_limit_bytes=VMEM_LIMIT)


def _const_spec(shape):
    n = len(shape)
    return pl.BlockSpec(shape, lambda *_: (0,) * n, pipeline_mode=pl.Buffered(1))


def _ada_kernel(cb_ref, w_ref, b_ref, o_ref, cact_ref):
    cb = cb_ref[...]
    cact_ref[...] = cb * jax.nn.sigmoid(cb)
    tn = w_ref.shape[1]
    for b in range(cb_ref.shape[0]):
        for j in range(tn // LANES):
            sl = slice(j * LANES, (j + 1) * LANES)
            prod = w_ref[:, sl] * cact_ref[b]
            o_ref[b:b + 1, sl] = jnp.sum(prod, axis=0, keepdims=True) + b_ref[:, sl]


def _ada(c, w_ada, b_ada):
    B, D = c.shape
    N = w_ada.shape[1]
    tn = 1024
    cb = jnp.broadcast_to(c[:, :, None], (B, D, LANES))
    return pl.pallas_call(
        _ada_kernel,
        out_shape=jax.ShapeDtypeStruct((B, N), F32),
        grid=(N // tn,),
        in_specs=[
            pl.BlockSpec((B, D, LANES), lambda j: (0, 0, 0)),
            pl.BlockSpec((D, tn), lambda j: (0, j)),
            pl.BlockSpec((1, tn), lambda j: (0, j)),
        ],
        out_specs=pl.BlockSpec((B, tn), lambda j: (0, j)),
        scratch_shapes=[pltpu.VMEM((B, D, LANES), F32)],
        compiler_params=_cparams(("arbitrary",)),
        name="ada",
    )(cb, w_ada, b_ada)


def _norm_mod_rows(x, gain, shift, scale):
    ms = jnp.mean(x * x, axis=-1, keepdims=True)
    y = (x * lax.rsqrt(ms + EPS)) * gain
    return y * (1.0 + scale) + shift


SIDE_CAST_BLOCK_BYTES = 2 * 1024 * 1024


def _side_cast_rows(n_rows, n_cols, n_steps):
    rows = SIDE_CAST_BLOCK_BYTES // (4 * n_cols)
    if n_rows % rows == 0 and n_rows // rows <= n_steps:
        return rows
    return None


def _side_specs(side, n_steps, step_of):
    rows = [_side_cast_rows(a.shape[0], a.shape[1], n_steps) for a in side]
    if any(r is None for r in rows):
        return None
    specs = []
    for a, r in zip(side, rows):
        nb = a.shape[0] // r
        specs.append(pl.BlockSpec(
            (r, a.shape[1]), lambda *g, nb=nb: (jnp.minimum(step_of(*g), nb - 1), 0)))
    return specs


def _cast_side(side_in, side_out):
    for src_ref, dst_ref in zip(side_in, side_out):
        dst_ref[...] = src_ref[...].astype(BF16)


def _inproj_kernel(x_ref, mod_ref, g_ref, w_ref, *rest, n_side):
    side_in = rest[:n_side]
    o_ref = rest[n_side]
    side_out = rest[n_side + 1:2 * n_side + 1]
    h_ref, r_ref = rest[2 * n_side + 1:]
    tm, d = x_ref.shape
    rc = 32

    @pl.when(pl.program_id(1) == 0)
    def _():
        shift = mod_ref[0, 0:1, :]
        mult = g_ref[...] * (1.0 + mod_ref[0, 1:2, :])

        def stats(c, carry):
            rows = pl.ds(pl.multiple_of(c * rc, rc), rc)
            x = x_ref[rows, :]
            r = lax.rsqrt(jnp.mean(x * x, axis=-1, keepdims=True) + EPS)
            r_ref[rows, :] = jnp.broadcast_to(r, (rc, LANES))
            return carry

        lax.fori_loop(0, tm // rc, stats, 0, unroll=4)

        def apply(c, carry):
            rows = pl.ds(pl.multiple_of(c * rc, rc), rc)
            r = r_ref[rows, :]
            for j in range(d // LANES):
                cols = slice(j * LANES, (j + 1) * LANES)
                h_ref[rows, cols] = ((x_ref[rows, cols] * r) * mult[:, cols]
                                     + shift[:, cols]).astype(BF16)
            return carry

        lax.fori_loop(0, tm // rc, apply, 0)

    _cast_side(side_in, side_out)
    o_ref[...] = jnp.dot(h_ref[...], w_ref[...], preferred_element_type=F32).astype(o_ref.dtype)


def _inproj(x2d, mod3, gain, w_in_bf, side, seq, tm, tn):
    T, D = x2d.shape
    N = w_in_bf.shape[1]
    tps = seq // tm
    nj = N // tn
    side_specs = _side_specs(side, (T // tm) * nj, lambda i, j: i * nj + j)
    if side_specs is None:
        out = _inproj(x2d, mod3, gain, w_in_bf, [], seq, tm, tn)
        return out[0], [a.astype(BF16) for a in side]
    outs = pl.pallas_call(
        functools.partial(_inproj_kernel, n_side=len(side)),
        out_shape=[jax.ShapeDtypeStruct((T, N), BF16)]
        + [jax.ShapeDtypeStruct(a.shape, BF16) for a in side],
        grid=(T // tm, nj),
        in_specs=[
            pl.BlockSpec((tm, D), lambda i, j: (i, 0)),
            pl.BlockSpec((1, N_ADA, D), lambda i, j: (i // tps, 0, 0)),
            pl.BlockSpec((1, D), lambda i, j: (0, 0)),
            pl.BlockSpec((D, tn), lambda i, j: (0, j)),
        ] + side_specs,
        out_specs=[pl.BlockSpec((tm, tn), lambda i, j: (i, j))] + side_specs,
        scratch_shapes=[pltpu.VMEM((tm, D), BF16), pltpu.VMEM((tm, LANES), F32)],
        compiler_params=_cparams(("arbitrary", "arbitrary")),
        name="inproj",
    )(x2d, mod3, gain, w_in_bf, *side)
    return outs[0], list(outs[1:])


def _ret_kernel(pos_ref, q_ref, k_ref, v_ref, rg_ref, invf_ref, sign_ref, mask_ref,
                qd_ref, kd_ref, cd_ref, *rest, k_scale, n_side):
    side_in = rest[:n_side]
    o_ref = rest[n_side]
    side_out = rest[n_side + 1:2 * n_side + 1]
    state_ref = rest[2 * n_side + 1]
    C = RET_CHUNK
    hc = C // 2
    nseq = q_ref.shape[0]

    @pl.when(pl.program_id(1) == 0)
    def _():
        state_ref[...] = jnp.zeros_like(state_ref)

    _cast_side(side_in, side_out)

    lo_lanes = lax.broadcasted_iota(jnp.int32, (hc, C), 1) < hc

    def spread(t):
        sw = pltpu.roll(t, hc, 1)
        return jnp.concatenate([jnp.where(lo_lanes, t, sw), jnp.where(lo_lanes, sw, t)], axis=0)

    trig = []
    for b in range(nseq):
        pos2 = jnp.where(lo_lanes, pos_ref[b, 0:hc, :], pos_ref[b, hc:C, :])
        ang = pos2 * invf_ref[...]
        trig.append((spread(jnp.cos(ang)), spread(jnp.sin(ang)) * sign_ref[...]))

    for h in range(RET_HEADS):
        cols = slice(h * C, (h + 1) * C)
        for b in range(nseq):
            cosv, sinv = trig[b]
            q = q_ref[b, :, cols].astype(F32)
            k = k_ref[b, :, cols].astype(F32)
            qr = q * cosv + pltpu.roll(q, hc, 1) * sinv
            kr = (k * cosv + pltpu.roll(k, hc, 1) * sinv) * k_scale
            v = v_ref[b, :, cols]
            s = lax.dot_general(qr.astype(BF16), kr.astype(BF16), (((1,), (1,)), ((), ())),
                                preferred_element_type=F32) * mask_ref[h]
            st = state_ref[b, h]
            lhs = jnp.concatenate([s.astype(BF16), (qr * qd_ref[h]).astype(BF16)], axis=1)
            rhs = jnp.concatenate([v, st.astype(BF16)], axis=0)
            o = jnp.dot(lhs, rhs, preferred_element_type=F32)
            kv = lax.dot_general((kr * kd_ref[h]).astype(BF16), v, (((0,), (0,)), ((), ())),
                                 preferred_element_type=F32)
            state_ref[b, h] = st * cd_ref[h] + kv
            o = o * lax.rsqrt(jnp.mean(o * o, axis=-1, keepdims=True) + EPS)
            g = rg_ref[b, :, cols].astype(F32)
            o_ref[b, :, cols] = (o * (g * jax.nn.sigmoid(g))).astype(o_ref.dtype)


RET_SEQS_PER_STEP = 2


def _retention(pos3, proj3, side):
    batch, seq, _ = proj3.shape
    H, C = RET_HEADS, RET_CHUNK
    W = H * C
    nseq = RET_SEQS_PER_STEP if batch % RET_SEQS_PER_STEP == 0 else 1
    spb = seq // C
    half = C // 2
    inv_freq = ROPE_BASE ** (-jnp.arange(half, dtype=F32) / half)
    invf2 = jnp.concatenate([inv_freq, inv_freq])[None, :]
    sign = jnp.concatenate([-jnp.ones((half,), F32), jnp.ones((half,), F32)])[None, :]
    log_gamma = jnp.log1p(-jnp.exp2(-5.0 - jnp.arange(H, dtype=F32)))
    idx = jnp.arange(C, dtype=F32)
    diff = idx[:, None] - idx[None, :]
    mask = jnp.where(diff >= 0, jnp.exp(log_gamma[:, None, None] * jnp.maximum(diff, 0.0)), 0.0)
    q_decay = jnp.exp(log_gamma[:, None] * (idx + 1.0))
    k_decay = jnp.exp(log_gamma[:, None] * (C - 1.0 - idx))
    chunk_decay = jnp.exp(log_gamma * C)
    qd = jnp.broadcast_to(q_decay[:, :, None], (H, C, C))
    kd = jnp.broadcast_to(k_decay[:, :, None], (H, C, C))
    cd = jnp.broadcast_to(chunk_decay[:, None, None], (H, 1, C))

    def tok(col):
        return pl.BlockSpec((nseq, C, W), lambda b, n, col=col: (b, n, col))

    side_specs = _side_specs(side, (batch // nseq) * spb, lambda b, n: b * spb + n)
    if side_specs is None:
        out = _retention(pos3, proj3, [])
        return out[0], [a.astype(BF16) for a in side]
    outs = pl.pallas_call(
        functools.partial(_ret_kernel, k_scale=float(C) ** -0.5, n_side=len(side)),
        out_shape=[jax.ShapeDtypeStruct((batch, seq, W), BF16)]
        + [jax.ShapeDtypeStruct(a.shape, BF16) for a in side],
        grid=(batch // nseq, spb),
        in_specs=[
            pl.BlockSpec((nseq, C, 1), lambda b, n: (b, n, 0)),
            tok(1), tok(2), tok(3), tok(4),
            pl.BlockSpec((1, C), lambda b, n: (0, 0)),
            pl.BlockSpec((1, C), lambda b, n: (0, 0)),
            pl.BlockSpec((H, C, C), lambda b, n: (0, 0, 0)),
            pl.BlockSpec((H, C, C), lambda b, n: (0, 0, 0)),
            pl.BlockSpec((H, C, C), lambda b, n: (0, 0, 0)),
            pl.BlockSpec((H, 1, C), lambda b, n: (0, 0, 0)),
        ] + side_specs,
        out_specs=[pl.BlockSpec((nseq, C, W), lambda b, n: (b, n, 0))] + side_specs,
        scratch_shapes=[pltpu.VMEM((nseq, H, C, C), F32)],
        compiler_params=_cparams(("arbitrary", "arbitrary")),
        name="ret",
    )(pos3, proj3, proj3, proj3, proj3, invf2, sign, mask, qd, kd, cd, *side)
    return outs[0], list(outs[1:])


def _mix_kernel(a_ref, halo_ref, r_ref, ga0_ref, ga1_ref, gb0_ref, gb1_ref, x_ref, mod_ref,
                wpool_ref, pscale_ref, wbp_ref, wbr_ref, wout_ref, g2_ref,
                wrc_ref, brt_ref, *rest, tiles_per_seq, n_side):
    side_in = rest[:n_side]
    x1_ref, h2_ref, lg_ref = rest[n_side:n_side + 3]
    side_out = rest[n_side + 3:2 * n_side + 3]
    ext_ref, lvl_ref, pm_ref, mg_ref, yr_ref = rest[2 * n_side + 3:]
    _cast_side(side_in, side_out)
    tm = a_ref.shape[0]
    pw = a_ref.shape[1]
    gd = pw // len(POOL_WINDOWS)
    i = pl.program_id(0)
    it = i % tiles_per_seq
    yr_ref[...] = jnp.dot(r_ref[...], wbr_ref[...], preferred_element_type=F32)
    halo = halo_ref[...].astype(F32)
    ext_ref[0:POOL_HALO, :] = jnp.where(it == 0, 0.0, halo)
    ext_ref[POOL_HALO:, :] = a_ref[...].astype(F32)
    tpos = it * tm + lax.broadcasted_iota(jnp.int32, (tm, 1), 0)
    top = POOL_HALO + tm
    for g, w in enumerate(POOL_WINDOWS):
        cols = slice(g * gd, (g + 1) * gd)
        cur = ext_ref[POOL_HALO:top, cols]
        n_levels = w.bit_length() - 1
        prev_ref, prev_cols = ext_ref, cols
        for k in range(1, n_levels + 1):
            lo = POOL_HALO if k == n_levels else SUBLANES * k
            sh = 1 << (k - 1)
            s = prev_ref[lo:top, prev_cols] + prev_ref[lo - sh:top - sh, prev_cols]
            if k < n_levels:
                prev_ref, prev_cols = lvl_ref.at[k % 2], slice(None)
                prev_ref[lo:top, :] = s
        cnt = jnp.minimum(tpos + 1, w).astype(F32)
        pooled = s / cnt - cur
        pm = jnp.dot(pooled.astype(BF16), wpool_ref[g], preferred_element_type=F32)
        pm_ref[:, cols] = (pm * pscale_ref[:, cols]).astype(BF16)

    y_pool = jnp.dot(pm_ref[...], wbp_ref[...], preferred_element_type=F32)
    half = y_pool.shape[1] // 2
    for hh, (ga_ref, gb_ref) in enumerate(((ga0_ref, gb0_ref), (ga1_ref, gb1_ref))):
        cols = slice(hh * half, (hh + 1) * half)
        ga = jax.nn.sigmoid(ga_ref[...].astype(F32))
        gb = jax.nn.sigmoid(gb_ref[...].astype(F32))
        mg_ref[:, cols] = (ga * y_pool[:, cols] + gb * yr_ref[:, cols]).astype(BF16)
    z = jnp.dot(mg_ref[...], wout_ref[...], preferred_element_type=F32)
    gate1 = mod_ref[0, 2:3, :]
    x1 = x_ref[...] + gate1 * z
    x1_ref[...] = x1
    h2 = _norm_mod_rows(x1, g2_ref[...], mod_ref[0, 3:4, :], mod_ref[0, 4:5, :])
    h2_ref[...] = h2
    h_hi = h2.astype(BF16)
    h_lo = (h2 - h_hi.astype(F32)).astype(BF16)
    parts = jnp.dot(jnp.concatenate([h_hi, h_lo], axis=0), wrc_ref[...],
                    preferred_element_type=F32)
    lg = (parts[:tm, :LANES] + parts[tm:, :LANES]) + (parts[:tm, LANES:] + parts[tm:, LANES:])
    lg_ref[...] = lg + brt_ref[...]


def _mix(proj, r, x2d, mod3, wpool_bf, pscale, wbp_bf, wbr_bf, wout_bf, g2, wr_cat, brt,
         side, seq, tm):
    T, D = x2d.shape
    PW = wbp_bf.shape[0]
    tps = seq // tm
    hb = tm // POOL_HALO

    def tok(col):
        return pl.BlockSpec((tm, PW), lambda i, col=col: (i, col))

    side_specs = _side_specs(side, T // tm, lambda i: i)
    if side_specs is None:
        outs = _mix(proj, r, x2d, mod3, wpool_bf, pscale, wbp_bf, wbr_bf, wout_bf, g2, wr_cat, brt,
                    [], seq, tm)
        return outs[:3] + [a.astype(BF16) for a in side]
    kern = functools.partial(_mix_kernel, tiles_per_seq=tps, n_side=len(side))
    outs = pl.pallas_call(
        kern,
        out_shape=[jax.ShapeDtypeStruct((T, D), F32),
                   jax.ShapeDtypeStruct((T, D), F32),
                   jax.ShapeDtypeStruct((T, LANES), F32)]
        + [jax.ShapeDtypeStruct(a.shape, BF16) for a in side],
        grid=(T // tm,),
        in_specs=[
            tok(0),
            pl.BlockSpec((POOL_HALO, PW), lambda i: (jnp.maximum(i * hb - 1, 0), 0)),
            pl.BlockSpec((tm, PW), lambda i: (i, 0)),
            tok(5), tok(6), tok(7), tok(8),
            pl.BlockSpec((tm, D), lambda i: (i, 0)),
            pl.BlockSpec((1, N_ADA, D), lambda i: (i // tps, 0, 0)),
            _const_spec(wpool_bf.shape),
            _const_spec(pscale.shape),
            _const_spec(wbp_bf.shape),
            _const_spec(wbr_bf.shape),
            _const_spec(wout_bf.shape),
            _const_spec(g2.shape),
            _const_spec(wr_cat.shape),
            _const_spec(brt.shape),
        ] + side_specs,
        out_specs=[pl.BlockSpec((tm, D), lambda i: (i, 0)),
                   pl.BlockSpec((tm, D), lambda i: (i, 0)),
                   pl.BlockSpec((tm, LANES), lambda i: (i, 0))] + side_specs,
        scratch_shapes=[pltpu.VMEM((POOL_HALO + tm, PW), F32),
                        pltpu.VMEM((2, POOL_HALO + tm, PW // len(POOL_WINDOWS)), F32),
                        pltpu.VMEM((tm, PW), BF16),
                        pltpu.VMEM((tm, D), BF16),
                        pltpu.VMEM((tm, D), F32)],
        compiler_params=_cparams(("arbitrary",)),
        name="mix",
    )(proj, proj, r, proj, proj, proj, proj, x2d, mod3, wpool_bf, pscale, wbp_bf, wbr_bf,
      wout_bf, g2, wr_cat, brt, *side)
    return list(outs)


def _route_kernel(lg_ref, tri_ref, col_ref, row_ref, cnt_ref, carry_ref):
    tm = lg_ref.shape[0]

    @pl.when(pl.program_id(0) == 0)
    def _():
        carry_ref[...] = jnp.zeros_like(carry_ref)

    L = lg_ref[...]
    lane = lax.broadcasted_iota(jnp.int32, (tm, LANES), 1).astype(F32)
    neg = -jnp.inf
    none = float(LANES)
    is_g = lane < N_GROUPS
    gl = jnp.where(is_g, L, neg)
    gmax = jnp.max(gl, axis=1, keepdims=True)
    grp = jnp.min(jnp.where(gl == gmax, lane, none), axis=1, keepdims=True)
    gsum = jnp.sum(jnp.where(is_g, jnp.exp(gl - gmax), 0.0), axis=1, keepdims=True)
    p_grp = 1.0 / gsum
    lo = ROUTE_LANE0 + grp * EXPERTS_PER_GROUP
    el = jnp.where((lane >= lo) & (lane < lo + EXPERTS_PER_GROUP), L, neg)
    v1 = jnp.max(el, axis=1, keepdims=True)
    i1 = jnp.min(jnp.where(el == v1, lane, none), axis=1, keepdims=True)
    el2 = jnp.where(lane == i1, neg, el)
    v2 = jnp.max(el2, axis=1, keepdims=True)
    i2 = jnp.min(jnp.where(el2 == v2, lane, none), axis=1, keepdims=True)
    e = jnp.exp(v2 - v1)
    w1 = p_grp / (1.0 + e)
    w2 = p_grp * e / (1.0 + e)
    sel1 = lane == i1
    sel2 = lane == i2
    onehot = jnp.where(sel1 | sel2, 1.0, 0.0)
    cum = jnp.dot(tri_ref[...], onehot.astype(BF16), preferred_element_type=F32)
    cum = cum + carry_ref[0:1, :]
    r1 = jnp.sum(jnp.where(sel1, cum, 0.0), axis=1, keepdims=True)
    r2 = jnp.sum(jnp.where(sel2, cum, 0.0), axis=1, keepdims=True)
    carry_ref[...] = carry_ref[...] + jnp.sum(onehot, axis=0, keepdims=True)
    cnt_ref[...] = carry_ref[...]
    slab = jnp.where(lane == 0, i1 - ROUTE_LANE0, 0.0)
    slab = jnp.where(lane == 1, i2 - ROUTE_LANE0, slab)
    slab = jnp.where(lane == 2, r1, slab)
    slab = jnp.where(lane == 3, r2, slab)
    slab = jnp.where(lane == 4, w1, slab)
    slab = jnp.where(lane == 5, w2, slab)
    col_ref[...] = slab
    row_ref[...] = slab.T[0:8, :]


def _route(logits, tm):
    T = logits.shape[0]
    tri = jnp.tril(jnp.ones((tm, tm), F32), -1).astype(BF16)
    return pl.pallas_call(
        _route_kernel,
        out_shape=(jax.ShapeDtypeStruct((T, LANES), F32),
                   jax.ShapeDtypeStruct((8, T), F32),
                   jax.ShapeDtypeStruct((8, LANES), F32)),
        grid=(T // tm,),
        in_specs=[pl.BlockSpec((tm, LANES), lambda i: (i, 0)),
                  pl.BlockSpec((tm, tm), lambda i: (0, 0))],
        out_specs=(pl.BlockSpec((tm, LANES), lambda i: (i, 0)),
                   pl.BlockSpec((8, tm), lambda i: (0, i)),
                   pl.BlockSpec((8, LANES), lambda i: (0, 0))),
        scratch_shapes=[pltpu.VMEM((8, LANES), F32)],
        compiler_params=_cparams(("arbitrary",)),
        name="route",
    )(logits, tri)


def _dest_kernel(pstart_ref, row_ref, o_ref):
    e = row_ref[0:2, :].astype(jnp.int32)
    r = row_ref[2:4, :].astype(jnp.int32)
    base = jnp.zeros_like(e)
    for k in range(N_EXPERTS):
        base = jnp.where(e == k, pstart_ref[k], base)
    o_ref[...] = jnp.zeros_like(o_ref)
    o_ref[0:2, :] = base + r


def _dest(pstart, row, tn):
    T = row.shape[1]
    grid_spec = pltpu.PrefetchScalarGridSpec(
        num_scalar_prefetch=1,
        grid=(T // tn,),
        in_specs=[pl.BlockSpec((SUBLANES, tn), lambda i, *_: (0, i))],
        out_specs=pl.BlockSpec((SUBLANES, tn), lambda i, *_: (0, i)),
    )
    return pl.pallas_call(
        _dest_kernel,
        out_shape=jax.ShapeDtypeStruct((SUBLANES, T), jnp.int32),
        grid_spec=grid_spec,
        compiler_params=_cparams(("arbitrary",)),
        name="dest",
    )(pstart, row)


def _tile_major(dest2, tm):
    T = dest2.shape[1]
    return dest2.reshape(2, T // tm, tm).transpose(1, 0, 2).reshape(-1)


def _row_ref(ref, row):
    return ref.at[lax.shift_right_logical(row, 3), pl.ds(jnp.bitwise_and(row, SUBLANES - 1), 1)]


def _dispatch_kernel(pstart_ref, pend_ref, nu_ref, dest_ref, h2_ref, buf_ref, zero_ref, sem):
    ng = h2_ref.shape[0]
    tm = ng * SUBLANES
    blk = zero_ref.shape[0]
    nb = buf_ref.shape[0] // blk

    @pl.when(pl.program_id(0) == 0)
    def _():
        zero_ref[...] = jnp.zeros_like(zero_ref)

        def zero_copy(start):
            return pltpu.make_async_copy(zero_ref, buf_ref.at[pl.ds(start, blk)], sem)

        def zbody(e, n_started):
            used = pend_ref[e] > pstart_ref[e]

            @pl.when(used)
            def _():
                zero_copy(lax.shift_right_logical(pend_ref[e], 3) - blk).start()
            return n_started + used.astype(jnp.int32)

        n_tails = lax.fori_loop(0, N_EXPERTS, zbody, 0)

        def tail(j, carry):
            zero_copy(j * blk).start()
            return carry

        lax.fori_loop(nu_ref[0], nb, tail, 0)

        def drain(i, carry):
            zero_copy(0).wait()
            return carry

        lax.fori_loop(0, n_tails + (nb - nu_ref[0]), drain, 0)

    def issue(g, carry):
        for u in range(SUBLANES):
            for k in range(2):
                dest = dest_ref[k * tm + g * SUBLANES + u]
                pltpu.make_async_copy(h2_ref.at[g, pl.ds(u, 1)], _row_ref(buf_ref, dest),
                                      sem).start(priority=k)
        return carry

    lax.fori_loop(0, ng, issue, 0, unroll=ROW_DMA_UNROLL)
    for k in range(2):
        pltpu.make_async_copy(h2_ref, buf_ref.at[pl.ds(0, ng)], sem).wait()


def _dispatch(pstart, pend, n_used, dest_flat, h2, n_rows, tm):
    G, _, D = h2.shape
    ng = tm // SUBLANES
    grid_spec = pltpu.PrefetchScalarGridSpec(
        num_scalar_prefetch=3,
        grid=(G // ng,),
        in_specs=[pl.BlockSpec((2 * tm,), lambda i, *_: (i,), memory_space=pltpu.SMEM),
                  pl.BlockSpec((ng, SUBLANES, D), lambda i, *_: (i, 0, 0))],
        out_specs=pl.BlockSpec(memory_space=pl.ANY),
        scratch_shapes=[pltpu.VMEM((EXPERT_ROWS // SUBLANES, SUBLANES, D), h2.dtype),
                        pltpu.SemaphoreType.DMA(())],
    )
    return pl.pallas_call(
        _dispatch_kernel,
        out_shape=jax.ShapeDtypeStruct((n_rows // SUBLANES, SUBLANES, D), h2.dtype),
        grid_spec=grid_spec,
        compiler_params=_cparams(("arbitrary",)),
        name="dispatch",
    )(pstart, pend, n_used, dest_flat, h2)


def _expert_kernel(be_ref, nu_ref, x_ref, w1_ref, w3_ref, w2_ref, o_ref):
    @pl.when(pl.program_id(0) < nu_ref[0])
    def _():
        x = x_ref[...].astype(BF16)
        a = jnp.dot(x, w1_ref[...], preferred_element_type=F32)
        b = jnp.dot(x, w3_ref[...], preferred_element_type=F32)
        hid = (a * jax.nn.sigmoid(a)) * b
        o_ref[...] = jnp.dot(hid.astype(BF16), w2_ref[...], preferred_element_type=F32)

    @pl.when(pl.program_id(0) >= nu_ref[0])
    def _():
        o_ref[...] = jnp.zeros_like(o_ref)


def _experts(block_e, n_used, buf, w1_bf, w3_bf, w2_bf):
    P, D = buf.shape
    F = w1_bf.shape[2]
    nb = P // EXPERT_ROWS

    def row_map(j, be, nu):
        return (jnp.minimum(j, nu[0] - 1), 0)

    grid_spec = pltpu.PrefetchScalarGridSpec(
        num_scalar_prefetch=2,
        grid=(nb,),
        in_specs=[pl.BlockSpec((EXPERT_ROWS, D), row_map),
                  pl.BlockSpec((None, D, F), lambda j, be, nu: (be[j], 0, 0)),
                  pl.BlockSpec((None, D, F), lambda j, be, nu: (be[j], 0, 0)),
                  pl.BlockSpec((None, F, D), lambda j, be, nu: (be[j], 0, 0))],
        out_specs=pl.BlockSpec((EXPERT_ROWS, D), lambda j, be, nu: (j, 0)),
    )
    return pl.pallas_call(
        _expert_kernel,
        out_shape=jax.ShapeDtypeStruct((P, D), F32),
        grid_spec=grid_spec,
        compiler_params=_cparams(("arbitrary",)),
        name="expert",
    )(block_e, n_used, buf, w1_bf, w3_bf, w2_bf)


def _final_kernel(dcur_ref, dnext_ref, col_ref, x1_ref, mod_ref, fg_ref, yb_ref, o_ref,
                  g_ref, sems):
    ng = x1_ref.shape[0]
    tm = ng * SUBLANES
    i = pl.program_id(0)
    buf = lax.rem(i, 2)

    def issue_tile(d_ref, b):
        def issue(g, carry):
            for u in range(SUBLANES):
                for k in range(2):
                    dest = d_ref[k * tm + g * SUBLANES + u]
                    pltpu.make_async_copy(_row_ref(yb_ref, dest), g_ref.at[b, k, g, pl.ds(u, 1)],
                                          sems.at[b]).start(priority=k)
            return carry

        lax.fori_loop(0, ng, issue, 0, unroll=ROW_DMA_UNROLL)

    @pl.when(i == 0)
    def _():
        issue_tile(dcur_ref, 0)

    @pl.when(i + 1 < pl.num_programs(0))
    def _():
        issue_tile(dnext_ref, 1 - buf)

    for k in range(2):
        pltpu.make_async_copy(yb_ref.at[pl.ds(0, ng)], g_ref.at[buf, k], sems.at[buf]).wait()

    w1 = col_ref[:, :, 4:5]
    w2 = col_ref[:, :, 5:6]
    y = g_ref[buf, 0] * w1 + g_ref[buf, 1] * w2
    x2 = x1_ref[...] + mod_ref[0, 5:6, :] * y
    ms = jnp.mean(x2 * x2, axis=-1, keepdims=True)
    o_ref[...] = (x2 * lax.rsqrt(ms + EPS)) * fg_ref[...]


def _final(dest_flat, col, x1, mod3, fgain, yb, seq, tm):
    G, _, D = x1.shape
    ng = tm // SUBLANES
    n_tiles = G // ng
    tps = seq // tm
    return pl.pallas_call(
        _final_kernel,
        out_shape=jax.ShapeDtypeStruct((G, SUBLANES, D), F32),
        grid=(n_tiles,),
        in_specs=[pl.BlockSpec((2 * tm,), lambda i: (i,), memory_space=pltpu.SMEM),
                  pl.BlockSpec((2 * tm,), lambda i: (jnp.minimum(i + 1, n_tiles - 1),),
                               memory_space=pltpu.SMEM),
                  pl.BlockSpec((ng, SUBLANES, LANES), lambda i: (i, 0, 0)),
                  pl.BlockSpec((ng, SUBLANES, D), lambda i: (i, 0, 0)),
                  pl.BlockSpec((1, N_ADA, D), lambda i: (i // tps, 0, 0)),
                  pl.BlockSpec((1, D), lambda i: (0, 0)),
                  pl.BlockSpec(memory_space=pl.ANY)],
        out_specs=pl.BlockSpec((ng, SUBLANES, D), lambda i: (i, 0, 0)),
        scratch_shapes=[pltpu.VMEM((2, 2, ng, SUBLANES, D), F32),
                        pltpu.SemaphoreType.DMA((2,))],
        compiler_params=_cparams(("arbitrary",)),
        name="final",
    )(dest_flat, dest_flat, col, x1, mod3, fgain, yb)


def _tile(n, pref):
    t = min(n, pref)
    assert n % t == 0, (n, t)
    return t


def kernel(x, c, positions, w_ada, b_ada, norm1_gain, w_in, w_pool, pool_scale, w_branch_pool,
           w_branch_ret, w_out, norm2_gain, w_group, b_group, w_router, b_router, w1, w3, w2,
           final_gain):
    B, S, D = x.shape
    T = B * S
    assert w_ada.shape[0] == 1, "only DEPTH == 1 is supported"
    x2d = x.reshape(T, D)
    posf = positions.astype(F32).reshape(T, 1)
    for l in range(1):
        mod3 = _ada(c, w_ada[l], b_ada[l][None, :]).reshape(B, N_ADA, D)

        n_exp, _, d_exp = w1[l].shape
        n_in = w_in[l].shape[1]
        tn = INPROJ_TN if n_in % INPROJ_TN == 0 else 1024
        proj, (w2_bf,) = _inproj(x2d, mod3, norm1_gain[l][None, :], w_in[l].astype(BF16),
                                 [w2[l].reshape(n_exp * d_exp, D)], S, _tile(S, 1024), tn)
        r, (w1_bf,) = _retention(posf.reshape(B, S, 1), proj.reshape(B, S, -1),
                                 [w1[l].reshape(n_exp * D, d_exp)])
        r = r.reshape(T, -1)
        w1_bf = w1_bf.reshape(n_exp, D, d_exp)
        w2_bf = w2_bf.reshape(n_exp, d_exp, D)

        w_rt = jnp.concatenate(
            [w_group[l], w_router[l],
             jnp.zeros((D, LANES - N_GROUPS - N_EXPERTS), F32)], axis=1)
        wr_hi = w_rt.astype(BF16)
        wr_lo = (w_rt - wr_hi.astype(F32)).astype(BF16)
        wr_cat = jnp.concatenate([wr_hi, wr_lo], axis=1)
        brt = jnp.concatenate(
            [b_group[l], b_router[l], jnp.zeros((LANES - N_GROUPS - N_EXPERTS,), F32)])[None, :]
        x1, h2, logits, w3_bf = _mix(
            proj, r, x2d, mod3, w_pool[l].astype(BF16), pool_scale[l][None, :],
            w_branch_pool[l].astype(BF16), w_branch_ret[l].astype(BF16), w_out[l].astype(BF16),
            norm2_gain[l][None, :], wr_cat, brt, [w3[l].reshape(n_exp * D, d_exp)], S,
            _tile(S, 256))
        w3_bf = w3_bf.reshape(n_exp, D, d_exp)

        col, row, cnt = _route(logits, _tile(T, 512))
        counts = cnt[0, ROUTE_LANE0:ROUTE_LANE0 + N_EXPERTS].astype(jnp.int32)
        padded = (counts + EXPERT_ROWS - 1) // EXPERT_ROWS * EXPERT_ROWS
        pend = jnp.cumsum(padded)
        pstart = pend - padded
        n_rows = 2 * T + N_EXPERTS * EXPERT_ROWS
        nb = n_rows // EXPERT_ROWS
        n_used = (pend[-1:] // EXPERT_ROWS).astype(jnp.int32)
        block_row0 = jnp.arange(nb, dtype=jnp.int32) * EXPERT_ROWS
        block_e = jnp.minimum(
            jnp.sum((pend[None, :] <= block_row0[:, None]).astype(jnp.int32), axis=1),
            N_EXPERTS - 1)

        dest2 = _dest(pstart, row, _tile(T, 4096))[:2]
        tm_d = _tile(T, 1024)
        tm_f = _tile(S, 512)
        buf = _dispatch(pstart, pend, n_used, _tile_major(dest2, tm_d),
                        h2.reshape(T // SUBLANES, SUBLANES, D), n_rows, tm_d)
        yb = _experts(block_e, n_used, buf.reshape(n_rows, D), w1_bf, w3_bf, w2_bf)
        out = _final(_tile_major(dest2, tm_f), col.reshape(T // SUBLANES, SUBLANES, LANES),
                     x1.reshape(T // SUBLANES, SUBLANES, D), mod3, final_gain[None, :],
                     yb.reshape(n_rows // SUBLANES, SUBLANES, D), S, tm_f)
    return out.reshape(B, S, D)
```

```python
import functools

import jax
import jax.numpy as jnp
from jax import lax
from jax.experimental import pallas as pl
from jax.experimental.pallas import tpu as pltpu

F32 = jnp.float32
BF16 = jnp.bfloat16

EPS = 1e-6
ROPE_BASE = 10000.0
POOL_WINDOWS = (2, 4, 8, 16)
POOL_HALO = 32
RET_HEADS = 8
RET_CHUNK = 128
N_GROUPS = 4
EXPERTS_PER_GROUP = 8
N_EXPERTS = N_GROUPS * EXPERTS_PER_GROUP
N_ADA = 6
LANES = 128
ROUTE_LANE0 = N_GROUPS
EXPERT_ROWS = 256
INPROJ_TN = 1536
SUBLANES = 8
ROW_DMA_UNROLL = 2
VMEM_LIMIT = 56 * 1024 * 1024


def _cparams(sem):
    return pltpu.CompilerParams(dimension_semantics=sem, vmem_limit_bytes=VMEM_LIMIT)


def _const_spec(shape):
    n = len(shape)
    return pl.BlockSpec(shape, lambda *_: (0,) * n, pipeline_mode=pl.Buffered(1))


def _ada_kernel(cb_ref, w_ref, b_ref, o_ref, cact_ref):
    cb = cb_ref[...]
    cact_ref[...] = cb * jax.nn.sigmoid(cb)
    tn = w_ref.shape[1]
    for b in range(cb_ref.shape[0]):
        for j in range(tn // LANES):
            sl = slice(j * LANES, (j + 1) * LANES)
            prod = w_ref[:, sl] * cact_ref[b]
            o_ref[b:b + 1, sl] = jnp.sum(prod, axis=0, keepdims=True) + b_ref[:, sl]


def _ada(c, w_ada, b_ada):
    B, D = c.shape
    N = w_ada.shape[1]
    tn = 1024
    cb = jnp.broadcast_to(c[:, :, None], (B, D, LANES))
    return pl.pallas_call(
        _ada_kernel,
        out_shape=jax.ShapeDtypeStruct((B, N), F32),
        grid=(N // tn,),
        in_specs=[
            pl.BlockSpec((B, D, LANES), lambda j: (0, 0, 0)),
            pl.BlockSpec((D, tn), lambda j: (0, j)),
            pl.BlockSpec((1, tn), lambda j: (0, j)),
        ],
        out_specs=pl.BlockSpec((B, tn), lambda j: (0, j)),
        scratch_shapes=[pltpu.VMEM((B, D, LANES), F32)],
        compiler_params=_cparams(("arbitrary",)),
        name="ada",
    )(cb, w_ada, b_ada)


def _norm_mod_rows(x, gain, shift, scale):
    ms = jnp.mean(x * x, axis=-1, keepdims=True)
    y = (x * lax.rsqrt(ms + EPS)) * gain
    return y * (1.0 + scale) + shift


SIDE_CAST_BLOCK_BYTES = 2 * 1024 * 1024
SIDE_CAST_MAX_BLOCK_BYTES = 4 * 1024 * 1024


def _side_cast_rows(n_rows, n_cols, n_steps):
    rows = SIDE_CAST_BLOCK_BYTES // (4 * n_cols)
    while rows * 4 * n_cols <= SIDE_CAST_MAX_BLOCK_BYTES:
        if n_rows % rows == 0 and n_rows // rows <= n_steps:
            return rows
        rows *= 2
    return None


def _side_specs(side, n_steps, step_of):
    rows = [_side_cast_rows(a.shape[0], a.shape[1], n_steps) for a in side]
    if any(r is None for r in rows):
        return None
    specs = []
    for a, r in zip(side, rows):
        nb = a.shape[0] // r
        specs.append(pl.BlockSpec(
            (r, a.shape[1]), lambda *g, nb=nb: (jnp.minimum(step_of(*g), nb - 1), 0)))
    return specs


def _cast_side(side_in, side_out):
    for src_ref, dst_ref in zip(side_in, side_out):
        dst_ref[...] = src_ref[...].astype(BF16)


def _inproj_kernel(x_ref, mod_ref, g_ref, w_ref, *rest, n_side):
    side_in = rest[:n_side]
    o_ref = rest[n_side]
    side_out = rest[n_side + 1:2 * n_side + 1]
    h_ref, r_ref = rest[2 * n_side + 1:]
    tm, d = x_ref.shape
    rc = 32

    @pl.when(pl.program_id(1) == 0)
    def _():
        shift = mod_ref[0, 0:1, :]
        mult = g_ref[...] * (1.0 + mod_ref[0, 1:2, :])

        def stats(c, carry):
            rows = pl.ds(pl.multiple_of(c * rc, rc), rc)
            x = x_ref[rows, :]
            r = lax.rsqrt(jnp.mean(x * x, axis=-1, keepdims=True) + EPS)
            r_ref[rows, :] = jnp.broadcast_to(r, (rc, LANES))
            return carry

        lax.fori_loop(0, tm // rc, stats, 0, unroll=4)

        def apply(c, carry):
            rows = pl.ds(pl.multiple_of(c * rc, rc), rc)
            r = r_ref[rows, :]
            for j in range(d // LANES):
                cols = slice(j * LANES, (j + 1) * LANES)
                h_ref[rows, cols] = ((x_ref[rows, cols] * r) * mult[:, cols]
                                     + shift[:, cols]).astype(BF16)
            return carry

        lax.fori_loop(0, tm // rc, apply, 0)

    _cast_side(side_in, side_out)
    o_ref[...] = jnp.dot(h_ref[...], w_ref[...], preferred_element_type=F32).astype(o_ref.dtype)


def _inproj(x2d, mod3, gain, w_in_bf, side, seq, tm, tn):
    T, D = x2d.shape
    N = w_in_bf.shape[1]
    tps = seq // tm
    nj = N // tn
    side_specs = _side_specs(side, (T // tm) * nj, lambda i, j: i * nj + j)
    if side_specs is None:
        out = _inproj(x2d, mod3, gain, w_in_bf, [], seq, tm, tn)
        return out[0], [a.astype(BF16) for a in side]
    outs = pl.pallas_call(
        functools.partial(_inproj_kernel, n_side=len(side)),
        out_shape=[jax.ShapeDtypeStruct((T, N), BF16)]
        + [jax.ShapeDtypeStruct(a.shape, BF16) for a in side],
        grid=(T // tm, nj),
        in_specs=[
            pl.BlockSpec((tm, D), lambda i, j: (i, 0)),
            pl.BlockSpec((1, N_ADA, D), lambda i, j: (i // tps, 0, 0)),
            pl.BlockSpec((1, D), lambda i, j: (0, 0)),
            pl.BlockSpec((D, tn), lambda i, j: (0, j)),
        ] + side_specs,
        out_specs=[pl.BlockSpec((tm, tn), lambda i, j: (i, j))] + side_specs,
        scratch_shapes=[pltpu.VMEM((tm, D), BF16), pltpu.VMEM((tm, LANES), F32)],
        compiler_params=_cparams(("arbitrary", "arbitrary")),
        name="inproj",
    )(x2d, mod3, gain, w_in_bf, *side)
    return outs[0], list(outs[1:])


def _ret_kernel(pos_ref, q_ref, k_ref, v_ref, rg_ref, invf_ref, sign_ref, mask_ref,
                qd_ref, kd_ref, cd_ref, *rest, k_scale, n_side):
    side_in = rest[:n_side]
    o_ref = rest[n_side]
    side_out = rest[n_side + 1:2 * n_side + 1]
    state_ref = rest[2 * n_side + 1]
    C = RET_CHUNK
    hc = C // 2
    nseq = q_ref.shape[0]

    @pl.when(pl.program_id(1) == 0)
    def _():
        state_ref[...] = jnp.zeros_like(state_ref)

    _cast_side(side_in, side_out)

    lo_lanes = lax.broadcasted_iota(jnp.int32, (hc, C), 1) < hc

    def spread(t):
        sw = pltpu.roll(t, hc, 1)
        return jnp.concatenate([jnp.where(lo_lanes, t, sw), jnp.where(lo_lanes, sw, t)], axis=0)

    for c in range(q_ref.shape[1] // C):
        rows = slice(c * C, (c + 1) * C)
        trig = []
        for b in range(nseq):
            pos2 = jnp.where(lo_lanes, pos_ref[b, c * C:c * C + hc, :],
                             pos_ref[b, c * C + hc:(c + 1) * C, :])
            ang = pos2 * invf_ref[...]
            trig.append((spread(jnp.cos(ang)), spread(jnp.sin(ang)) * sign_ref[...]))

        for h in range(RET_HEADS):
            cols = slice(h * C, (h + 1) * C)
            for b in range(nseq):
                cosv, sinv = trig[b]
                q = q_ref[b, rows, cols].astype(F32)
                k = k_ref[b, rows, cols].astype(F32)
                qr = q * cosv + pltpu.roll(q, hc, 1) * sinv
                kr = (k * cosv + pltpu.roll(k, hc, 1) * sinv) * k_scale
                v = v_ref[b, rows, cols]
                s = lax.dot_general(qr.astype(BF16), kr.astype(BF16), (((1,), (1,)), ((), ())),
                                    preferred_element_type=F32) * mask_ref[h]
                st = state_ref[b, h]
                lhs = jnp.concatenate([s.astype(BF16), (qr * qd_ref[h]).astype(BF16)], axis=1)
                rhs = jnp.concatenate([v, st.astype(BF16)], axis=0)
                o = jnp.dot(lhs, rhs, preferred_element_type=F32)
                kv = lax.dot_general((kr * kd_ref[h]).astype(BF16), v, (((0,), (0,)), ((), ())),
                                     preferred_element_type=F32)
                state_ref[b, h] = st * cd_ref[h] + kv
                o = o * lax.rsqrt(jnp.mean(o * o, axis=-1, keepdims=True) + EPS)
                g = rg_ref[b, rows, cols].astype(F32)
                o_ref[b, rows, cols] = (o * (g * jax.nn.sigmoid(g))).astype(o_ref.dtype)


RET_SEQS_PER_STEP = 2
RET_CHUNKS_PER_STEP = 2


def _retention(pos3, proj3, side):
    batch, seq, _ = proj3.shape
    H, C = RET_HEADS, RET_CHUNK
    W = H * C
    nseq = RET_SEQS_PER_STEP if batch % RET_SEQS_PER_STEP == 0 else 1
    nchunk = RET_CHUNKS_PER_STEP if (seq // C) % RET_CHUNKS_PER_STEP == 0 else 1
    ct = nchunk * C
    spb = seq // ct
    half = C // 2
    inv_freq = ROPE_BASE ** (-jnp.arange(half, dtype=F32) / half)
    invf2 = jnp.concatenate([inv_freq, inv_freq])[None, :]
    sign = jnp.concatenate([-jnp.ones((half,), F32), jnp.ones((half,), F32)])[None, :]
    log_gamma = jnp.log1p(-jnp.exp2(-5.0 - jnp.arange(H, dtype=F32)))
    idx = jnp.arange(C, dtype=F32)
    diff = idx[:, None] - idx[None, :]
    mask = jnp.where(diff >= 0, jnp.exp(log_gamma[:, None, None] * jnp.maximum(diff, 0.0)), 0.0)
    q_decay = jnp.exp(log_gamma[:, None] * (idx + 1.0))
    k_decay = jnp.exp(log_gamma[:, None] * (C - 1.0 - idx))
    chunk_decay = jnp.exp(log_gamma * C)
    qd = jnp.broadcast_to(q_decay[:, :, None], (H, C, C))
    kd = jnp.broadcast_to(k_decay[:, :, None], (H, C, C))
    cd = jnp.broadcast_to(chunk_decay[:, None, None], (H, 1, C))

    def tok(col):
        return pl.BlockSpec((nseq, ct, W), lambda b, n, col=col: (b, n, col))

    side_specs = _side_specs(side, (batch // nseq) * spb, lambda b, n: b * spb + n)
    if side_specs is None:
        out = _retention(pos3, proj3, [])
        return out[0], [a.astype(BF16) for a in side]
    outs = pl.pallas_call(
        functools.partial(_ret_kernel, k_scale=float(C) ** -0.5, n_side=len(side)),
        out_shape=[jax.ShapeDtypeStruct((batch, seq, W), BF16)]
        + [jax.ShapeDtypeStruct(a.shape, BF16) for a in side],
        grid=(batch // nseq, spb),
        in_specs=[
            pl.BlockSpec((nseq, ct, 1), lambda b, n: (b, n, 0)),
            tok(1), tok(2), tok(3), tok(4),
            pl.BlockSpec((1, C), lambda b, n: (0, 0)),
            pl.BlockSpec((1, C), lambda b, n: (0, 0)),
            pl.BlockSpec((H, C, C), lambda b, n: (0, 0, 0)),
            pl.BlockSpec((H, C, C), lambda b, n: (0, 0, 0)),
            pl.BlockSpec((H, C, C), lambda b, n: (0, 0, 0)),
            pl.BlockSpec((H, 1, C), lambda b, n: (0, 0, 0)),
        ] + side_specs,
        out_specs=[pl.BlockSpec((nseq, ct, W), lambda b, n: (b, n, 0))] + side_specs,
        scratch_shapes=[pltpu.VMEM((nseq, H, C, C), F32)],
        compiler_params=_cparams(("arbitrary", "arbitrary")),
        name="ret",
    )(pos3, proj3, proj3, proj3, proj3, invf2, sign, mask, qd, kd, cd, *side)
    return outs[0], list(outs[1:])


def _mix_kernel(a_ref, halo_ref, r_ref, ga0_ref, ga1_ref, gb0_ref, gb1_ref, x_ref, mod_ref,
                wpool_ref, pscale_ref, wbp_ref, wbr_ref, wout_ref, g2_ref,
                wrc_ref, brt_ref, *rest, tiles_per_seq, n_side):
    side_in = rest[:n_side]
    x1_ref, h2_ref, lg_ref = rest[n_side:n_side + 3]
    side_out = rest[n_side + 3:2 * n_side + 3]
    ext_ref, lvl_ref, pm_ref, mg_ref, yr_ref = rest[2 * n_side + 3:]
    _cast_side(side_in, side_out)
    tm = a_ref.shape[0]
    pw = a_ref.shape[1]
    gd = pw // len(POOL_WINDOWS)
    i = pl.program_id(0)
    it = i % tiles_per_seq
    yr_ref[...] = jnp.dot(r_ref[...], wbr_ref[...], preferred_element_type=F32)
    halo = halo_ref[...].astype(F32)
    ext_ref[0:POOL_HALO, :] = jnp.where(it == 0, 0.0, halo)
    ext_ref[POOL_HALO:, :] = a_ref[...].astype(F32)
    tpos = it * tm + lax.broadcasted_iota(jnp.int32, (tm, 1), 0)
    top = POOL_HALO + tm
    for g, w in enumerate(POOL_WINDOWS):
        cols = slice(g * gd, (g + 1) * gd)
        cur = ext_ref[POOL_HALO:top, cols]
        n_levels = w.bit_length() - 1
        prev_ref, prev_cols = ext_ref, cols
        for k in range(1, n_levels + 1):
            lo = POOL_HALO if k == n_levels else SUBLANES * k
            sh = 1 << (k - 1)
            s = prev_ref[lo:top, prev_cols] + prev_ref[lo - sh:top - sh, prev_cols]
            if k < n_levels:
                prev_ref, prev_cols = lvl_ref.at[k % 2], slice(None)
                prev_ref[lo:top, :] = s
        cnt = jnp.minimum(tpos + 1, w).astype(F32)
        pooled = s / cnt - cur
        pm = jnp.dot(pooled.astype(BF16), wpool_ref[g], preferred_element_type=F32)
        pm_ref[:, cols] = (pm * pscale_ref[:, cols]).astype(BF16)

    y_pool = jnp.dot(pm_ref[...], wbp_ref[...], preferred_element_type=F32)
    half = y_pool.shape[1] // 2
    for hh, (ga_ref, gb_ref) in enumerate(((ga0_ref, gb0_ref), (ga1_ref, gb1_ref))):
        cols = slice(hh * half, (hh + 1) * half)
        ga = jax.nn.sigmoid(ga_ref[...].astype(F32))
        gb = jax.nn.sigmoid(gb_ref[...].astype(F32))
        mg_ref[:, cols] = (ga * y_pool[:, cols] + gb * yr_ref[:, cols]).astype(BF16)
    z = jnp.dot(mg_ref[...], wout_ref[...], preferred_element_type=F32)
    gate1 = mod_ref[0, 2:3, :]
    x1 = x_ref[...] + gate1 * z
    x1_ref[...] = x1
    h2 = _norm_mod_rows(x1, g2_ref[...], mod_ref[0, 3:4, :], mod_ref[0, 4:5, :])
    h2_ref[...] = h2
    h_hi = h2.astype(BF16)
    h_lo = (h2 - h_hi.astype(F32)).astype(BF16)
    parts = jnp.dot(jnp.concatenate([h_hi, h_lo], axis=0), wrc_ref[...],
                    preferred_element_type=F32)
    lg = (parts[:tm, :LANES] + parts[tm:, :LANES]) + (parts[:tm, LANES:] + parts[tm:, LANES:])
    lg_ref[...] = lg + brt_ref[...]


def _mix(proj, r, x2d, mod3, wpool_bf, pscale, wbp_bf, wbr_bf, wout_bf, g2, wr_cat, brt,
         side, seq, tm):
    T, D = x2d.shape
    PW = wbp_bf.shape[0]
    tps = seq // tm
    hb = tm // POOL_HALO

    def tok(col):
        return pl.BlockSpec((tm, PW), lambda i, col=col: (i, col))

    side_specs = _side_specs(side, T // tm, lambda i: i)
    if side_specs is None:
        outs = _mix(proj, r, x2d, mod3, wpool_bf, pscale, wbp_bf, wbr_bf, wout_bf, g2, wr_cat, brt,
                    [], seq, tm)
        return outs[:3] + [a.astype(BF16) for a in side]
    kern = functools.partial(_mix_kernel, tiles_per_seq=tps, n_side=len(side))
    outs = pl.pallas_call(
        kern,
        out_shape=[jax.ShapeDtypeStruct((T, D), F32),
                   jax.ShapeDtypeStruct((T, D), F32),
                   jax.ShapeDtypeStruct((T, LANES), F32)]
        + [jax.ShapeDtypeStruct(a.shape, BF16) for a in side],
        grid=(T // tm,),
        in_specs=[
            tok(0),
            pl.BlockSpec((POOL_HALO, PW), lambda i: (jnp.maximum(i * hb - 1, 0), 0)),
            pl.BlockSpec((tm, PW), lambda i: (i, 0)),
            tok(5), tok(6), tok(7), tok(8),
            pl.BlockSpec((tm, D), lambda i: (i, 0)),
            pl.BlockSpec((1, N_ADA, D), lambda i: (i // tps, 0, 0)),
            _const_spec(wpool_bf.shape),
            _const_spec(pscale.shape),
            _const_spec(wbp_bf.shape),
            _const_spec(wbr_bf.shape),
            _const_spec(wout_bf.shape),
            _const_spec(g2.shape),
            _const_spec(wr_cat.shape),
            _const_spec(brt.shape),
        ] + side_specs,
        out_specs=[pl.BlockSpec((tm, D), lambda i: (i, 0)),
                   pl.BlockSpec((tm, D), lambda i: (i, 0)),
                   pl.BlockSpec((tm, LANES), lambda i: (i, 0))] + side_specs,
        scratch_shapes=[pltpu.VMEM((POOL_HALO + tm, PW), F32),
                        pltpu.VMEM((2, POOL_HALO + tm, PW // len(POOL_WINDOWS)), F32),
                        pltpu.VMEM((tm, PW), BF16),
                        pltpu.VMEM((tm, D), BF16),
                        pltpu.VMEM((tm, D), F32)],
        compiler_params=_cparams(("arbitrary",)),
        name="mix",
    )(proj, proj, r, proj, proj, proj, proj, x2d, mod3, wpool_bf, pscale, wbp_bf, wbr_bf,
      wout_bf, g2, wr_cat, brt, *side)
    return list(outs)


def _route_kernel(lg_ref, tri_ref, col_ref, row_ref, cnt_ref, carry_ref):
    tm = lg_ref.shape[0]

    @pl.when(pl.program_id(0) == 0)
    def _():
        carry_ref[...] = jnp.zeros_like(carry_ref)

    L = lg_ref[...]
    lane = lax.broadcasted_iota(jnp.int32, (tm, LANES), 1).astype(F32)
    neg = -jnp.inf
    none = float(LANES)
    is_g = lane < N_GROUPS
    gl = jnp.where(is_g, L, neg)
    gmax = jnp.max(gl, axis=1, keepdims=True)
    grp = jnp.min(jnp.where(gl == gmax, lane, none), axis=1, keepdims=True)
    gsum = jnp.sum(jnp.where(is_g, jnp.exp(gl - gmax), 0.0), axis=1, keepdims=True)
    p_grp = 1.0 / gsum
    lo = ROUTE_LANE0 + grp * EXPERTS_PER_GROUP
    el = jnp.where((lane >= lo) & (lane < lo + EXPERTS_PER_GROUP), L, neg)
    v1 = jnp.max(el, axis=1, keepdims=True)
    i1 = jnp.min(jnp.where(el == v1, lane, none), axis=1, keepdims=True)
    el2 = jnp.where(lane == i1, neg, el)
    v2 = jnp.max(el2, axis=1, keepdims=True)
    i2 = jnp.min(jnp.where(el2 == v2, lane, none), axis=1, keepdims=True)
    e = jnp.exp(v2 - v1)
    w1 = p_grp / (1.0 + e)
    w2 = p_grp * e / (1.0 + e)
    sel1 = lane == i1
    sel2 = lane == i2
    onehot = jnp.where(sel1 | sel2, 1.0, 0.0)
    cum = jnp.dot(tri_ref[...], onehot.astype(BF16), preferred_element_type=F32)
    cum = cum + carry_ref[0:1, :]
    r1 = jnp.sum(jnp.where(sel1, cum, 0.0), axis=1, keepdims=True)
    r2 = jnp.sum(jnp.where(sel2, cum, 0.0), axis=1, keepdims=True)
    carry_ref[...] = carry_ref[...] + jnp.sum(onehot, axis=0, keepdims=True)
    cnt_ref[...] = carry_ref[...]
    slab = jnp.where(lane == 0, i1 - ROUTE_LANE0, 0.0)
    slab = jnp.where(lane == 1, i2 - ROUTE_LANE0, slab)
    slab = jnp.where(lane == 2, r1, slab)
    slab = jnp.where(lane == 3, r2, slab)
    slab = jnp.where(lane == 4, w1, slab)
    slab = jnp.where(lane == 5, w2, slab)
    col_ref[...] = slab
    row_ref[...] = slab.T[0:8, :]


def _route(logits, tm):
    T = logits.shape[0]
    tri = jnp.tril(jnp.ones((tm, tm), F32), -1).astype(BF16)
    return pl.pallas_call(
        _route_kernel,
        out_shape=(jax.ShapeDtypeStruct((T, LANES), F32),
                   jax.ShapeDtypeStruct((8, T), F32),
                   jax.ShapeDtypeStruct((8, LANES), F32)),
        grid=(T // tm,),
        in_specs=[pl.BlockSpec((tm, LANES), lambda i: (i, 0)),
                  pl.BlockSpec((tm, tm), lambda i: (0, 0))],
        out_specs=(pl.BlockSpec((tm, LANES), lambda i: (i, 0)),
                   pl.BlockSpec((8, tm), lambda i: (0, i)),
                   pl.BlockSpec((8, LANES), lambda i: (0, 0))),
        scratch_shapes=[pltpu.VMEM((8, LANES), F32)],
        compiler_params=_cparams(("arbitrary",)),
        name="route",
    )(logits, tri)


def _dest_kernel(pstart_ref, row_ref, o_ref):
    e = row_ref[0:2, :].astype(jnp.int32)
    r = row_ref[2:4, :].astype(jnp.int32)
    base = jnp.zeros_like(e)
    for k in range(N_EXPERTS):
        base = jnp.where(e == k, pstart_ref[k], base)
    o_ref[...] = jnp.zeros_like(o_ref)
    o_ref[0:2, :] = base + r


def _dest(pstart, row, tn):
    T = row.shape[1]
    grid_spec = pltpu.PrefetchScalarGridSpec(
        num_scalar_prefetch=1,
        grid=(T // tn,),
        in_specs=[pl.BlockSpec((SUBLANES, tn), lambda i, *_: (0, i))],
        out_specs=pl.BlockSpec((SUBLANES, tn), lambda i, *_: (0, i)),
    )
    return pl.pallas_call(
        _dest_kernel,
        out_shape=jax.ShapeDtypeStruct((SUBLANES, T), jnp.int32),
        grid_spec=grid_spec,
        compiler_params=_cparams(("arbitrary",)),
        name="dest",
    )(pstart, row)


def _tile_major(dest2, tm):
    T = dest2.shape[1]
    return dest2.reshape(2, T // tm, tm).transpose(1, 0, 2).reshape(-1)


def _row_ref(ref, row):
    return ref.at[lax.shift_right_logical(row, 3), pl.ds(jnp.bitwise_and(row, SUBLANES - 1), 1)]


def _dispatch_kernel(pstart_ref, pend_ref, nu_ref, dest_ref, h2_ref, buf_ref, zero_ref, sem):
    ng = h2_ref.shape[0]
    tm = ng * SUBLANES
    blk = zero_ref.shape[0]
    nb = buf_ref.shape[0] // blk

    @pl.when(pl.program_id(0) == 0)
    def _():
        zero_ref[...] = jnp.zeros_like(zero_ref)

        def zero_copy(start):
            return pltpu.make_async_copy(zero_ref, buf_ref.at[pl.ds(start, blk)], sem)

        def zbody(e, n_started):
            used = pend_ref[e] > pstart_ref[e]

            @pl.when(used)
            def _():
                zero_copy(lax.shift_right_logical(pend_ref[e], 3) - blk).start()
            return n_started + used.astype(jnp.int32)

        n_tails = lax.fori_loop(0, N_EXPERTS, zbody, 0)

        def tail(j, carry):
            zero_copy(j * blk).start()
            return carry

        lax.fori_loop(nu_ref[0], nb, tail, 0)

        def drain(i, carry):
            zero_copy(0).wait()
            return carry

        lax.fori_loop(0, n_tails + (nb - nu_ref[0]), drain, 0)

    def issue(g, carry):
        for u in range(SUBLANES):
            for k in range(2):
                dest = dest_ref[k * tm + g * SUBLANES + u]
                pltpu.make_async_copy(h2_ref.at[g, pl.ds(u, 1)], _row_ref(buf_ref, dest),
                                      sem).start(priority=k)
        return carry

    lax.fori_loop(0, ng, issue, 0, unroll=ROW_DMA_UNROLL)
    for k in range(2):
        pltpu.make_async_copy(h2_ref, buf_ref.at[pl.ds(0, ng)], sem).wait()


def _dispatch(pstart, pend, n_used, dest_flat, h2, n_rows, tm):
    G, _, D = h2.shape
    ng = tm // SUBLANES
    grid_spec = pltpu.PrefetchScalarGridSpec(
        num_scalar_prefetch=3,
        grid=(G // ng,),
        in_specs=[pl.BlockSpec((2 * tm,), lambda i, *_: (i,), memory_space=pltpu.SMEM),
                  pl.BlockSpec((ng, SUBLANES, D), lambda i, *_: (i, 0, 0))],
        out_specs=pl.BlockSpec(memory_space=pl.ANY),
        scratch_shapes=[pltpu.VMEM((EXPERT_ROWS // SUBLANES, SUBLANES, D), h2.dtype),
                        pltpu.SemaphoreType.DMA(())],
    )
    return pl.pallas_call(
        _dispatch_kernel,
        out_shape=jax.ShapeDtypeStruct((n_rows // SUBLANES, SUBLANES, D), h2.dtype),
        grid_spec=grid_spec,
        compiler_params=_cparams(("arbitrary",)),
        name="dispatch",
    )(pstart, pend, n_used, dest_flat, h2)


def _expert_kernel(be_ref, nu_ref, x_ref, w1_ref, w3_ref, w2_ref, o_ref):
    @pl.when(pl.program_id(0) < nu_ref[0])
    def _():
        x = x_ref[...].astype(BF16)
        a = jnp.dot(x, w1_ref[...], preferred_element_type=F32)
        b = jnp.dot(x, w3_ref[...], preferred_element_type=F32)
        hid = (a * jax.nn.sigmoid(a)) * b
        o_ref[...] = jnp.dot(hid.astype(BF16), w2_ref[...], preferred_element_type=F32)

    @pl.when(pl.program_id(0) >= nu_ref[0])
    def _():
        o_ref[...] = jnp.zeros_like(o_ref)


def _experts(block_e, n_used, buf, w1_bf, w3_bf, w2_bf):
    P, D = buf.shape
    F = w1_bf.shape[2]
    nb = P // EXPERT_ROWS

    def row_map(j, be, nu):
        return (jnp.minimum(j, nu[0] - 1), 0)

    grid_spec = pltpu.PrefetchScalarGridSpec(
        num_scalar_prefetch=2,
        grid=(nb,),
        in_specs=[pl.BlockSpec((EXPERT_ROWS, D), row_map),
                  pl.BlockSpec((None, D, F), lambda j, be, nu: (be[j], 0, 0)),
                  pl.BlockSpec((None, D, F), lambda j, be, nu: (be[j], 0, 0)),
                  pl.BlockSpec((None, F, D), lambda j, be, nu: (be[j], 0, 0))],
        out_specs=pl.BlockSpec((EXPERT_ROWS, D), lambda j, be, nu: (j, 0)),
    )
    return pl.pallas_call(
        _expert_kernel,
        out_shape=jax.ShapeDtypeStruct((P, D), F32),
        grid_spec=grid_spec,
        compiler_params=_cparams(("arbitrary",)),
        name="expert",
    )(block_e, n_used, buf, w1_bf, w3_bf, w2_bf)


def _final_kernel(dcur_ref, dnext_ref, col_ref, x1_ref, mod_ref, fg_ref, yb_ref, o_ref,
                  g_ref, sems):
    ng = x1_ref.shape[0]
    tm = ng * SUBLANES
    i = pl.program_id(0)
    buf = lax.rem(i, 2)

    def issue_tile(d_ref, b):
        def issue(g, carry):
            for u in range(SUBLANES):
                for k in range(2):
                    dest = d_ref[k * tm + g * SUBLANES + u]
                    pltpu.make_async_copy(_row_ref(yb_ref, dest), g_ref.at[b, k, g, pl.ds(u, 1)],
                                          sems.at[b]).start(priority=k)
            return carry

        lax.fori_loop(0, ng, issue, 0, unroll=ROW_DMA_UNROLL)

    @pl.when(i == 0)
    def _():
        issue_tile(dcur_ref, 0)

    @pl.when(i + 1 < pl.num_programs(0))
    def _():
        issue_tile(dnext_ref, 1 - buf)

    for k in range(2):
        pltpu.make_async_copy(yb_ref.at[pl.ds(0, ng)], g_ref.at[buf, k], sems.at[buf]).wait()

    w1 = col_ref[:, :, 4:5]
    w2 = col_ref[:, :, 5:6]
    y = g_ref[buf, 0] * w1 + g_ref[buf, 1] * w2
    x2 = x1_ref[...] + mod_ref[0, 5:6, :] * y
    ms = jnp.mean(x2 * x2, axis=-1, keepdims=True)
    o_ref[...] = (x2 * lax.rsqrt(ms + EPS)) * fg_ref[...]


def _final(dest_flat, col, x1, mod3, fgain, yb, seq, tm):
    G, _, D = x1.shape
    ng = tm // SUBLANES
    n_tiles = G // ng
    tps = seq // tm
    return pl.pallas_call(
        _final_kernel,
        out_shape=jax.ShapeDtypeStruct((G, SUBLANES, D), F32),
        grid=(n_tiles,),
        in_specs=[pl.BlockSpec((2 * tm,), lambda i: (i,), memory_space=pltpu.SMEM),
                  pl.BlockSpec((2 * tm,), lambda i: (jnp.minimum(i + 1, n_tiles - 1),),
                               memory_space=pltpu.SMEM),
                  pl.BlockSpec((ng, SUBLANES, LANES), lambda i: (i, 0, 0)),
                  pl.BlockSpec((ng, SUBLANES, D), lambda i: (i, 0, 0)),
                  pl.BlockSpec((1, N_ADA, D), lambda i: (i // tps, 0, 0)),
                  pl.BlockSpec((1, D), lambda i: (0, 0)),
                  pl.BlockSpec(memory_space=pl.ANY)],
        out_specs=pl.BlockSpec((ng, SUBLANES, D), lambda i: (i, 0, 0)),
        scratch_shapes=[pltpu.VMEM((2, 2, ng, SUBLANES, D), F32),
                        pltpu.SemaphoreType.DMA((2,))],
        compiler_params=_cparams(("arbitrary",)),
        name="final",
    )(dest_flat, dest_flat, col, x1, mod3, fgain, yb)


def _tile(n, pref):
    t = min(n, pref)
    assert n % t == 0, (n, t)
    return t


def kernel(x, c, positions, w_ada, b_ada, norm1_gain, w_in, w_pool, pool_scale, w_branch_pool,
           w_branch_ret, w_out, norm2_gain, w_group, b_group, w_router, b_router, w1, w3, w2,
           final_gain):
    B, S, D = x.shape
    T = B * S
    assert w_ada.shape[0] == 1, "only DEPTH == 1 is supported"
    x2d = x.reshape(T, D)
    posf = positions.astype(F32).reshape(T, 1)
    for l in range(1):
        mod3 = _ada(c, w_ada[l], b_ada[l][None, :]).reshape(B, N_ADA, D)

        n_exp, _, d_exp = w1[l].shape
        n_in = w_in[l].shape[1]
        tn = INPROJ_TN if n_in % INPROJ_TN == 0 else 1024
        proj, (w2_bf,) = _inproj(x2d, mod3, norm1_gain[l][None, :], w_in[l].astype(BF16),
                                 [w2[l].reshape(n_exp * d_exp, D)], S, _tile(S, 1024), tn)
        r, (w1_bf,) = _retention(posf.reshape(B, S, 1), proj.reshape(B, S, -1),
                                 [w1[l].reshape(n_exp * D, d_exp)])
        r = r.reshape(T, -1)
        w1_bf = w1_bf.reshape(n_exp, D, d_exp)
        w2_bf = w2_bf.reshape(n_exp, d_exp, D)

        w_rt = jnp.concatenate(
            [w_group[l], w_router[l],
             jnp.zeros((D, LANES - N_GROUPS - N_EXPERTS), F32)], axis=1)
        wr_hi = w_rt.astype(BF16)
        wr_lo = (w_rt - wr_hi.astype(F32)).astype(BF16)
        wr_cat = jnp.concatenate([wr_hi, wr_lo], axis=1)
        brt = jnp.concatenate(
            [b_group[l], b_router[l], jnp.zeros((LANES - N_GROUPS - N_EXPERTS,), F32)])[None, :]
        x1, h2, logits, w3_bf = _mix(
            proj, r, x2d, mod3, w_pool[l].astype(BF16), pool_scale[l][None, :],
            w_branch_pool[l].astype(BF16), w_branch_ret[l].astype(BF16), w_out[l].astype(BF16),
            norm2_gain[l][None, :], wr_cat, brt, [w3[l].reshape(n_exp * D, d_exp)], S,
            _tile(S, 256))
        w3_bf = w3_bf.reshape(n_exp, D, d_exp)

        col, row, cnt = _route(logits, _tile(T, 512))
        counts = cnt[0, ROUTE_LANE0:ROUTE_LANE0 + N_EXPERTS].astype(jnp.int32)
        padded = (counts + EXPERT_ROWS - 1) // EXPERT_ROWS * EXPERT_ROWS
        pend = jnp.cumsum(padded)
        pstart = pend - padded
        n_rows = 2 * T + N_EXPERTS * EXPERT_ROWS
        nb = n_rows // EXPERT_ROWS
        n_used = (pend[-1:] // EXPERT_ROWS).astype(jnp.int32)
        block_row0 = jnp.arange(nb, dtype=jnp.int32) * EXPERT_ROWS
        block_e = jnp.minimum(
            jnp.sum((pend[None, :] <= block_row0[:, None]).astype(jnp.int32), axis=1),
            N_EXPERTS - 1)

        dest2 = _dest(pstart, row, _tile(T, 4096))[:2]
        tm_d = _tile(T, 1024)
        tm_f = _tile(S, 512)
        buf = _dispatch(pstart, pend, n_used, _tile_major(dest2, tm_d),
                        h2.reshape(T // SUBLANES, SUBLANES, D), n_rows, tm_d)
        yb = _experts(block_e, n_used, buf.reshape(n_rows, D), w1_bf, w3_bf, w2_bf)
        out = _final(_tile_major(dest2, tm_f), col.reshape(T // SUBLANES, SUBLANES, LANES),
                     x1.reshape(T // SUBLANES, SUBLANES, D), mod3, final_gain[None, :],
                     yb.reshape(n_rows // SUBLANES, SUBLANES, D), S, tm_f)
    return out.reshape(B, S, D)
```

```python
import functools

import jax
import jax.numpy as jnp
from jax import lax
from jax.experimental import pallas as pl
from jax.experimental.pallas import tpu as pltpu

F32 = jnp.float32
BF16 = jnp.bfloat16

EPS = 1e-6
ROPE_BASE = 10000.0
POOL_WINDOWS = (2, 4, 8, 16)
POOL_HALO = 32
RET_HEADS = 8
RET_CHUNK = 128
N_GROUPS = 4
EXPERTS_PER_GROUP = 8
N_EXPERTS = N_GROUPS * EXPERTS_PER_GROUP
N_ADA = 6
LANES = 128
ROUTE_LANE0 = N_GROUPS
EXPERT_ROWS = 256
INPROJ_TN = 1536
SUBLANES = 8
ROW_DMA_UNROLL = 2
VMEM_LIMIT = 56 * 1024 * 1024


def _cparams(sem):
    return pltpu.CompilerParams(dimension_semantics=sem, vmem_limit_bytes=VMEM_LIMIT)


def _const_spec(shape):
    n = len(shape)
    return pl.BlockSpec(shape, lambda *_: (0,) * n, pipeline_mode=pl.Buffered(1))


def _ada_kernel(cb_ref, w_ref, b_ref, o_ref, cact_ref):
    cb = cb_ref[...]
    cact_ref[...] = cb * jax.nn.sigmoid(cb)
    tn = w_ref.shape[1]
    for b in range(cb_ref.shape[0]):
        for j in range(tn // LANES):
            sl = slice(j * LANES, (j + 1) * LANES)
            prod = w_ref[:, sl] * cact_ref[b]
            o_ref[b:b + 1, sl] = jnp.sum(prod, axis=0, keepdims=True) + b_ref[:, sl]


def _ada(c, w_ada, b_ada):
    B, D = c.shape
    N = w_ada.shape[1]
    tn = 1024
    cb = jnp.broadcast_to(c[:, :, None], (B, D, LANES))
    return pl.pallas_call(
        _ada_kernel,
        out_shape=jax.ShapeDtypeStruct((B, N), F32),
        grid=(N // tn,),
        in_specs=[
            pl.BlockSpec((B, D, LANES), lambda j: (0, 0, 0)),
            pl.BlockSpec((D, tn), lambda j: (0, j)),
            pl.BlockSpec((1, tn), lambda j: (0, j)),
        ],
        out_specs=pl.BlockSpec((B, tn), lambda j: (0, j)),
        scratch_shapes=[pltpu.VMEM((B, D, LANES), F32)],
        compiler_params=_cparams(("arbitrary",)),
        name="ada",
    )(cb, w_ada, b_ada)


def _norm_mod_rows(x, gain, shift, scale):
    ms = jnp.mean(x * x, axis=-1, keepdims=True)
    y = (x * lax.rsqrt(ms + EPS)) * gain
    return y * (1.0 + scale) + shift


SIDE_CAST_BLOCK_BYTES = 2 * 1024 * 1024
SIDE_CAST_MAX_BLOCK_BYTES = 4 * 1024 * 1024


def _side_cast_rows(n_rows, n_cols, n_steps):
    rows = SIDE_CAST_BLOCK_BYTES // (4 * n_cols)
    while rows * 4 * n_cols <= SIDE_CAST_MAX_BLOCK_BYTES:
        if n_rows % rows == 0 and n_rows // rows <= n_steps:
            return rows
        rows *= 2
    return None


def _side_specs(side, n_steps, step_of):
    rows = [_side_cast_rows(a.shape[0], a.shape[1], n_steps) for a in side]
    if any(r is None for r in rows):
        return None
    specs = []
    for a, r in zip(side, rows):
        nb = a.shape[0] // r
        specs.append(pl.BlockSpec(
            (r, a.shape[1]), lambda *g, nb=nb: (jnp.minimum(step_of(*g), nb - 1), 0)))
    return specs


def _cast_side(side_in, side_out):
    for src_ref, dst_ref in zip(side_in, side_out):
        dst_ref[...] = src_ref[...].astype(BF16)


def _inproj_kernel(x_ref, mod_ref, g_ref, w_ref, *rest, n_side):
    side_in = rest[:n_side]
    o_ref = rest[n_side]
    side_out = rest[n_side + 1:2 * n_side + 1]
    h_ref, r_ref = rest[2 * n_side + 1:]
    tm, d = x_ref.shape
    rc = 32

    @pl.when(pl.program_id(1) == 0)
    def _():
        shift = mod_ref[0, 0:1, :]
        mult = g_ref[...] * (1.0 + mod_ref[0, 1:2, :])

        def stats(c, carry):
            rows = pl.ds(pl.multiple_of(c * rc, rc), rc)
            x = x_ref[rows, :]
            r = lax.rsqrt(jnp.mean(x * x, axis=-1, keepdims=True) + EPS)
            r_ref[rows, :] = jnp.broadcast_to(r, (rc, LANES))
            return carry

        lax.fori_loop(0, tm // rc, stats, 0, unroll=4)

        def apply(c, carry):
            rows = pl.ds(pl.multiple_of(c * rc, rc), rc)
            r = r_ref[rows, :]
            for j in range(d // LANES):
                cols = slice(j * LANES, (j + 1) * LANES)
                h_ref[rows, cols] = ((x_ref[rows, cols] * r) * mult[:, cols]
                                     + shift[:, cols]).astype(BF16)
            return carry

        lax.fori_loop(0, tm // rc, apply, 0)

    _cast_side(side_in, side_out)
    o_ref[...] = jnp.dot(h_ref[...], w_ref[...], preferred_element_type=F32).astype(o_ref.dtype)


def _inproj(x2d, mod3, gain, w_in_bf, side, seq, tm, tn):
    T, D = x2d.shape
    N = w_in_bf.shape[1]
    tps = seq // tm
    nj = N // tn
    side_specs = _side_specs(side, (T // tm) * nj, lambda i, j: i * nj + j)
    if side_specs is None:
        out = _inproj(x2d, mod3, gain, w_in_bf, [], seq, tm, tn)
        return out[0], [a.astype(BF16) for a in side]
    outs = pl.pallas_call(
        functools.partial(_inproj_kernel, n_side=len(side)),
        out_shape=[jax.ShapeDtypeStruct((T, N), BF16)]
        + [jax.ShapeDtypeStruct(a.shape, BF16) for a in side],
        grid=(T // tm, nj),
        in_specs=[
            pl.BlockSpec((tm, D), lambda i, j: (i, 0)),
            pl.BlockSpec((1, N_ADA, D), lambda i, j: (i // tps, 0, 0)),
            pl.BlockSpec((1, D), lambda i, j: (0, 0)),
            pl.BlockSpec((D, tn), lambda i, j: (0, j)),
        ] + side_specs,
        out_specs=[pl.BlockSpec((tm, tn), lambda i, j: (i, j))] + side_specs,
        scratch_shapes=[pltpu.VMEM((tm, D), BF16), pltpu.VMEM((tm, LANES), F32)],
        compiler_params=_cparams(("arbitrary", "arbitrary")),
        name="inproj",
    )(x2d, mod3, gain, w_in_bf, *side)
    return outs[0], list(outs[1:])


def _ret_kernel(pos_ref, q_ref, k_ref, v_ref, rg_ref, invf_ref, sign_ref, mask_ref,
                qd_ref, kd_ref, cd_ref, *rest, k_scale, n_side):
    side_in = rest[:n_side]
    o_ref = rest[n_side]
    side_out = rest[n_side + 1:2 * n_side + 1]
    state_ref = rest[2 * n_side + 1]
    C = RET_CHUNK
    hc = C // 2
    nseq = q_ref.shape[0]

    @pl.when(pl.program_id(1) == 0)
    def _():
        state_ref[...] = jnp.zeros_like(state_ref)

    _cast_side(side_in, side_out)

    lo_lanes = lax.broadcasted_iota(jnp.int32, (hc, C), 1) < hc

    def spread(t):
        sw = pltpu.roll(t, hc, 1)
        return jnp.concatenate([jnp.where(lo_lanes, t, sw), jnp.where(lo_lanes, sw, t)], axis=0)

    for c in range(q_ref.shape[1] // C):
        rows = slice(c * C, (c + 1) * C)
        trig = []
        for b in range(nseq):
            pos2 = jnp.where(lo_lanes, pos_ref[b, c * C:c * C + hc, :],
                             pos_ref[b, c * C + hc:(c + 1) * C, :])
            ang = pos2 * invf_ref[...]
            trig.append((spread(jnp.cos(ang)), spread(jnp.sin(ang)) * sign_ref[...]))

        for h in range(RET_HEADS):
            cols = slice(h * C, (h + 1) * C)
            for b in range(nseq):
                cosv, sinv = trig[b]
                q = q_ref[b, rows, cols].astype(F32)
                k = k_ref[b, rows, cols].astype(F32)
                qr = q * cosv + pltpu.roll(q, hc, 1) * sinv
                kr = (k * cosv + pltpu.roll(k, hc, 1) * sinv) * k_scale
                v = v_ref[b, rows, cols]
                s = lax.dot_general(qr.astype(BF16), kr.astype(BF16), (((1,), (1,)), ((), ())),
                                    preferred_element_type=F32) * mask_ref[h]
                st = state_ref[b, h]
                lhs = jnp.concatenate([s.astype(BF16), (qr * qd_ref[h]).astype(BF16)], axis=1)
                rhs = jnp.concatenate([v, st.astype(BF16)], axis=0)
                o = jnp.dot(lhs, rhs, preferred_element_type=F32)
                kv = lax.dot_general((kr * kd_ref[h]).astype(BF16), v, (((0,), (0,)), ((), ())),
                                     preferred_element_type=F32)
                state_ref[b, h] = st * cd_ref[h] + kv
                o = o * lax.rsqrt(jnp.mean(o * o, axis=-1, keepdims=True) + EPS)
                g = rg_ref[b, rows, cols].astype(F32)
                o_ref[b, rows, cols] = (o * (g * jax.nn.sigmoid(g))).astype(o_ref.dtype)


RET_SEQS_PER_STEP = 2
RET_CHUNKS_PER_STEP = 2


def _retention(pos3, proj3, side):
    batch, seq, _ = proj3.shape
    H, C = RET_HEADS, RET_CHUNK
    W = H * C
    nseq = RET_SEQS_PER_STEP if batch % RET_SEQS_PER_STEP == 0 else 1
    nchunk = RET_CHUNKS_PER_STEP if (seq // C) % RET_CHUNKS_PER_STEP == 0 else 1
    ct = nchunk * C
    spb = seq // ct
    half = C // 2
    inv_freq = ROPE_BASE ** (-jnp.arange(half, dtype=F32) / half)
    invf2 = jnp.concatenate([inv_freq, inv_freq])[None, :]
    sign = jnp.concatenate([-jnp.ones((half,), F32), jnp.ones((half,), F32)])[None, :]
    log_gamma = jnp.log1p(-jnp.exp2(-5.0 - jnp.arange(H, dtype=F32)))
    idx = jnp.arange(C, dtype=F32)
    diff = idx[:, None] - idx[None, :]
    mask = jnp.where(diff >= 0, jnp.exp(log_gamma[:, None, None] * jnp.maximum(diff, 0.0)), 0.0)
    q_decay = jnp.exp(log_gamma[:, None] * (idx + 1.0))
    k_decay = jnp.exp(log_gamma[:, None] * (C - 1.0 - idx))
    chunk_decay = jnp.exp(log_gamma * C)
    qd = jnp.broadcast_to(q_decay[:, :, None], (H, C, C))
    kd = jnp.broadcast_to(k_decay[:, :, None], (H, C, C))
    cd = jnp.broadcast_to(chunk_decay[:, None, None], (H, 1, C))

    def tok(col):
        return pl.BlockSpec((nseq, ct, W), lambda b, n, col=col: (b, n, col))

    side_specs = _side_specs(side, (batch // nseq) * spb, lambda b, n: b * spb + n)
    if side_specs is None:
        out = _retention(pos3, proj3, [])
        return out[0], [a.astype(BF16) for a in side]
    outs = pl.pallas_call(
        functools.partial(_ret_kernel, k_scale=float(C) ** -0.5, n_side=len(side)),
        out_shape=[jax.ShapeDtypeStruct((batch, seq, W), BF16)]
        + [jax.ShapeDtypeStruct(a.shape, BF16) for a in side],
        grid=(batch // nseq, spb),
        in_specs=[
            pl.BlockSpec((nseq, ct, 1), lambda b, n: (b, n, 0)),
            tok(1), tok(2), tok(3), tok(4),
            pl.BlockSpec((1, C), lambda b, n: (0, 0)),
            pl.BlockSpec((1, C), lambda b, n: (0, 0)),
            pl.BlockSpec((H, C, C), lambda b, n: (0, 0, 0)),
            pl.BlockSpec((H, C, C), lambda b, n: (0, 0, 0)),
            pl.BlockSpec((H, C, C), lambda b, n: (0, 0, 0)),
            pl.BlockSpec((H, 1, C), lambda b, n: (0, 0, 0)),
        ] + side_specs,
        out_specs=[pl.BlockSpec((nseq, ct, W), lambda b, n: (b, n, 0))] + side_specs,
        scratch_shapes=[pltpu.VMEM((nseq, H, C, C), F32)],
        compiler_params=_cparams(("arbitrary", "arbitrary")),
        name="ret",
    )(pos3, proj3, proj3, proj3, proj3, invf2, sign, mask, qd, kd, cd, *side)
    return outs[0], list(outs[1:])


def _mix_kernel(a_ref, halo_ref, r_ref, ga0_ref, ga1_ref, gb0_ref, gb1_ref, x_ref, mod_ref,
                wpool_ref, pscale_ref, wbp_ref, wbr_ref, wout_ref, g2_ref,
                wrc_ref, brt_ref, *rest, tiles_per_seq, n_side):
    side_in = rest[:n_side]
    x1_ref, h2_ref, lg_ref = rest[n_side:n_side + 3]
    side_out = rest[n_side + 3:2 * n_side + 3]
    ext_ref, lvl_ref, pm_ref, mg_ref, yr_ref = rest[2 * n_side + 3:]
    _cast_side(side_in, side_out)
    tm = a_ref.shape[0]
    pw = a_ref.shape[1]
    gd = pw // len(POOL_WINDOWS)
    i = pl.program_id(0)
    it = i % tiles_per_seq
    yr_ref[...] = jnp.dot(r_ref[...], wbr_ref[...], preferred_element_type=F32)
    halo = halo_ref[...].astype(F32)
    ext_ref[0:POOL_HALO, :] = jnp.where(it == 0, 0.0, halo)
    ext_ref[POOL_HALO:, :] = a_ref[...].astype(F32)
    tpos = it * tm + lax.broadcasted_iota(jnp.int32, (tm, 1), 0)
    top = POOL_HALO + tm
    for g, w in enumerate(POOL_WINDOWS):
        cols = slice(g * gd, (g + 1) * gd)
        cur = ext_ref[POOL_HALO:top, cols]
        n_levels = w.bit_length() - 1
        prev_ref, prev_cols = ext_ref, cols
        for k in range(1, n_levels + 1):
            lo = POOL_HALO if k == n_levels else SUBLANES * k
            sh = 1 << (k - 1)
            s = prev_ref[lo:top, prev_cols] + prev_ref[lo - sh:top - sh, prev_cols]
            if k < n_levels:
                prev_ref, prev_cols = lvl_ref.at[k % 2], slice(None)
                prev_ref[lo:top, :] = s
        cnt = jnp.minimum(tpos + 1, w).astype(F32)
        pooled = s / cnt - cur
        pm = jnp.dot(pooled.astype(BF16), wpool_ref[g], preferred_element_type=F32)
        pm_ref[:, cols] = (pm * pscale_ref[:, cols]).astype(BF16)

    y_pool = jnp.dot(pm_ref[...], wbp_ref[...], preferred_element_type=F32)
    half = y_pool.shape[1] // 2
    for hh, (ga_ref, gb_ref) in enumerate(((ga0_ref, gb0_ref), (ga1_ref, gb1_ref))):
        cols = slice(hh * half, (hh + 1) * half)
        ga = jax.nn.sigmoid(ga_ref[...].astype(F32))
        gb = jax.nn.sigmoid(gb_ref[...].astype(F32))
        mg_ref[:, cols] = (ga * y_pool[:, cols] + gb * yr_ref[:, cols]).astype(BF16)
    z = jnp.dot(mg_ref[...], wout_ref[...], preferred_element_type=F32)
    gate1 = mod_ref[0, 2:3, :]
    x1 = x_ref[...] + gate1 * z
    x1_ref[...] = x1
    h2 = _norm_mod_rows(x1, g2_ref[...], mod_ref[0, 3:4, :], mod_ref[0, 4:5, :])
    h2_ref[...] = h2
    h_hi = h2.astype(BF16)
    h_lo = (h2 - h_hi.astype(F32)).astype(BF16)
    parts = jnp.dot(jnp.concatenate([h_hi, h_lo], axis=0), wrc_ref[...],
                    preferred_element_type=F32)
    lg = (parts[:tm, :LANES] + parts[tm:, :LANES]) + (parts[:tm, LANES:] + parts[tm:, LANES:])
    lg_ref[...] = lg + brt_ref[...]


def _mix(proj, r, x2d, mod3, wpool_bf, pscale, wbp_bf, wbr_bf, wout_bf, g2, wr_cat, brt,
         side, seq, tm):
    T, D = x2d.shape
    PW = wbp_bf.shape[0]
    tps = seq // tm
    hb = tm // POOL_HALO

    def tok(col):
        return pl.BlockSpec((tm, PW), lambda i, col=col: (i, col))

    side_specs = _side_specs(side, T // tm, lambda i: i)
    if side_specs is None:
        outs = _mix(proj, r, x2d, mod3, wpool_bf, pscale, wbp_bf, wbr_bf, wout_bf, g2, wr_cat, brt,
                    [], seq, tm)
        return outs[:3] + [a.astype(BF16) for a in side]
    kern = functools.partial(_mix_kernel, tiles_per_seq=tps, n_side=len(side))
    outs = pl.pallas_call(
        kern,
        out_shape=[jax.ShapeDtypeStruct((T, D), F32),
                   jax.ShapeDtypeStruct((T, D), F32),
                   jax.ShapeDtypeStruct((T, LANES), F32)]
        + [jax.ShapeDtypeStruct(a.shape, BF16) for a in side],
        grid=(T // tm,),
        in_specs=[
            tok(0),
            pl.BlockSpec((POOL_HALO, PW), lambda i: (jnp.maximum(i * hb - 1, 0), 0)),
            pl.BlockSpec((tm, PW), lambda i: (i, 0)),
            tok(5), tok(6), tok(7), tok(8),
            pl.BlockSpec((tm, D), lambda i: (i, 0)),
            pl.BlockSpec((1, N_ADA, D), lambda i: (i // tps, 0, 0)),
            _const_spec(wpool_bf.shape),
            _const_spec(pscale.shape),
            _const_spec(wbp_bf.shape),
            _const_spec(wbr_bf.shape),
            _const_spec(wout_bf.shape),
            _const_spec(g2.shape),
            _const_spec(wr_cat.shape),
            _const_spec(brt.shape),
        ] + side_specs,
        out_specs=[pl.BlockSpec((tm, D), lambda i: (i, 0)),
                   pl.BlockSpec((tm, D), lambda i: (i, 0)),
                   pl.BlockSpec((tm, LANES), lambda i: (i, 0))] + side_specs,
        scratch_shapes=[pltpu.VMEM((POOL_HALO + tm, PW), F32),
                        pltpu.VMEM((2, POOL_HALO + tm, PW // len(POOL_WINDOWS)), F32),
                        pltpu.VMEM((tm, PW), BF16),
                        pltpu.VMEM((tm, D), BF16),
                        pltpu.VMEM((tm, D), F32)],
        compiler_params=_cparams(("arbitrary",)),
        name="mix",
    )(proj, proj, r, proj, proj, proj, proj, x2d, mod3, wpool_bf, pscale, wbp_bf, wbr_bf,
      wout_bf, g2, wr_cat, brt, *side)
    return list(outs)


def _route_kernel(lg_ref, tri_ref, col_ref, row_ref, cnt_ref, carry_ref):
    tm = lg_ref.shape[0]

    @pl.when(pl.program_id(0) == 0)
    def _():
        carry_ref[...] = jnp.zeros_like(carry_ref)

    L = lg_ref[...]
    lane = lax.broadcasted_iota(jnp.int32, (tm, LANES), 1).astype(F32)
    neg = -jnp.inf
    none = float(LANES)
    is_g = lane < N_GROUPS
    gl = jnp.where(is_g, L, neg)
    gmax = jnp.max(gl, axis=1, keepdims=True)
    grp = jnp.min(jnp.where(gl == gmax, lane, none), axis=1, keepdims=True)
    gsum = jnp.sum(jnp.where(is_g, jnp.exp(gl - gmax), 0.0), axis=1, keepdims=True)
    p_grp = 1.0 / gsum
    lo = ROUTE_LANE0 + grp * EXPERTS_PER_GROUP
    el = jnp.where((lane >= lo) & (lane < lo + EXPERTS_PER_GROUP), L, neg)
    v1 = jnp.max(el, axis=1, keepdims=True)
    i1 = jnp.min(jnp.where(el == v1, lane, none), axis=1, keepdims=True)
    el2 = jnp.where(lane == i1, neg, el)
    v2 = jnp.max(el2, axis=1, keepdims=True)
    i2 = jnp.min(jnp.where(el2 == v2, lane, none), axis=1, keepdims=True)
    e = jnp.exp(v2 - v1)
    w1 = p_grp / (1.0 + e)
    w2 = p_grp * e / (1.0 + e)
    sel1 = lane == i1
    sel2 = lane == i2
    onehot = jnp.where(sel1 | sel2, 1.0, 0.0)
    cum = jnp.dot(tri_ref[...], onehot.astype(BF16), preferred_element_type=F32)
    cum = cum + carry_ref[0:1, :]
    r1 = jnp.sum(jnp.where(sel1, cum, 0.0), axis=1, keepdims=True)
    r2 = jnp.sum(jnp.where(sel2, cum, 0.0), axis=1, keepdims=True)
    carry_ref[...] = carry_ref[...] + jnp.sum(onehot, axis=0, keepdims=True)
    cnt_ref[...] = carry_ref[...]
    slab = jnp.where(lane == 0, i1 - ROUTE_LANE0, 0.0)
    slab = jnp.where(lane == 1, i2 - ROUTE_LANE0, slab)
    slab = jnp.where(lane == 2, r1, slab)
    slab = jnp.where(lane == 3, r2, slab)
    slab = jnp.where(lane == 4, w1, slab)
    slab = jnp.where(lane == 5, w2, slab)
    col_ref[...] = slab
    row_ref[...] = slab.T[0:8, :]


def _route(logits, tm):
    T = logits.shape[0]
    tri = jnp.tril(jnp.ones((tm, tm), F32), -1).astype(BF16)
    return pl.pallas_call(
        _route_kernel,
        out_shape=(jax.ShapeDtypeStruct((T, LANES), F32),
                   jax.ShapeDtypeStruct((8, T), F32),
                   jax.ShapeDtypeStruct((8, LANES), F32)),
        grid=(T // tm,),
        in_specs=[pl.BlockSpec((tm, LANES), lambda i: (i, 0)),
                  pl.BlockSpec((tm, tm), lambda i: (0, 0))],
        out_specs=(pl.BlockSpec((tm, LANES), lambda i: (i, 0)),
                   pl.BlockSpec((8, tm), lambda i: (0, i)),
                   pl.BlockSpec((8, LANES), lambda i: (0, 0))),
        scratch_shapes=[pltpu.VMEM((8, LANES), F32)],
        compiler_params=_cparams(("arbitrary",)),
        name="route",
    )(logits, tri)


def _dest_kernel(pstart_ref, row_ref, o_ref):
    e = row_ref[0:2, :].astype(jnp.int32)
    r = row_ref[2:4, :].astype(jnp.int32)
    base = jnp.zeros_like(e)
    for k in range(N_EXPERTS):
        base = jnp.where(e == k, pstart_ref[k], base)
    o_ref[...] = jnp.zeros_like(o_ref)
    o_ref[0:2, :] = base + r


def _dest(pstart, row, tn):
    T = row.shape[1]
    grid_spec = pltpu.PrefetchScalarGridSpec(
        num_scalar_prefetch=1,
        grid=(T // tn,),
        in_specs=[pl.BlockSpec((SUBLANES, tn), lambda i, *_: (0, i))],
        out_specs=pl.BlockSpec((SUBLANES, tn), lambda i, *_: (0, i)),
    )
    return pl.pallas_call(
        _dest_kernel,
        out_shape=jax.ShapeDtypeStruct((SUBLANES, T), jnp.int32),
        grid_spec=grid_spec,
        compiler_params=_cparams(("arbitrary",)),
        name="dest",
    )(pstart, row)


def _tile_major(dest2, tm):
    T = dest2.shape[1]
    return dest2.reshape(2, T // tm, tm).transpose(1, 0, 2).reshape(-1)


def _row_ref(ref, row):
    group = lax.shift_right_logical(row, SUBLANES.bit_length() - 1)
    return ref.at[group, pl.ds(jnp.bitwise_and(row, SUBLANES - 1), 1)]


def _dispatch_kernel(pstart_ref, pend_ref, nu_ref, dest_ref, h2_ref, buf_ref, zero_ref, sem):
    ng = h2_ref.shape[0]
    tm = ng * SUBLANES
    blk = zero_ref.shape[0]
    nb = buf_ref.shape[0] // blk

    @pl.when(pl.program_id(0) == 0)
    def _():
        zero_ref[...] = jnp.zeros_like(zero_ref)

        def zero_copy(start):
            return pltpu.make_async_copy(zero_ref, buf_ref.at[pl.ds(start, blk)], sem)

        def zbody(e, n_started):
            used = pend_ref[e] > pstart_ref[e]

            @pl.when(used)
            def _():
                zero_copy(lax.shift_right_logical(pend_ref[e], 3) - blk).start()
            return n_started + used.astype(jnp.int32)

        n_tails = lax.fori_loop(0, N_EXPERTS, zbody, 0)

        def tail(j, carry):
            zero_copy(j * blk).start()
            return carry

        lax.fori_loop(nu_ref[0], nb, tail, 0)

        def drain(i, carry):
            zero_copy(0).wait()
            return carry

        lax.fori_loop(0, n_tails + (nb - nu_ref[0]), drain, 0)

    def issue(g, carry):
        for u in range(SUBLANES):
            for k in range(2):
                dest = dest_ref[k * tm + g * SUBLANES + u]
                pltpu.make_async_copy(h2_ref.at[g, pl.ds(u, 1)], _row_ref(buf_ref, dest),
                                      sem).start(priority=k)
        return carry

    lax.fori_loop(0, ng, issue, 0, unroll=ROW_DMA_UNROLL)
    for k in range(2):
        pltpu.make_async_copy(h2_ref, buf_ref.at[pl.ds(0, ng)], sem).wait()


def _dispatch(pstart, pend, n_used, dest_flat, h2, n_rows, tm):
    G, _, D = h2.shape
    ng = tm // SUBLANES
    grid_spec = pltpu.PrefetchScalarGridSpec(
        num_scalar_prefetch=3,
        grid=(G // ng,),
        in_specs=[pl.BlockSpec((2 * tm,), lambda i, *_: (i,), memory_space=pltpu.SMEM),
                  pl.BlockSpec((ng, SUBLANES, D), lambda i, *_: (i, 0, 0))],
        out_specs=pl.BlockSpec(memory_space=pl.ANY),
        scratch_shapes=[pltpu.VMEM((EXPERT_ROWS // SUBLANES, SUBLANES, D), h2.dtype),
                        pltpu.SemaphoreType.DMA(())],
    )
    return pl.pallas_call(
        _dispatch_kernel,
        out_shape=jax.ShapeDtypeStruct((n_rows // SUBLANES, SUBLANES, D), h2.dtype),
        grid_spec=grid_spec,
        compiler_params=_cparams(("arbitrary",)),
        name="dispatch",
    )(pstart, pend, n_used, dest_flat, h2)


def _expert_kernel(be_ref, nu_ref, x_ref, w1_ref, w3_ref, w2_ref, o_ref):
    @pl.when(pl.program_id(0) < nu_ref[0])
    def _():
        x = x_ref[...].astype(BF16)
        a = jnp.dot(x, w1_ref[...], preferred_element_type=F32)
        b = jnp.dot(x, w3_ref[...], preferred_element_type=F32)
        hid = (a * jax.nn.sigmoid(a)) * b
        o_ref[...] = jnp.dot(hid.astype(BF16), w2_ref[...], preferred_element_type=F32)

    @pl.when(pl.program_id(0) >= nu_ref[0])
    def _():
        o_ref[...] = jnp.zeros_like(o_ref)


def _experts(block_e, n_used, buf, w1_bf, w3_bf, w2_bf):
    P, D = buf.shape
    F = w1_bf.shape[2]
    nb = P // EXPERT_ROWS

    def row_map(j, be, nu):
        return (jnp.minimum(j, nu[0] - 1), 0)

    grid_spec = pltpu.PrefetchScalarGridSpec(
        num_scalar_prefetch=2,
        grid=(nb,),
        in_specs=[pl.BlockSpec((EXPERT_ROWS, D), row_map),
                  pl.BlockSpec((None, D, F), lambda j, be, nu: (be[j], 0, 0)),
                  pl.BlockSpec((None, D, F), lambda j, be, nu: (be[j], 0, 0)),
                  pl.BlockSpec((None, F, D), lambda j, be, nu: (be[j], 0, 0))],
        out_specs=pl.BlockSpec((EXPERT_ROWS, D), lambda j, be, nu: (j, 0)),
    )
    return pl.pallas_call(
        _expert_kernel,
        out_shape=jax.ShapeDtypeStruct((P, D), F32),
        grid_spec=grid_spec,
        compiler_params=_cparams(("arbitrary",)),
        name="expert",
    )(block_e, n_used, buf, w1_bf, w3_bf, w2_bf)


def _final_kernel(dcur_ref, dnext_ref, col_ref, x1_ref, mod_ref, fg_ref, yb_ref, o_ref,
                  g_ref, sems):
    ng = x1_ref.shape[0]
    tm = ng * SUBLANES
    i = pl.program_id(0)
    buf = lax.rem(i, 2)

    def issue_tile(d_ref, b):
        def issue(g, carry):
            for u in range(SUBLANES):
                for k in range(2):
                    dest = d_ref[k * tm + g * SUBLANES + u]
                    pltpu.make_async_copy(_row_ref(yb_ref, dest), g_ref.at[b, k, g, pl.ds(u, 1)],
                                          sems.at[b]).start(priority=k)
            return carry

        lax.fori_loop(0, ng, issue, 0, unroll=ROW_DMA_UNROLL)

    @pl.when(i == 0)
    def _():
        issue_tile(dcur_ref, 0)

    @pl.when(i + 1 < pl.num_programs(0))
    def _():
        issue_tile(dnext_ref, 1 - buf)

    for k in range(2):
        pltpu.make_async_copy(yb_ref.at[pl.ds(0, ng)], g_ref.at[buf, k], sems.at[buf]).wait()

    w1 = col_ref[:, :, 4:5]
    w2 = col_ref[:, :, 5:6]
    y = g_ref[buf, 0] * w1 + g_ref[buf, 1] * w2
    x2 = x1_ref[...] + mod_ref[0, 5:6, :] * y
    ms = jnp.mean(x2 * x2, axis=-1, keepdims=True)
    o_ref[...] = (x2 * lax.rsqrt(ms + EPS)) * fg_ref[...]


def _final(dest_flat, col, x1, mod3, fgain, yb, seq, tm):
    G, _, D = x1.shape
    ng = tm // SUBLANES
    n_tiles = G // ng
    tps = seq // tm
    return pl.pallas_call(
        _final_kernel,
        out_shape=jax.ShapeDtypeStruct((G, SUBLANES, D), F32),
        grid=(n_tiles,),
        in_specs=[pl.BlockSpec((2 * tm,), lambda i: (i,), memory_space=pltpu.SMEM),
                  pl.BlockSpec((2 * tm,), lambda i: (jnp.minimum(i + 1, n_tiles - 1),),
                               memory_space=pltpu.SMEM),
                  pl.BlockSpec((ng, SUBLANES, LANES), lambda i: (i, 0, 0)),
                  pl.BlockSpec((ng, SUBLANES, D), lambda i: (i, 0, 0)),
                  pl.BlockSpec((1, N_ADA, D), lambda i: (i // tps, 0, 0)),
                  pl.BlockSpec((1, D), lambda i: (0, 0)),
                  pl.BlockSpec(memory_space=pl.ANY)],
        out_specs=pl.BlockSpec((ng, SUBLANES, D), lambda i: (i, 0, 0)),
        scratch_shapes=[pltpu.VMEM((2, 2, ng, SUBLANES, D), F32),
                        pltpu.SemaphoreType.DMA((2,))],
        compiler_params=_cparams(("arbitrary",)),
        name="final",
    )(dest_flat, dest_flat, col, x1, mod3, fgain, yb)


def _tile(n, pref):
    t = min(n, pref)
    assert n % t == 0, (n, t)
    return t


def kernel(x, c, positions, w_ada, b_ada, norm1_gain, w_in, w_pool, pool_scale, w_branch_pool,
           w_branch_ret, w_out, norm2_gain, w_group, b_group, w_router, b_router, w1, w3, w2,
           final_gain):
    B, S, D = x.shape
    T = B * S
    assert w_ada.shape[0] == 1, "only DEPTH == 1 is supported"
    x2d = x.reshape(T, D)
    posf = positions.astype(F32).reshape(T, 1)
    for l in range(1):
        mod3 = _ada(c, w_ada[l], b_ada[l][None, :]).reshape(B, N_ADA, D)

        n_exp, _, d_exp = w1[l].shape
        n_in = w_in[l].shape[1]
        tn = INPROJ_TN if n_in % INPROJ_TN == 0 else 1024
        proj, (w2_bf,) = _inproj(x2d, mod3, norm1_gain[l][None, :], w_in[l].astype(BF16),
                                 [w2[l].reshape(n_exp * d_exp, D)], S, _tile(S, 1024), tn)
        r, (w1_bf,) = _retention(posf.reshape(B, S, 1), proj.reshape(B, S, -1),
                                 [w1[l].reshape(n_exp * D, d_exp)])
        r = r.reshape(T, -1)
        w1_bf = w1_bf.reshape(n_exp, D, d_exp)
        w2_bf = w2_bf.reshape(n_exp, d_exp, D)

        w_rt = jnp.concatenate(
            [w_group[l], w_router[l],
             jnp.zeros((D, LANES - N_GROUPS - N_EXPERTS), F32)], axis=1)
        wr_hi = w_rt.astype(BF16)
        wr_lo = (w_rt - wr_hi.astype(F32)).astype(BF16)
        wr_cat = jnp.concatenate([wr_hi, wr_lo], axis=1)
        brt = jnp.concatenate(
            [b_group[l], b_router[l], jnp.zeros((LANES - N_GROUPS - N_EXPERTS,), F32)])[None, :]
        x1, h2, logits, w3_bf = _mix(
            proj, r, x2d, mod3, w_pool[l].astype(BF16), pool_scale[l][None, :],
            w_branch_pool[l].astype(BF16), w_branch_ret[l].astype(BF16), w_out[l].astype(BF16),
            norm2_gain[l][None, :], wr_cat, brt, [w3[l].reshape(n_exp * D, d_exp)], S,
            _tile(S, 256))
        w3_bf = w3_bf.reshape(n_exp, D, d_exp)

        col, row, cnt = _route(logits, _tile(T, 512))
        counts = cnt[0, ROUTE_LANE0:ROUTE_LANE0 + N_EXPERTS].astype(jnp.int32)
        padded = (counts + EXPERT_ROWS - 1) // EXPERT_ROWS * EXPERT_ROWS
        pend = jnp.cumsum(padded)
        pstart = pend - padded
        n_rows = 2 * T + N_EXPERTS * EXPERT_ROWS
        nb = n_rows // EXPERT_ROWS
        n_used = (pend[-1:] // EXPERT_ROWS).astype(jnp.int32)
        block_row0 = jnp.arange(nb, dtype=jnp.int32) * EXPERT_ROWS
        block_e = jnp.minimum(
            jnp.sum((pend[None, :] <= block_row0[:, None]).astype(jnp.int32), axis=1),
            N_EXPERTS - 1)

        dest2 = _dest(pstart, row, _tile(T, 4096))[:2]
        tm_d = _tile(T, 2048)
        tm_f = _tile(S, 512)
        buf = _dispatch(pstart, pend, n_used, _tile_major(dest2, tm_d),
                        h2.reshape(T // SUBLANES, SUBLANES, D), n_rows, tm_d)
        yb = _experts(block_e, n_used, buf.reshape(n_rows, D), w1_bf, w3_bf, w2_bf)
        out = _final(_tile_major(dest2, tm_f), col.reshape(T // SUBLANES, SUBLANES, LANES),
                     x1.reshape(T // SUBLANES, SUBLANES, D), mod3, final_gain[None, :],
                     yb.reshape(n_rows // SUBLANES, SUBLANES, D), S, tm_f)
    return out.reshape(B, S, D)
```

```python
import functools

import jax
import jax.numpy as jnp
from jax import lax
from jax.experimental import pallas as pl
from jax.experimental.pallas import tpu as pltpu

F32 = jnp.float32
BF16 = jnp.bfloat16

EPS = 1e-6
ROPE_BASE = 10000.0
POOL_WINDOWS = (2, 4, 8, 16)
POOL_HALO = 32
RET_HEADS = 8
RET_CHUNK = 128
N_GROUPS = 4
EXPERTS_PER_GROUP = 8
N_EXPERTS = N_GROUPS * EXPERTS_PER_GROUP
N_ADA = 6
LANES = 128
ROUTE_LANE0 = N_GROUPS
EXPERT_ROWS = 256
INPROJ_TN = 2304
SUBLANES = 8
ROW_DMA_UNROLL = 2
VMEM_LIMIT = 56 * 1024 * 1024
INPROJ_VMEM_LIMIT = 60 * 1024 * 1024


def _cparams(sem, vmem_limit=VMEM_LIMIT):
    return pltpu.CompilerParams(dimension_semantics=sem, vmem_limit_bytes=vmem_limit)


def _const_spec(shape):
    n = len(shape)
    return pl.BlockSpec(shape, lambda *_: (0,) * n, pipeline_mode=pl.Buffered(1))


def _ada_kernel(cb_ref, w_ref, b_ref, o_ref, cact_ref):
    cb = cb_ref[...]
    cact_ref[...] = cb * jax.nn.sigmoid(cb)
    tn = w_ref.shape[1]
    for b in range(cb_ref.shape[0]):
        for j in range(tn // LANES):
            sl = slice(j * LANES, (j + 1) * LANES)
            prod = w_ref[:, sl] * cact_ref[b]
            o_ref[b:b + 1, sl] = jnp.sum(prod, axis=0, keepdims=True) + b_ref[:, sl]


def _ada(c, w_ada, b_ada):
    B, D = c.shape
    N = w_ada.shape[1]
    tn = 1024
    cb = jnp.broadcast_to(c[:, :, None], (B, D, LANES))
    return pl.pallas_call(
        _ada_kernel,
        out_shape=jax.ShapeDtypeStruct((B, N), F32),
        grid=(N // tn,),
        in_specs=[
            pl.BlockSpec((B, D, LANES), lambda j: (0, 0, 0)),
            pl.BlockSpec((D, tn), lambda j: (0, j)),
            pl.BlockSpec((1, tn), lambda j: (0, j)),
        ],
        out_specs=pl.BlockSpec((B, tn), lambda j: (0, j)),
        scratch_shapes=[pltpu.VMEM((B, D, LANES), F32)],
        compiler_params=_cparams(("arbitrary",)),
        name="ada",
    )(cb, w_ada, b_ada)


def _norm_mod_rows(x, gain, shift, scale):
    ms = jnp.mean(x * x, axis=-1, keepdims=True)
    y = (x * lax.rsqrt(ms + EPS)) * gain
    return y * (1.0 + scale) + shift


SIDE_CAST_BLOCK_BYTES = 2 * 1024 * 1024
SIDE_CAST_MAX_BLOCK_BYTES = 4 * 1024 * 1024


def _side_cast_rows(n_rows, n_cols, n_steps):
    rows = SIDE_CAST_BLOCK_BYTES // (4 * n_cols)
    while rows * 4 * n_cols <= SIDE_CAST_MAX_BLOCK_BYTES:
        if n_rows % rows == 0 and n_rows // rows <= n_steps:
            return rows
        rows *= 2
    return None


def _side_specs(side, n_steps, step_of):
    rows = [_side_cast_rows(a.shape[0], a.shape[1], n_steps) for a in side]
    if any(r is None for r in rows):
        return None
    specs = []
    for a, r in zip(side, rows):
        nb = a.shape[0] // r
        specs.append(pl.BlockSpec(
            (r, a.shape[1]), lambda *g, nb=nb: (jnp.minimum(step_of(*g), nb - 1), 0)))
    return specs


def _cast_side(side_in, side_out):
    for src_ref, dst_ref in zip(side_in, side_out):
        dst_ref[...] = src_ref[...].astype(BF16)


def _inproj_kernel(x_ref, mod_ref, g_ref, w_ref, *rest, n_side):
    side_in = rest[:n_side]
    o_ref = rest[n_side]
    side_out = rest[n_side + 1:2 * n_side + 1]
    h_ref, r_ref = rest[2 * n_side + 1:]
    tm, d = x_ref.shape
    rc = 32

    @pl.when(pl.program_id(1) == 0)
    def _():
        shift = mod_ref[0, 0:1, :]
        mult = g_ref[...] * (1.0 + mod_ref[0, 1:2, :])

        def stats(c, carry):
            rows = pl.ds(pl.multiple_of(c * rc, rc), rc)
            x = x_ref[rows, :]
            r = lax.rsqrt(jnp.mean(x * x, axis=-1, keepdims=True) + EPS)
            r_ref[rows, :] = jnp.broadcast_to(r, (rc, LANES))
            return carry

        lax.fori_loop(0, tm // rc, stats, 0, unroll=4)

        def apply(c, carry):
            rows = pl.ds(pl.multiple_of(c * rc, rc), rc)
            r = r_ref[rows, :]
            for j in range(d // LANES):
                cols = slice(j * LANES, (j + 1) * LANES)
                h_ref[rows, cols] = ((x_ref[rows, cols] * r) * mult[:, cols]
                                     + shift[:, cols]).astype(BF16)
            return carry

        lax.fori_loop(0, tm // rc, apply, 0)

    _cast_side(side_in, side_out)
    o_ref[...] = jnp.dot(h_ref[...], w_ref[...], preferred_element_type=F32).astype(o_ref.dtype)


def _inproj(x2d, mod3, gain, w_in_bf, side, seq, tm, tn):
    T, D = x2d.shape
    N = w_in_bf.shape[1]
    tps = seq // tm
    nj = N // tn
    side_specs = _side_specs(side, (T // tm) * nj, lambda i, j: i * nj + j)
    if side_specs is None:
        out = _inproj(x2d, mod3, gain, w_in_bf, [], seq, tm, tn)
        return out[0], [a.astype(BF16) for a in side]
    outs = pl.pallas_call(
        functools.partial(_inproj_kernel, n_side=len(side)),
        out_shape=[jax.ShapeDtypeStruct((T, N), BF16)]
        + [jax.ShapeDtypeStruct(a.shape, BF16) for a in side],
        grid=(T // tm, nj),
        in_specs=[
            pl.BlockSpec((tm, D), lambda i, j: (i, 0)),
            pl.BlockSpec((1, N_ADA, D), lambda i, j: (i // tps, 0, 0)),
            pl.BlockSpec((1, D), lambda i, j: (0, 0)),
            pl.BlockSpec((D, tn), lambda i, j: (0, j)),
        ] + side_specs,
        out_specs=[pl.BlockSpec((tm, tn), lambda i, j: (i, j))] + side_specs,
        scratch_shapes=[pltpu.VMEM((tm, D), BF16), pltpu.VMEM((tm, LANES), F32)],
        compiler_params=_cparams(("arbitrary", "arbitrary"), INPROJ_VMEM_LIMIT),
        name="inproj",
    )(x2d, mod3, gain, w_in_bf, *side)
    return outs[0], list(outs[1:])


def _ret_kernel(pos_ref, q_ref, k_ref, v_ref, rg_ref, invf_ref, sign_ref, mask_ref,
                qd_ref, kd_ref, cd_ref, *rest, k_scale, n_side):
    side_in = rest[:n_side]
    o_ref = rest[n_side]
    side_out = rest[n_side + 1:2 * n_side + 1]
    state_ref = rest[2 * n_side + 1]
    C = RET_CHUNK
    hc = C // 2
    nseq = q_ref.shape[0]

    @pl.when(pl.program_id(1) == 0)
    def _():
        state_ref[...] = jnp.zeros_like(state_ref)

    _cast_side(side_in, side_out)

    lo_lanes = lax.broadcasted_iota(jnp.int32, (hc, C), 1) < hc

    def spread(t):
        sw = pltpu.roll(t, hc, 1)
        return jnp.concatenate([jnp.where(lo_lanes, t, sw), jnp.where(lo_lanes, sw, t)], axis=0)

    for c in range(q_ref.shape[1] // C):
        rows = slice(c * C, (c + 1) * C)
        trig = []
        for b in range(nseq):
            pos2 = jnp.where(lo_lanes, pos_ref[b, c * C:c * C + hc, :],
                             pos_ref[b, c * C + hc:(c + 1) * C, :])
            ang = pos2 * invf_ref[...]
            trig.append((spread(jnp.cos(ang)), spread(jnp.sin(ang)) * sign_ref[...]))

        for h in range(RET_HEADS):
            cols = slice(h * C, (h + 1) * C)
            for b in range(nseq):
                cosv, sinv = trig[b]
                q = q_ref[b, rows, cols].astype(F32)
                k = k_ref[b, rows, cols].astype(F32)
                qr = q * cosv + pltpu.roll(q, hc, 1) * sinv
                kr = (k * cosv + pltpu.roll(k, hc, 1) * sinv) * k_scale
                v = v_ref[b, rows, cols]
                s = lax.dot_general(qr.astype(BF16), kr.astype(BF16), (((1,), (1,)), ((), ())),
                                    preferred_element_type=F32) * mask_ref[h]
                st = state_ref[b, h]
                lhs = jnp.concatenate([s.astype(BF16), (qr * qd_ref[h]).astype(BF16)], axis=1)
                rhs = jnp.concatenate([v, st.astype(BF16)], axis=0)
                o = jnp.dot(lhs, rhs, preferred_element_type=F32)
                kv = lax.dot_general((kr * kd_ref[h]).astype(BF16), v, (((0,), (0,)), ((), ())),
                                     preferred_element_type=F32)
                state_ref[b, h] = st * cd_ref[h] + kv
                o = o * lax.rsqrt(jnp.mean(o * o, axis=-1, keepdims=True) + EPS)
                g = rg_ref[b, rows, cols].astype(F32)
                o_ref[b, rows, cols] = (o * (g * jax.nn.sigmoid(g))).astype(o_ref.dtype)


RET_SEQS_PER_STEP = 2
RET_CHUNKS_PER_STEP = 2


def _retention(pos3, proj3, side):
    batch, seq, _ = proj3.shape
    H, C = RET_HEADS, RET_CHUNK
    W = H * C
    nseq = RET_SEQS_PER_STEP if batch % RET_SEQS_PER_STEP == 0 else 1
    nchunk = RET_CHUNKS_PER_STEP if (seq // C) % RET_CHUNKS_PER_STEP == 0 else 1
    ct = nchunk * C
    spb = seq // ct
    half = C // 2
    inv_freq = ROPE_BASE ** (-jnp.arange(half, dtype=F32) / half)
    invf2 = jnp.concatenate([inv_freq, inv_freq])[None, :]
    sign = jnp.concatenate([-jnp.ones((half,), F32), jnp.ones((half,), F32)])[None, :]
    log_gamma = jnp.log1p(-jnp.exp2(-5.0 - jnp.arange(H, dtype=F32)))
    idx = jnp.arange(C, dtype=F32)
    diff = idx[:, None] - idx[None, :]
    mask = jnp.where(diff >= 0, jnp.exp(log_gamma[:, None, None] * jnp.maximum(diff, 0.0)), 0.0)
    q_decay = jnp.exp(log_gamma[:, None] * (idx + 1.0))
    k_decay = jnp.exp(log_gamma[:, None] * (C - 1.0 - idx))
    chunk_decay = jnp.exp(log_gamma * C)
    qd = jnp.broadcast_to(q_decay[:, :, None], (H, C, C))
    kd = jnp.broadcast_to(k_decay[:, :, None], (H, C, C))
    cd = jnp.broadcast_to(chunk_decay[:, None, None], (H, 1, C))

    def tok(col):
        return pl.BlockSpec((nseq, ct, W), lambda b, n, col=col: (b, n, col))

    side_specs = _side_specs(side, (batch // nseq) * spb, lambda b, n: b * spb + n)
    if side_specs is None:
        out = _retention(pos3, proj3, [])
        return out[0], [a.astype(BF16) for a in side]
    outs = pl.pallas_call(
        functools.partial(_ret_kernel, k_scale=float(C) ** -0.5, n_side=len(side)),
        out_shape=[jax.ShapeDtypeStruct((batch, seq, W), BF16)]
        + [jax.ShapeDtypeStruct(a.shape, BF16) for a in side],
        grid=(batch // nseq, spb),
        in_specs=[
            pl.BlockSpec((nseq, ct, 1), lambda b, n: (b, n, 0)),
            tok(1), tok(2), tok(3), tok(4),
            pl.BlockSpec((1, C), lambda b, n: (0, 0)),
            pl.BlockSpec((1, C), lambda b, n: (0, 0)),
            pl.BlockSpec((H, C, C), lambda b, n: (0, 0, 0)),
            pl.BlockSpec((H, C, C), lambda b, n: (0, 0, 0)),
            pl.BlockSpec((H, C, C), lambda b, n: (0, 0, 0)),
            pl.BlockSpec((H, 1, C), lambda b, n: (0, 0, 0)),
        ] + side_specs,
        out_specs=[pl.BlockSpec((nseq, ct, W), lambda b, n: (b, n, 0))] + side_specs,
        scratch_shapes=[pltpu.VMEM((nseq, H, C, C), F32)],
        compiler_params=_cparams(("arbitrary", "arbitrary")),
        name="ret",
    )(pos3, proj3, proj3, proj3, proj3, invf2, sign, mask, qd, kd, cd, *side)
    return outs[0], list(outs[1:])


def _mix_kernel(a_ref, halo_ref, r_ref, ga0_ref, ga1_ref, gb0_ref, gb1_ref, x_ref, mod_ref,
                wpool_ref, pscale_ref, wbp_ref, wbr_ref, wout_ref, g2_ref,
                wrc_ref, brt_ref, *rest, tiles_per_seq, n_side):
    side_in = rest[:n_side]
    x1_ref, h2_ref, lg_ref = rest[n_side:n_side + 3]
    side_out = rest[n_side + 3:2 * n_side + 3]
    ext_ref, lvl_ref, pm_ref, mg_ref, yr_ref = rest[2 * n_side + 3:]
    _cast_side(side_in, side_out)
    tm = a_ref.shape[0]
    pw = a_ref.shape[1]
    gd = pw // len(POOL_WINDOWS)
    i = pl.program_id(0)
    it = i % tiles_per_seq
    yr_ref[...] = jnp.dot(r_ref[...], wbr_ref[...], preferred_element_type=F32)
    halo = halo_ref[...].astype(F32)
    ext_ref[0:POOL_HALO, :] = jnp.where(it == 0, 0.0, halo)
    ext_ref[POOL_HALO:, :] = a_ref[...].astype(F32)
    tpos = it * tm + lax.broadcasted_iota(jnp.int32, (tm, 1), 0)
    top = POOL_HALO + tm
    for g, w in enumerate(POOL_WINDOWS):
        cols = slice(g * gd, (g + 1) * gd)
        cur = ext_ref[POOL_HALO:top, cols]
        n_levels = w.bit_length() - 1
        prev_ref, prev_cols = ext_ref, cols
        for k in range(1, n_levels + 1):
            lo = POOL_HALO if k == n_levels else SUBLANES * k
            sh = 1 << (k - 1)
            s = prev_ref[lo:top, prev_cols] + prev_ref[lo - sh:top - sh, prev_cols]
            if k < n_levels:
                prev_ref, prev_cols = lvl_ref.at[k % 2], slice(None)
                prev_ref[lo:top, :] = s
        cnt = jnp.minimum(tpos + 1, w).astype(F32)
        pooled = s / cnt - cur
        pm = jnp.dot(pooled.astype(BF16), wpool_ref[g], preferred_element_type=F32)
        pm_ref[:, cols] = (pm * pscale_ref[:, cols]).astype(BF16)

    y_pool = jnp.dot(pm_ref[...], wbp_ref[...], preferred_element_type=F32)
    half = y_pool.shape[1] // 2
    for hh, (ga_ref, gb_ref) in enumerate(((ga0_ref, gb0_ref), (ga1_ref, gb1_ref))):
        cols = slice(hh * half, (hh + 1) * half)
        ga = jax.nn.sigmoid(ga_ref[...].astype(F32))
        gb = jax.nn.sigmoid(gb_ref[...].astype(F32))
        mg_ref[:, cols] = (ga * y_pool[:, cols] + gb * yr_ref[:, cols]).astype(BF16)
    z = jnp.dot(mg_ref[...], wout_ref[...], preferred_element_type=F32)
    gate1 = mod_ref[0, 2:3, :]
    x1 = x_ref[...] + gate1 * z
    x1_ref[...] = x1
    h2 = _norm_mod_rows(x1, g2_ref[...], mod_ref[0, 3:4, :], mod_ref[0, 4:5, :])
    h2_ref[...] = h2
    h_hi = h2.astype(BF16)
    h_lo = (h2 - h_hi.astype(F32)).astype(BF16)
    parts = jnp.dot(jnp.concatenate([h_hi, h_lo], axis=0), wrc_ref[...],
                    preferred_element_type=F32)
    lg = (parts[:tm, :LANES] + parts[tm:, :LANES]) + (parts[:tm, LANES:] + parts[tm:, LANES:])
    lg_ref[...] = lg + brt_ref[...]


def _mix(proj, r, x2d, mod3, wpool_bf, pscale, wbp_bf, wbr_bf, wout_bf, g2, wr_cat, brt,
         side, seq, tm):
    T, D = x2d.shape
    PW = wbp_bf.shape[0]
    tps = seq // tm
    hb = tm // POOL_HALO

    def tok(col):
        return pl.BlockSpec((tm, PW), lambda i, col=col: (i, col))

    side_specs = _side_specs(side, T // tm, lambda i: i)
    if side_specs is None:
        outs = _mix(proj, r, x2d, mod3, wpool_bf, pscale, wbp_bf, wbr_bf, wout_bf, g2, wr_cat, brt,
                    [], seq, tm)
        return outs[:3] + [a.astype(BF16) for a in side]
    kern = functools.partial(_mix_kernel, tiles_per_seq=tps, n_side=len(side))
    outs = pl.pallas_call(
        kern,
        out_shape=[jax.ShapeDtypeStruct((T, D), F32),
                   jax.ShapeDtypeStruct((T, D), F32),
                   jax.ShapeDtypeStruct((T, LANES), F32)]
        + [jax.ShapeDtypeStruct(a.shape, BF16) for a in side],
        grid=(T // tm,),
        in_specs=[
            tok(0),
            pl.BlockSpec((POOL_HALO, PW), lambda i: (jnp.maximum(i * hb - 1, 0), 0)),
            pl.BlockSpec((tm, PW), lambda i: (i, 0)),
            tok(5), tok(6), tok(7), tok(8),
            pl.BlockSpec((tm, D), lambda i: (i, 0)),
            pl.BlockSpec((1, N_ADA, D), lambda i: (i // tps, 0, 0)),
            _const_spec(wpool_bf.shape),
            _const_spec(pscale.shape),
            _const_spec(wbp_bf.shape),
            _const_spec(wbr_bf.shape),
            _const_spec(wout_bf.shape),
            _const_spec(g2.shape),
            _const_spec(wr_cat.shape),
            _const_spec(brt.shape),
        ] + side_specs,
        out_specs=[pl.BlockSpec((tm, D), lambda i: (i, 0)),
                   pl.BlockSpec((tm, D), lambda i: (i, 0)),
                   pl.BlockSpec((tm, LANES), lambda i: (i, 0))] + side_specs,
        scratch_shapes=[pltpu.VMEM((POOL_HALO + tm, PW), F32),
                        pltpu.VMEM((2, POOL_HALO + tm, PW // len(POOL_WINDOWS)), F32),
                        pltpu.VMEM((tm, PW), BF16),
                        pltpu.VMEM((tm, D), BF16),
                        pltpu.VMEM((tm, D), F32)],
        compiler_params=_cparams(("arbitrary",)),
        name="mix",
    )(proj, proj, r, proj, proj, proj, proj, x2d, mod3, wpool_bf, pscale, wbp_bf, wbr_bf,
      wout_bf, g2, wr_cat, brt, *side)
    return list(outs)


def _route_kernel(lg_ref, tri_ref, col_ref, row_ref, cnt_ref, carry_ref):
    tm = lg_ref.shape[0]

    @pl.when(pl.program_id(0) == 0)
    def _():
        carry_ref[...] = jnp.zeros_like(carry_ref)

    L = lg_ref[...]
    lane = lax.broadcasted_iota(jnp.int32, (tm, LANES), 1).astype(F32)
    neg = -jnp.inf
    none = float(LANES)
    is_g = lane < N_GROUPS
    gl = jnp.where(is_g, L, neg)
    gmax = jnp.max(gl, axis=1, keepdims=True)
    grp = jnp.min(jnp.where(gl == gmax, lane, none), axis=1, keepdims=True)
    gsum = jnp.sum(jnp.where(is_g, jnp.exp(gl - gmax), 0.0), axis=1, keepdims=True)
    p_grp = 1.0 / gsum
    lo = ROUTE_LANE0 + grp * EXPERTS_PER_GROUP
    el = jnp.where((lane >= lo) & (lane < lo + EXPERTS_PER_GROUP), L, neg)
    v1 = jnp.max(el, axis=1, keepdims=True)
    i1 = jnp.min(jnp.where(el == v1, lane, none), axis=1, keepdims=True)
    el2 = jnp.where(lane == i1, neg, el)
    v2 = jnp.max(el2, axis=1, keepdims=True)
    i2 = jnp.min(jnp.where(el2 == v2, lane, none), axis=1, keepdims=True)
    e = jnp.exp(v2 - v1)
    w1 = p_grp / (1.0 + e)
    w2 = p_grp * e / (1.0 + e)
    sel1 = lane == i1
    sel2 = lane == i2
    onehot = jnp.where(sel1 | sel2, 1.0, 0.0)
    cum = jnp.dot(tri_ref[...], onehot.astype(BF16), preferred_element_type=F32)
    cum = cum + carry_ref[0:1, :]
    r1 = jnp.sum(jnp.where(sel1, cum, 0.0), axis=1, keepdims=True)
    r2 = jnp.sum(jnp.where(sel2, cum, 0.0), axis=1, keepdims=True)
    carry_ref[...] = carry_ref[...] + jnp.sum(onehot, axis=0, keepdims=True)
    cnt_ref[...] = carry_ref[...]
    slab = jnp.where(lane == 0, i1 - ROUTE_LANE0, 0.0)
    slab = jnp.where(lane == 1, i2 - ROUTE_LANE0, slab)
    slab = jnp.where(lane == 2, r1, slab)
    slab = jnp.where(lane == 3, r2, slab)
    slab = jnp.where(lane == 4, w1, slab)
    slab = jnp.where(lane == 5, w2, slab)
    col_ref[...] = slab
    row_ref[...] = slab.T[0:8, :]


def _route(logits, tm):
    T = logits.shape[0]
    tri = jnp.tril(jnp.ones((tm, tm), F32), -1).astype(BF16)
    return pl.pallas_call(
        _route_kernel,
        out_shape=(jax.ShapeDtypeStruct((T, LANES), F32),
                   jax.ShapeDtypeStruct((8, T), F32),
                   jax.ShapeDtypeStruct((8, LANES), F32)),
        grid=(T // tm,),
        in_specs=[pl.BlockSpec((tm, LANES), lambda i: (i, 0)),
                  pl.BlockSpec((tm, tm), lambda i: (0, 0))],
        out_specs=(pl.BlockSpec((tm, LANES), lambda i: (i, 0)),
                   pl.BlockSpec((8, tm), lambda i: (0, i)),
                   pl.BlockSpec((8, LANES), lambda i: (0, 0))),
        scratch_shapes=[pltpu.VMEM((8, LANES), F32)],
        compiler_params=_cparams(("arbitrary",)),
        name="route",
    )(logits, tri)


def _dest_kernel(pstart_ref, row_ref, o_ref):
    e = row_ref[0:2, :].astype(jnp.int32)
    r = row_ref[2:4, :].astype(jnp.int32)
    base = jnp.zeros_like(e)
    for k in range(N_EXPERTS):
        base = jnp.where(e == k, pstart_ref[k], base)
    o_ref[...] = jnp.zeros_like(o_ref)
    o_ref[0:2, :] = base + r


def _dest(pstart, row, tn):
    T = row.shape[1]
    grid_spec = pltpu.PrefetchScalarGridSpec(
        num_scalar_prefetch=1,
        grid=(T // tn,),
        in_specs=[pl.BlockSpec((SUBLANES, tn), lambda i, *_: (0, i))],
        out_specs=pl.BlockSpec((SUBLANES, tn), lambda i, *_: (0, i)),
    )
    return pl.pallas_call(
        _dest_kernel,
        out_shape=jax.ShapeDtypeStruct((SUBLANES, T), jnp.int32),
        grid_spec=grid_spec,
        compiler_params=_cparams(("arbitrary",)),
        name="dest",
    )(pstart, row)


def _tile_major(dest2, tm):
    T = dest2.shape[1]
    return dest2.reshape(2, T // tm, tm).transpose(1, 0, 2).reshape(-1)


def _row_ref(ref, row):
    return ref.at[lax.shift_right_logical(row, 3), pl.ds(jnp.bitwise_and(row, SUBLANES - 1), 1)]


def _dispatch_kernel(pstart_ref, pend_ref, nu_ref, dest_ref, h2_ref, buf_ref, zero_ref, sem):
    ng = h2_ref.shape[0]
    tm = ng * SUBLANES
    blk = zero_ref.shape[0]
    nb = buf_ref.shape[0] // blk

    @pl.when(pl.program_id(0) == 0)
    def _():
        zero_ref[...] = jnp.zeros_like(zero_ref)

        def zero_copy(start):
            return pltpu.make_async_copy(zero_ref, buf_ref.at[pl.ds(start, blk)], sem)

        def zbody(e, n_started):
            used = pend_ref[e] > pstart_ref[e]

            @pl.when(used)
            def _():
                zero_copy(lax.shift_right_logical(pend_ref[e], 3) - blk).start()
            return n_started + used.astype(jnp.int32)

        n_tails = lax.fori_loop(0, N_EXPERTS, zbody, 0)

        def tail(j, carry):
            zero_copy(j * blk).start()
            return carry

        lax.fori_loop(nu_ref[0], nb, tail, 0)

        def drain(i, carry):
            zero_copy(0).wait()
            return carry

        lax.fori_loop(0, n_tails + (nb - nu_ref[0]), drain, 0)

    def issue(g, carry):
        for u in range(SUBLANES):
            for k in range(2):
                dest = dest_ref[k * tm + g * SUBLANES + u]
                pltpu.make_async_copy(h2_ref.at[g, pl.ds(u, 1)], _row_ref(buf_ref, dest),
                                      sem).start(priority=k)
        return carry

    lax.fori_loop(0, ng, issue, 0, unroll=ROW_DMA_UNROLL)
    for k in range(2):
        pltpu.make_async_copy(h2_ref, buf_ref.at[pl.ds(0, ng)], sem).wait()


def _dispatch(pstart, pend, n_used, dest_flat, h2, n_rows, tm):
    G, _, D = h2.shape
    ng = tm // SUBLANES
    grid_spec = pltpu.PrefetchScalarGridSpec(
        num_scalar_prefetch=3,
        grid=(G // ng,),
        in_specs=[pl.BlockSpec((2 * tm,), lambda i, *_: (i,), memory_space=pltpu.SMEM),
                  pl.BlockSpec((ng, SUBLANES, D), lambda i, *_: (i, 0, 0))],
        out_specs=pl.BlockSpec(memory_space=pl.ANY),
        scratch_shapes=[pltpu.VMEM((EXPERT_ROWS // SUBLANES, SUBLANES, D), h2.dtype),
                        pltpu.SemaphoreType.DMA(())],
    )
    return pl.pallas_call(
        _dispatch_kernel,
        out_shape=jax.ShapeDtypeStruct((n_rows // SUBLANES, SUBLANES, D), h2.dtype),
        grid_spec=grid_spec,
        compiler_params=_cparams(("arbitrary",)),
        name="dispatch",
    )(pstart, pend, n_used, dest_flat, h2)


def _expert_kernel(be_ref, nu_ref, x_ref, w1_ref, w3_ref, w2_ref, o_ref):
    @pl.when(pl.program_id(0) < nu_ref[0])
    def _():
        x = x_ref[...].astype(BF16)
        a = jnp.dot(x, w1_ref[...], preferred_element_type=F32)
        b = jnp.dot(x, w3_ref[...], preferred_element_type=F32)
        hid = (a * jax.nn.sigmoid(a)) * b
        o_ref[...] = jnp.dot(hid.astype(BF16), w2_ref[...], preferred_element_type=F32)

    @pl.when(pl.program_id(0) >= nu_ref[0])
    def _():
        o_ref[...] = jnp.zeros_like(o_ref)


def _experts(block_e, n_used, buf, w1_bf, w3_bf, w2_bf):
    P, D = buf.shape
    F = w1_bf.shape[2]
    nb = P // EXPERT_ROWS

    def row_map(j, be, nu):
        return (jnp.minimum(j, nu[0] - 1), 0)

    grid_spec = pltpu.PrefetchScalarGridSpec(
        num_scalar_prefetch=2,
        grid=(nb,),
        in_specs=[pl.BlockSpec((EXPERT_ROWS, D), row_map),
                  pl.BlockSpec((None, D, F), lambda j, be, nu: (be[j], 0, 0)),
                  pl.BlockSpec((None, D, F), lambda j, be, nu: (be[j], 0, 0)),
                  pl.BlockSpec((None, F, D), lambda j, be, nu: (be[j], 0, 0))],
        out_specs=pl.BlockSpec((EXPERT_ROWS, D), lambda j, be, nu: (j, 0)),
    )
    return pl.pallas_call(
        _expert_kernel,
        out_shape=jax.ShapeDtypeStruct((P, D), F32),
        grid_spec=grid_spec,
        compiler_params=_cparams(("arbitrary",)),
        name="expert",
    )(block_e, n_used, buf, w1_bf, w3_bf, w2_bf)


def _final_kernel(dcur_ref, dnext_ref, col_ref, x1_ref, mod_ref, fg_ref, yb_ref, o_ref,
                  g_ref, sems):
    ng = x1_ref.shape[0]
    tm = ng * SUBLANES
    i = pl.program_id(0)
    buf = lax.rem(i, 2)

    def issue_tile(d_ref, b):
        def issue(g, carry):
            for u in range(SUBLANES):
                for k in range(2):
                    dest = d_ref[k * tm + g * SUBLANES + u]
                    pltpu.make_async_copy(_row_ref(yb_ref, dest), g_ref.at[b, k, g, pl.ds(u, 1)],
                                          sems.at[b]).start(priority=k)
            return carry

        lax.fori_loop(0, ng, issue, 0, unroll=ROW_DMA_UNROLL)

    @pl.when(i == 0)
    def _():
        issue_tile(dcur_ref, 0)

    @pl.when(i + 1 < pl.num_programs(0))
    def _():
        issue_tile(dnext_ref, 1 - buf)

    for k in range(2):
        pltpu.make_async_copy(yb_ref.at[pl.ds(0, ng)], g_ref.at[buf, k], sems.at[buf]).wait()

    w1 = col_ref[:, :, 4:5]
    w2 = col_ref[:, :, 5:6]
    y = g_ref[buf, 0] * w1 + g_ref[buf, 1] * w2
    x2 = x1_ref[...] + mod_ref[0, 5:6, :] * y
    ms = jnp.mean(x2 * x2, axis=-1, keepdims=True)
    o_ref[...] = (x2 * lax.rsqrt(ms + EPS)) * fg_ref[...]


def _final(dest_flat, col, x1, mod3, fgain, yb, seq, tm):
    G, _, D = x1.shape
    ng = tm // SUBLANES
    n_tiles = G // ng
    tps = seq // tm
    return pl.pallas_call(
        _final_kernel,
        out_shape=jax.ShapeDtypeStruct((G, SUBLANES, D), F32),
        grid=(n_tiles,),
        in_specs=[pl.BlockSpec((2 * tm,), lambda i: (i,), memory_space=pltpu.SMEM),
                  pl.BlockSpec((2 * tm,), lambda i: (jnp.minimum(i + 1, n_tiles - 1),),
                               memory_space=pltpu.SMEM),
                  pl.BlockSpec((ng, SUBLANES, LANES), lambda i: (i, 0, 0)),
                  pl.BlockSpec((ng, SUBLANES, D), lambda i: (i, 0, 0)),
                  pl.BlockSpec((1, N_ADA, D), lambda i: (i // tps, 0, 0)),
                  pl.BlockSpec((1, D), lambda i: (0, 0)),
                  pl.BlockSpec(memory_space=pl.ANY)],
        out_specs=pl.BlockSpec((ng, SUBLANES, D), lambda i: (i, 0, 0)),
        scratch_shapes=[pltpu.VMEM((2, 2, ng, SUBLANES, D), F32),
                        pltpu.SemaphoreType.DMA((2,))],
        compiler_params=_cparams(("arbitrary",)),
        name="final",
    )(dest_flat, dest_flat, col, x1, mod3, fgain, yb)


def _tile(n, pref):
    t = min(n, pref)
    assert n % t == 0, (n, t)
    return t


def kernel(x, c, positions, w_ada, b_ada, norm1_gain, w_in, w_pool, pool_scale, w_branch_pool,
           w_branch_ret, w_out, norm2_gain, w_group, b_group, w_router, b_router, w1, w3, w2,
           final_gain):
    B, S, D = x.shape
    T = B * S
    assert w_ada.shape[0] == 1, "only DEPTH == 1 is supported"
    x2d = x.reshape(T, D)
    posf = positions.astype(F32).reshape(T, 1)
    for l in range(1):
        mod3 = _ada(c, w_ada[l], b_ada[l][None, :]).reshape(B, N_ADA, D)

        n_exp, _, d_exp = w1[l].shape
        n_in = w_in[l].shape[1]
        tn = INPROJ_TN if n_in % INPROJ_TN == 0 else 1024
        proj, (w2_bf,) = _inproj(x2d, mod3, norm1_gain[l][None, :], w_in[l].astype(BF16),
                                 [w2[l].reshape(n_exp * d_exp, D)], S, _tile(S, 1024), tn)
        r, (w1_bf,) = _retention(posf.reshape(B, S, 1), proj.reshape(B, S, -1),
                                 [w1[l].reshape(n_exp * D, d_exp)])
        r = r.reshape(T, -1)
        w1_bf = w1_bf.reshape(n_exp, D, d_exp)
        w2_bf = w2_bf.reshape(n_exp, d_exp, D)

        w_rt = jnp.concatenate(
            [w_group[l], w_router[l],
             jnp.zeros((D, LANES - N_GROUPS - N_EXPERTS), F32)], axis=1)
        wr_hi = w_rt.astype(BF16)
        wr_lo = (w_rt - wr_hi.astype(F32)).astype(BF16)
        wr_cat = jnp.concatenate([wr_hi, wr_lo], axis=1)
        brt = jnp.concatenate(
            [b_group[l], b_router[l], jnp.zeros((LANES - N_GROUPS - N_EXPERTS,), F32)])[None, :]
        x1, h2, logits, w3_bf = _mix(
            proj, r, x2d, mod3, w_pool[l].astype(BF16), pool_scale[l][None, :],
            w_branch_pool[l].astype(BF16), w_branch_ret[l].astype(BF16), w_out[l].astype(BF16),
            norm2_gain[l][None, :], wr_cat, brt, [w3[l].reshape(n_exp * D, d_exp)], S,
            _tile(S, 256))
        w3_bf = w3_bf.reshape(n_exp, D, d_exp)

        col, row, cnt = _route(logits, _tile(T, 512))
        counts = cnt[0, ROUTE_LANE0:ROUTE_LANE0 + N_EXPERTS].astype(jnp.int32)
        padded = (counts + EXPERT_ROWS - 1) // EXPERT_ROWS * EXPERT_ROWS
        pend = jnp.cumsum(padded)
        pstart = pend - padded
        n_rows = 2 * T + N_EXPERTS * EXPERT_ROWS
        nb = n_rows // EXPERT_ROWS
        n_used = (pend[-1:] // EXPERT_ROWS).astype(jnp.int32)
        block_row0 = jnp.arange(nb, dtype=jnp.int32) * EXPERT_ROWS
        block_e = jnp.minimum(
            jnp.sum((pend[None, :] <= block_row0[:, None]).astype(jnp.int32), axis=1),
            N_EXPERTS - 1)

        dest2 = _dest(pstart, row, _tile(T, 4096))[:2]
        tm_d = _tile(T, 1024)
        tm_f = _tile(S, 512)
        buf = _dispatch(pstart, pend, n_used, _tile_major(dest2, tm_d),
                        h2.reshape(T // SUBLANES, SUBLANES, D), n_rows, tm_d)
        yb = _experts(block_e, n_used, buf.reshape(n_rows, D), w1_bf, w3_bf, w2_bf)
        out = _final(_tile_major(dest2, tm_f), col.reshape(T // SUBLANES, SUBLANES, LANES),
                     x1.reshape(T // SUBLANES, SUBLANES, D), mod3, final_gain[None, :],
                     yb.reshape(n_rows // SUBLANES, SUBLANES, D), S, tm_f)
    return out.reshape(B, S, D)
```
